```python
import math
import jax, jax.numpy as jnp
from jax import lax
import numpy as np

D_MODEL = 1024
BATCH = 2
SEQ = 8192
DEPTH = 1

PLE_DIM = 256
D_MIX = D_MODEL
RET_HEADS = 4
RET_HEAD_DIM = 128
RET_WIDTH = RET_HEADS * RET_HEAD_DIM
RET_CHUNK = 128
MLA_HEADS = 8
MLA_NOPE_DIM = 64
MLA_ROPE_DIM = 32
MLA_QK_DIM = MLA_NOPE_DIM + MLA_ROPE_DIM
MLA_V_DIM = 64
MLA_WIDTH = MLA_HEADS * MLA_V_DIM
MLA_Q_LORA = 384
MLA_KV_LORA = 256
Q_BLOCK = 128
IN_COLS = 4 * RET_WIDTH + MLA_Q_LORA + MLA_KV_LORA + MLA_ROPE_DIM
D_FF = ((8 * D_MODEL // 3 + 255) // 256) * 256
ROPE_BASE = 10000.0
EPS = 1e-6

kernel_name = "hybrid_retention_mla_parallel_heads"


def rmsnorm(x, w):
    xf = x.astype(jnp.float32)
    y = xf * lax.rsqrt(jnp.mean(xf * xf, axis=-1, keepdims=True) + EPS)
    return (y * w.astype(jnp.float32)).astype(x.dtype)


def head_groupnorm(x, w):
    xf = x.astype(jnp.float32)
    mu = jnp.mean(xf, axis=-1, keepdims=True)
    var = jnp.mean(jnp.square(xf - mu), axis=-1, keepdims=True)
    y = (xf - mu) * lax.rsqrt(var + EPS)
    B, S, H, d = x.shape
    return (y.reshape(B, S, H * d) * w.astype(jnp.float32)).astype(x.dtype)


def rope(x, positions):
    d = x.shape[-1]
    half = d // 2
    inv = 1.0 / (ROPE_BASE ** (jnp.arange(half, dtype=jnp.float32) / half))
    ang = positions.astype(jnp.float32)[..., None] * inv
    cos = jnp.cos(ang)[:, :, None, :].astype(x.dtype)
    sin = jnp.sin(ang)[:, :, None, :].astype(x.dtype)
    x1, x2 = x[..., :half], x[..., half:]
    return jnp.concatenate([x1 * cos - x2 * sin, x2 * cos + x1 * sin], axis=-1)


def retention_chunkwise(q, k, v):
    B, S, H, d = q.shape
    C = RET_CHUNK
    N = S // C
    dt = q.dtype
    log_g = jnp.log(1.0 - 2.0 ** (-5.0 - jnp.arange(H, dtype=jnp.float32)))
    j = jnp.arange(C, dtype=jnp.float32)
    diff = j[:, None] - j[None, :]
    D = jnp.where(diff[None] >= 0, jnp.exp(jnp.maximum(diff, 0.0)[None] * log_g[:, None, None]), 0.0).astype(dt)
    zeta = jnp.exp((C - 1 - j)[None, :] * log_g[:, None]).astype(dt)
    xi = jnp.exp((j + 1)[None, :] * log_g[:, None]).astype(dt)
    g_chunk = jnp.exp(C * log_g).astype(dt)

    qc = q.reshape(B, N, C, H, d)
    kc = k.reshape(B, N, C, H, d)
    vc = v.reshape(B, N, C, H, d)
    scores = jnp.einsum('bnchd,bnmhd->bnhcm', qc, kc) * D[None, None]
    inner = jnp.einsum('bnhcm,bnmhe->bnche', scores, vc)
    U = jnp.einsum('bnmhd,bnmhe,hm->nbhde', kc, vc, zeta)

    def step(R, u):
        return g_chunk[None, :, None, None] * R + u, R

    _, R_prev = lax.scan(step, jnp.zeros_like(U[0]), U)
    cross = jnp.einsum('bnchd,nbhde->bnche', qc, R_prev) * xi.T[None, None, :, :, None]
    return (inner + cross).reshape(B, S, H, d)


def mla_causal(q, k, v):
    B, S, H, dqk = q.shape
    dv = v.shape[-1]
    NB = S // Q_BLOCK
    scale = 1.0 / math.sqrt(dqk)
    kt = k.transpose(0, 2, 1, 3)
    vt = v.transpose(0, 2, 1, 3)
    qb = q.reshape(B, NB, Q_BLOCK, H, dqk).transpose(1, 0, 3, 2, 4)
    kpos = jnp.arange(S)

    def block(args):
        qi, bi = args
        s = jnp.einsum('bhqd,bhkd->bhqk', qi, kt).astype(jnp.float32) * scale
        qpos = bi * Q_BLOCK + jnp.arange(Q_BLOCK)
        mask = kpos[None, :] <= qpos[:, None]
        s = jnp.where(mask[None, None], s, -1e30)
        pr = jax.nn.softmax(s, axis=-1).astype(vt.dtype)
        return jnp.einsum('bhqk,bhkd->bhqd', pr, vt)

    out = lax.map(block, (qb, jnp.arange(NB)))
    return out.transpose(1, 0, 3, 2, 4).reshape(B, S, H * dv)


def token_mixer(xn, positions, w_in, ret_gn_w, mla_q_norm, w_uq, mla_kv_norm, w_ukv, w_o):
    B, S, _ = xn.shape
    proj = xn @ w_in
    o = 0
    rq = proj[..., o:o + RET_WIDTH]; o += RET_WIDTH
    rk = proj[..., o:o + RET_WIDTH]; o += RET_WIDTH
    rv = proj[..., o:o + RET_WIDTH]; o += RET_WIDTH
    rg = proj[..., o:o + RET_WIDTH]; o += RET_WIDTH
    cq = proj[..., o:o + MLA_Q_LORA]; o += MLA_Q_LORA
    ckv = proj[..., o:o + MLA_KV_LORA]; o += MLA_KV_LORA
    kr = proj[..., o:o + MLA_ROPE_DIM]

    shp = (B, S, RET_HEADS, RET_HEAD_DIM)
    rq = rope(rq.reshape(shp), positions)
    rk = rope(rk.reshape(shp), positions) * (RET_HEAD_DIM ** -0.5)
    ry = retention_chunkwise(rq, rk, rv.reshape(shp))
    ret_out = jax.nn.silu(rg) * head_groupnorm(ry, ret_gn_w)

    qh = (rmsnorm(cq, mla_q_norm) @ w_uq).reshape(B, S, MLA_HEADS, MLA_QK_DIM)
    q = jnp.concatenate([qh[..., :MLA_NOPE_DIM], rope(qh[..., MLA_NOPE_DIM:], positions)], axis=-1)
    kvh = (rmsnorm(ckv, mla_kv_norm) @ w_ukv).reshape(B, S, MLA_HEADS, MLA_NOPE_DIM + MLA_V_DIM)
    k_rope = rope(kr[:, :, None, :], positions)
    k = jnp.concatenate([kvh[..., :MLA_NOPE_DIM],
                         jnp.broadcast_to(k_rope, (B, S, MLA_HEADS, MLA_ROPE_DIM))], axis=-1)
    v = kvh[..., MLA_NOPE_DIM:]
    mla_out = mla_causal(q, k, v)

    return jnp.concatenate([ret_out, mla_out], axis=-1) @ w_o


def setup_inputs(seed: int = 0) -> dict:
    key = jax.random.key(seed)
    ks = jax.random.split(key, 24)
    L = DEPTH

    def nrm(k, shape, fan_in):
        return jax.random.normal(k, shape, jnp.float32) * (fan_in ** -0.5)

    def gain(k, shape):
        return 1.0 + 0.05 * jax.random.normal(k, shape, jnp.float32)

    return {
        "x": jax.random.normal(ks[0], (BATCH, SEQ, D_MODEL), jnp.float32),
        "p": jax.random.normal(ks[1], (DEPTH, BATCH, SEQ, PLE_DIM), jnp.float32),
        "positions": jnp.broadcast_to(jnp.arange(SEQ, dtype=jnp.int32)[None], (BATCH, SEQ)),
        "pre_mix_norm": gain(ks[2], (L, D_MODEL)),
        "w_in": nrm(ks[3], (L, D_MODEL, IN_COLS), D_MODEL),
        "ret_gn_w": gain(ks[4], (L, RET_WIDTH)),
        "mla_q_norm": gain(ks[5], (L, MLA_Q_LORA)),
        "w_uq": nrm(ks[6], (L, MLA_Q_LORA, MLA_HEADS * MLA_QK_DIM), MLA_Q_LORA),
        "mla_kv_norm": gain(ks[7], (L, MLA_KV_LORA)),
        "w_ukv": nrm(ks[8], (L, MLA_KV_LORA, MLA_HEADS * (MLA_NOPE_DIM + MLA_V_DIM)), MLA_KV_LORA),
        "w_o": nrm(ks[9], (L, D_MIX, D_MODEL), D_MIX),
        "post_mix_norm": gain(ks[10], (L, D_MODEL)),
        "pre_ffn_norm": gain(ks[11], (L, D_MODEL)),
        "w_gate": nrm(ks[12], (L, D_MODEL, D_FF), D_MODEL),
        "w_up": nrm(ks[13], (L, D_MODEL, D_FF), D_MODEL),
        "w_down": nrm(ks[14], (L, D_FF, D_MODEL), D_FF),
        "post_ffn_norm": gain(ks[15], (L, D_MODEL)),
        "w_ple_proj": nrm(ks[16], (L, PLE_DIM, D_MODEL), PLE_DIM),
        "ple_norm": gain(ks[17], (L, D_MODEL)),
        "w_ple_gate": nrm(ks[18], (L, D_MODEL, D_MODEL), D_MODEL),
        "b_ple_gate": 0.02 * jax.random.normal(ks[19], (L, D_MODEL), jnp.float32),
    }


def reference(x, p, positions, pre_mix_norm, w_in, ret_gn_w, mla_q_norm, w_uq, mla_kv_norm,
              w_ukv, w_o, post_mix_norm, pre_ffn_norm, w_gate, w_up, w_down, post_ffn_norm,
              w_ple_proj, ple_norm, w_ple_gate, b_ple_gate):
    h = x
    for i in range(DEPTH):
        xn = rmsnorm(h, pre_mix_norm[i])
        mix = token_mixer(xn, positions, w_in[i], ret_gn_w[i], mla_q_norm[i], w_uq[i],
                          mla_kv_norm[i], w_ukv[i], w_o[i])
        h = h + rmsnorm(mix, post_mix_norm[i])
        hn = rmsnorm(h, pre_ffn_norm[i])
        ff = (jax.nn.silu(hn @ w_gate[i]) * (hn @ w_up[i])) @ w_down[i]
        h = h + rmsnorm(ff, post_ffn_norm[i])
        e = rmsnorm(p[i] @ w_ple_proj[i], ple_norm[i])
        gate = jax.nn.sigmoid(h @ w_ple_gate[i] + b_ple_gate[i])
        h = h + e * gate
    return h
```

```python
import math
from functools import partial

import jax
import jax.numpy as jnp
from jax import lax
from jax.experimental import pallas as pl
from jax.experimental.pallas import tpu as pltpu

D_MODEL = 1024
PLE_DIM = 256
RET_HEADS = 4
RET_HEAD_DIM = 128
RET_WIDTH = RET_HEADS * RET_HEAD_DIM
MLA_HEADS = 8
MLA_NOPE_DIM = 64
MLA_ROPE_DIM = 32
MLA_QK_DIM = MLA_NOPE_DIM + MLA_ROPE_DIM
MLA_V_DIM = 64
MLA_WIDTH = MLA_HEADS * MLA_V_DIM
MLA_Q_LORA = 384
MLA_KV_LORA = 256
D_FF = 2816
ROPE_BASE = 10000.0
EPS = 1e-6
NEG_BIG = -1e30

LANES = 128
HEAD_SLAB = 128
IN_COLS_PAD = 4 * RET_WIDTH + MLA_Q_LORA + MLA_KV_LORA + LANES
VMEM_LIMIT = 56 * 1024 * 1024

TM_PROJ = 512
RET_CHUNK = 128
TM_RET = 512
TQ = 256
TK = TM_PROJ
ONES_ROWS = 16
TM_POST = 512
FF_CHUNK = 256

BF16 = jnp.bfloat16
F32 = jnp.float32


def _dot(a, b):
    return jnp.dot(a, b, preferred_element_type=F32)


def _dot_nt(a, b):
    return lax.dot_general(a, b, (((1,), (1,)), ((), ())), preferred_element_type=F32)


def _dot_tn(a, b):
    return lax.dot_general(a, b, (((0,), (0,)), ((), ())), preferred_element_type=F32)


def _rms(v, w):
    return v * lax.rsqrt(jnp.mean(v * v, axis=-1, keepdims=True) + EPS) * w


def _rope_tables_body(pos_ref, invr_ref, invm_ref, sgnr_ref, sgnm_ref, onesm_ref,
                      cr_ref, sr_ref, cm_ref, sm_ref):
    pos = pos_ref[0].astype(F32)
    ang_r = pos * invr_ref[...]
    cr_ref[0] = jnp.cos(ang_r)
    sr_ref[0] = jnp.sin(ang_r) * sgnr_ref[...]
    ang_m = pos * invm_ref[...]
    cm_ref[0] = jnp.cos(ang_m) * onesm_ref[...]
    sm_ref[0] = jnp.sin(ang_m) * sgnm_ref[...]


def _rope_tables(positions):
    B, S = positions.shape
    tm = 1024
    half_r = RET_HEAD_DIM // 2
    inv_r = 1.0 / (ROPE_BASE ** (jnp.arange(half_r, dtype=F32) / half_r))
    inv_r = jnp.concatenate([inv_r, inv_r])[None, :]
    sgn_r = jnp.concatenate([-jnp.ones((half_r,), F32), jnp.ones((half_r,), F32)])[None, :]
    half_m = MLA_ROPE_DIM // 2
    inv_m = 1.0 / (ROPE_BASE ** (jnp.arange(half_m, dtype=F32) / half_m))
    z64 = jnp.zeros((MLA_NOPE_DIM,), F32)
    z32 = jnp.zeros((HEAD_SLAB - MLA_QK_DIM,), F32)
    inv_m = jnp.concatenate([z64, inv_m, inv_m, z32])[None, :]
    sgn_m = jnp.concatenate([z64, -jnp.ones((half_m,), F32), jnp.ones((half_m,), F32), z32])[None, :]
    ones_m = jnp.concatenate([jnp.ones((MLA_QK_DIM,), F32), z32])[None, :]
    pos3 = positions.reshape(B, S, 1)
    vec = pl.BlockSpec((1, LANES), lambda b, i: (0, 0))
    tab = pl.BlockSpec((1, tm, LANES), lambda b, i: (b, i, 0))
    out = jax.ShapeDtypeStruct((B, S, LANES), F32)
    return pl.pallas_call(
        _rope_tables_body,
        grid=(B, S // tm),
        in_specs=[pl.BlockSpec((1, tm, 1), lambda b, i: (b, i, 0)), vec, vec, vec, vec, vec],
        out_specs=[tab, tab, tab, tab],
        out_shape=[out, out, out, out],
        compiler_params=pltpu.CompilerParams(dimension_semantics=("arbitrary", "arbitrary"),
                                             vmem_limit_bytes=VMEM_LIMIT),
        name="rope_tables",
    )(pos3, inv_r, inv_m, sgn_r, sgn_m, ones_m)


def _in_proj_body(x_ref, g_ref, win_ref, qn_ref, wuq_ref, kvn_ref, wuk_ref, wvt_ref,
                  cr_ref, sr_ref, cm_ref, sm_ref,
                  rq_ref, rk_ref, rv_ref, rg_ref, q_ref, k_ref, vt_ref, *, q_scale):
    xn = _rms(x_ref[0], g_ref[...]).astype(BF16)
    cr = cr_ref[0]
    sr = sr_ref[0]

    def ret_rope(v):
        return v * cr + pltpu.roll(v, RET_HEAD_DIM // 2, 1) * sr

    pq = _dot(xn, win_ref[:, 0:RET_WIDTH])
    pk = _dot(xn, win_ref[:, RET_WIDTH:2 * RET_WIDTH])
    for h in range(RET_HEADS):
        sl = slice(h * RET_HEAD_DIM, (h + 1) * RET_HEAD_DIM)
        rq_ref[0, :, sl] = ret_rope(pq[:, sl]).astype(BF16)
        rk_ref[0, :, sl] = (ret_rope(pk[:, sl]) * (RET_HEAD_DIM ** -0.5)).astype(BF16)
    rv_ref[0] = _dot(xn, win_ref[:, 2 * RET_WIDTH:3 * RET_WIDTH]).astype(BF16)
    rg_ref[0] = _dot(xn, win_ref[:, 3 * RET_WIDTH:4 * RET_WIDTH]).astype(BF16)

    cm = cm_ref[0]
    sm = sm_ref[0]
    lane = lax.broadcasted_iota(jnp.int32, cm.shape, 1)
    upper = lane >= (MLA_NOPE_DIM + MLA_ROPE_DIM // 2)

    def mla_rope(v):
        swapped = jnp.where(upper, pltpu.roll(v, MLA_ROPE_DIM // 2, 1),
                            pltpu.roll(v, LANES - MLA_ROPE_DIM // 2, 1))
        return v * cm + swapped * sm

    o = 4 * RET_WIDTH
    cq = _dot(xn, win_ref[:, o:o + MLA_Q_LORA])
    cqn = _rms(cq, qn_ref[...]).astype(BF16)
    qh = _dot(cqn, wuq_ref[...])
    for h in range(MLA_HEADS):
        sl = slice(h * HEAD_SLAB, (h + 1) * HEAD_SLAB)
        q_ref[0, :, sl] = (mla_rope(qh[:, sl]) * q_scale).astype(BF16)

    o += MLA_Q_LORA
    ckv = _dot(xn, win_ref[:, o:o + MLA_KV_LORA])
    ckvn = _rms(ckv, kvn_ref[...]).astype(BF16)
    o += MLA_KV_LORA
    kr = mla_rope(_dot(xn, win_ref[:, o:o + LANES]))
    kn = _dot(ckvn, wuk_ref[...])
    for h in range(MLA_HEADS):
        sl = slice(h * HEAD_SLAB, (h + 1) * HEAD_SLAB)
        k_ref[0, :, sl] = (kn[:, sl] + kr).astype(BF16)
    vt_ref[0, 0] = _dot_nt(wvt_ref[...], ckvn).astype(BF16)


def _in_proj(x, pre_mix_norm, w_in_p, q_norm, wuq_p, kv_norm, wuk_p, wvt, tables):
    B, S, D = x.shape
    tm = TM_PROJ
    cr, sr, cm, sm = tables
    q_scale = (1.0 / math.sqrt(MLA_QK_DIM)) * math.log2(math.e)
    const = lambda shape: pl.BlockSpec(shape, lambda b, i: (0,) * len(shape))
    tile = lambda w: pl.BlockSpec((1, tm, w), lambda b, i: (b, i, 0))
    bf = lambda w: jax.ShapeDtypeStruct((B, S, w), BF16)
    return pl.pallas_call(
        partial(_in_proj_body, q_scale=q_scale),
        grid=(B, S // tm),
        in_specs=[tile(D), const((1, D)), const((D, IN_COLS_PAD)),
                  const((1, MLA_Q_LORA)), const((MLA_Q_LORA, MLA_HEADS * HEAD_SLAB)),
                  const((1, MLA_KV_LORA)), const((MLA_KV_LORA, MLA_HEADS * HEAD_SLAB)),
                  const((MLA_WIDTH, MLA_KV_LORA)),
                  tile(LANES), tile(LANES), tile(LANES), tile(LANES)],
        out_specs=[tile(RET_WIDTH), tile(RET_WIDTH), tile(RET_WIDTH), tile(RET_WIDTH),
                   tile(MLA_HEADS * HEAD_SLAB), tile(MLA_HEADS * HEAD_SLAB),
                   pl.BlockSpec((1, 1, MLA_WIDTH, tm), lambda b, i: (b, i, 0, 0))],
        out_shape=[bf(RET_WIDTH), bf(RET_WIDTH), bf(RET_WIDTH), bf(RET_WIDTH),
                   bf(MLA_HEADS * HEAD_SLAB), bf(MLA_HEADS * HEAD_SLAB),
                   jax.ShapeDtypeStruct((B, S // tm, MLA_WIDTH, tm), BF16)],
        compiler_params=pltpu.CompilerParams(dimension_semantics=("arbitrary", "arbitrary"),
                                             vmem_limit_bytes=VMEM_LIMIT),
        name="in_proj",
    )(x, pre_mix_norm, w_in_p, q_norm, wuq_p, kv_norm, wuk_p, wvt, cr, sr, cm, sm)


def _retention_body(rq_ref, rk_ref, rv_ref, rg_ref, dmask_ref, zeta_ref, xi_ref, gnw_ref,
                    o_ref, state_ref, *, g_chunk):
    @pl.when(pl.program_id(1) == 0)
    def _():
        state_ref[...] = jnp.zeros_like(state_ref)

    C = RET_CHUNK
    for c in range(TM_RET // C):
        rows = slice(c * C, (c + 1) * C)
        for h in range(RET_HEADS):
            cols = slice(h * RET_HEAD_DIM, (h + 1) * RET_HEAD_DIM)
            q = rq_ref[0, rows, cols]
            k = rk_ref[0, rows, cols]
            v = rv_ref[0, rows, cols]
            scores = _dot_nt(q, k) * dmask_ref[h]
            inner = _dot(scores.astype(BF16), v)
            r_prev = state_ref[h]
            cross = _dot(q, r_prev.astype(BF16)) * xi_ref[h]
            kz = (k.astype(F32) * zeta_ref[h]).astype(BF16)
            state_ref[h] = g_chunk[h] * r_prev + _dot_tn(kz, v)
            y = inner + cross
            mu = jnp.mean(y, axis=-1, keepdims=True)
            yc = y - mu
            var = jnp.mean(yc * yc, axis=-1, keepdims=True)
            yn = yc * lax.rsqrt(var + EPS) * gnw_ref[:, cols]
            gate = rg_ref[0, rows, cols].astype(F32)
            o_ref[0, rows, cols] = (gate * jax.nn.sigmoid(gate) * yn).astype(BF16)


def _retention(rq, rk, rv, rg, ret_gn_w):
    B, S, W = rq.shape
    C = RET_CHUNK
    H = RET_HEADS
    log_g = jnp.log(1.0 - 2.0 ** (-5.0 - jnp.arange(H, dtype=F32)))
    j = jnp.arange(C, dtype=F32)
    diff = j[:, None] - j[None, :]
    dmask = jnp.where(diff[None] >= 0, jnp.exp(jnp.maximum(diff, 0.0)[None] * log_g[:, None, None]), 0.0)
    zeta = jnp.exp((C - 1 - j)[None, :] * log_g[:, None])
    xi = jnp.exp((j + 1)[None, :] * log_g[:, None])
    zeta_b = jnp.broadcast_to(zeta[:, :, None], (H, C, RET_HEAD_DIM))
    xi_b = jnp.broadcast_to(xi[:, :, None], (H, C, RET_HEAD_DIM))
    g_chunk = tuple(float((1.0 - 2.0 ** (-5.0 - h)) ** C) for h in range(H))
    tile = pl.BlockSpec((1, TM_RET, W), lambda b, i: (b, i, 0))
    const3 = pl.BlockSpec((H, C, C), lambda b, i: (0, 0, 0))
    return pl.pallas_call(
        partial(_retention_body, g_chunk=g_chunk),
        grid=(B, S // TM_RET),
        in_specs=[tile, tile, tile, tile, const3, const3, const3,
                  pl.BlockSpec((1, W), lambda b, i: (0, 0))],
        out_specs=tile,
        out_shape=jax.ShapeDtypeStruct((B, S, W), BF16),
        scratch_shapes=[pltpu.VMEM((H, RET_HEAD_DIM, RET_HEAD_DIM), F32)],
        compiler_params=pltpu.CompilerParams(dimension_semantics=("arbitrary", "arbitrary"),
                                             vmem_limit_bytes=VMEM_LIMIT),
        name="retention",
    )(rq, rk, rv, rg, dmask, zeta_b, xi_b, ret_gn_w)


def _attn_body(q_ref, k_ref, vt_ref, o_ref):
    qi = pl.program_id(2)
    n_full = (qi * TQ) // TK
    ones = jnp.ones((ONES_ROWS, TK), BF16)
    outs = []
    for hh in range(2):
        cols = slice(hh * HEAD_SLAB, (hh + 1) * HEAD_SLAB)
        vrows = slice(hh * MLA_V_DIM, (hh + 1) * MLA_V_DIM)
        q = q_ref[0, :, cols]

        def block(j, carry, masked):
            m, acc = carry
            start = pl.multiple_of(j * TK, TK)
            kb = k_ref[0, pl.ds(start, TK), cols]
            st = _dot_nt(kb, q)
            if masked:
                kpos = start + lax.broadcasted_iota(jnp.int32, (TK, TQ), 0)
                qpos = qi * TQ + lax.broadcasted_iota(jnp.int32, (TK, TQ), 1)
                st = jnp.where(kpos <= qpos, st, NEG_BIG)
            m_new = jnp.maximum(m, jnp.max(st, axis=0, keepdims=True))
            alpha = jnp.exp2(m - m_new)
            p = jnp.exp2(st - m_new).astype(BF16)
            vte = jnp.concatenate([vt_ref[0, j, vrows, :], ones], axis=0)
            return m_new, acc * alpha + _dot(vte, p)

        init = (jnp.full((1, TQ), NEG_BIG, F32), jnp.zeros((MLA_V_DIM + ONES_ROWS, TQ), F32))
        carry = lax.fori_loop(0, n_full, partial(block, masked=False), init)
        _, acc = block(n_full, carry, True)
        o_t = acc[:MLA_V_DIM] / acc[MLA_V_DIM:MLA_V_DIM + 1]
        outs.append(o_t.T)
    o_ref[0] = jnp.concatenate(outs, axis=1).astype(BF16)


def _mla_attention(q, k, vt):
    B, S, _ = q.shape
    nkb = S // TK
    return pl.pallas_call(
        _attn_body,
        grid=(B, MLA_HEADS // 2, S // TQ),
        in_specs=[pl.BlockSpec((1, TQ, 2 * HEAD_SLAB), lambda b, g, i: (b, i, g)),
                  pl.BlockSpec((1, S, 2 * HEAD_SLAB), lambda b, g, i: (b, 0, g)),
                  pl.BlockSpec((1, nkb, 2 * MLA_V_DIM, TK), lambda b, g, i: (b, 0, g, 0))],
        out_specs=pl.BlockSpec((1, TQ, 2 * MLA_V_DIM), lambda b, g, i: (b, i, g)),
        out_shape=jax.ShapeDtypeStruct((B, S, MLA_WIDTH), BF16),
        compiler_params=pltpu.CompilerParams(dimension_semantics=("arbitrary", "arbitrary", "arbitrary"),
                                             vmem_limit_bytes=VMEM_LIMIT),
        name="mla_attn",
    )(q, k, vt)


def _post_body(ret_ref, mla_ref, x_ref, p_ref, wo_ref, pmn_ref, pfn_ref, wg_ref, wu_ref, wd_ref,
               pofn_ref, wpp_ref, plen_ref, wpg_ref, bpg_ref, o_ref, act_ref):
    mix = _dot(ret_ref[0], wo_ref[0:RET_WIDTH, :]) + _dot(mla_ref[0], wo_ref[RET_WIDTH:, :])
    h1 = x_ref[0] + _rms(mix, pmn_ref[...])
    hn = _rms(h1, pfn_ref[...]).astype(BF16)
    for c in range(D_FF // FF_CHUNK):
        cols = slice(c * FF_CHUNK, (c + 1) * FF_CHUNK)
        g = _dot(hn, wg_ref[:, cols])
        u = _dot(hn, wu_ref[:, cols])
        act_ref[:, cols] = (g * jax.nn.sigmoid(g) * u).astype(BF16)
    ff = _dot(act_ref[...], wd_ref[...])
    h2 = h1 + _rms(ff, pofn_ref[...])
    e = _rms(_dot(p_ref[0].astype(BF16), wpp_ref[...]), plen_ref[...])
    gate = jax.nn.sigmoid(_dot(h2.astype(BF16), wpg_ref[...]) + bpg_ref[...])
    o_ref[0] = h2 + e * gate


def _post(ret_out, mla_out, x, p, w_o, post_mix_norm, pre_ffn_norm, w_gate, w_up, w_down,
          post_ffn_norm, w_ple_proj, ple_norm, w_ple_gate, b_ple_gate):
    B, S, D = x.shape
    tm = TM_POST
    const = lambda shape: pl.BlockSpec(shape, lambda b, i: (0,) * len(shape),
                                       pipeline_mode=pl.Buffered(1))
    tile = lambda w: pl.BlockSpec((1, tm, w), lambda b, i: (b, i, 0))
    return pl.pallas_call(
        _post_body,
        grid=(B, S // tm),
        in_specs=[tile(RET_WIDTH), tile(MLA_WIDTH), tile(D), tile(PLE_DIM),
                  const((D, D)), const((1, D)), const((1, D)),
                  const((D, D_FF)), const((D, D_FF)), const((D_FF, D)), const((1, D)),
                  const((PLE_DIM, D)), const((1, D)), const((D, D)), const((1, D))],
        out_specs=tile(D),
        out_shape=jax.ShapeDtypeStruct((B, S, D), F32),
        scratch_shapes=[pltpu.VMEM((tm, D_FF), BF16)],
        compiler_params=pltpu.CompilerParams(dimension_semantics=("arbitrary", "arbitrary"),
                                             vmem_limit_bytes=VMEM_LIMIT),
        name="post",
    )(ret_out, mla_out, x, p, w_o, post_mix_norm, pre_ffn_norm, w_gate, w_up, w_down,
      post_ffn_norm, w_ple_proj, ple_norm, w_ple_gate, b_ple_gate)


def _prep_w_in(w_in):
    base = 4 * RET_WIDTH + MLA_Q_LORA + MLA_KV_LORA
    kr = w_in[:, base:base + MLA_ROPE_DIM]
    z = jnp.zeros((w_in.shape[0], MLA_NOPE_DIM), w_in.dtype)
    zt = jnp.zeros((w_in.shape[0], LANES - MLA_QK_DIM), w_in.dtype)
    return jnp.concatenate([w_in[:, :base], z, kr, zt], axis=1).astype(BF16)


def _pad_heads(w, per_head, keep):
    K = w.shape[0]
    w = w.reshape(K, MLA_HEADS, per_head)[:, :, :keep]
    w = jnp.pad(w, ((0, 0), (0, 0), (0, HEAD_SLAB - keep)))
    return w.reshape(K, MLA_HEADS * HEAD_SLAB).astype(BF16)


def kernel(x, p, positions, pre_mix_norm, w_in, ret_gn_w, mla_q_norm, w_uq, mla_kv_norm, w_ukv, w_o,
           post_mix_norm, pre_ffn_norm, w_gate, w_up, w_down, post_ffn_norm, w_ple_proj, ple_norm,
           w_ple_gate, b_ple_gate):
    depth = pre_mix_norm.shape[0]
    tables = _rope_tables(positions)
    h = x
    for i in range(depth):
        wuq_p = _pad_heads(w_uq[i], MLA_QK_DIM, MLA_QK_DIM)
        wuk_p = _pad_heads(w_ukv[i], MLA_NOPE_DIM + MLA_V_DIM, MLA_NOPE_DIM)
        wv = w_ukv[i].reshape(MLA_KV_LORA, MLA_HEADS, MLA_NOPE_DIM + MLA_V_DIM)[:, :, MLA_NOPE_DIM:]
        wvt = wv.reshape(MLA_KV_LORA, MLA_WIDTH).T.astype(BF16)
        rq, rk, rv, rg, q, k, vt = _in_proj(h, pre_mix_norm[i][None], _prep_w_in(w_in[i]),
                                            mla_q_norm[i][None], wuq_p, mla_kv_norm[i][None], wuk_p, wvt,
                                            tables)
        ret_out = _retention(rq, rk, rv, rg, ret_gn_w[i][None])
        mla_out = _mla_attention(q, k, vt)
        h = _post(ret_out, mla_out, h, p[i], w_o[i].astype(BF16), post_mix_norm[i][None],
                  pre_ffn_norm[i][None], w_gate[i].astype(BF16), w_up[i].astype(BF16),
                  w_down[i].astype(BF16), post_ffn_norm[i][None], w_ple_proj[i].astype(BF16),
                  ple_norm[i][None], w_ple_gate[i].astype(BF16), b_ple_gate[i][None])
    return h
```

```python
import math
from functools import partial

import jax
import jax.numpy as jnp
from jax import lax
from jax.experimental import pallas as pl
from jax.experimental.pallas import tpu as pltpu

D_MODEL = 1024
PLE_DIM = 256
RET_HEADS = 4
RET_HEAD_DIM = 128
RET_WIDTH = RET_HEADS * RET_HEAD_DIM
MLA_HEADS = 8
MLA_NOPE_DIM = 64
MLA_ROPE_DIM = 32
MLA_QK_DIM = MLA_NOPE_DIM + MLA_ROPE_DIM
MLA_V_DIM = 64
MLA_WIDTH = MLA_HEADS * MLA_V_DIM
MLA_Q_LORA = 384
MLA_KV_LORA = 256
D_FF = 2816
ROPE_BASE = 10000.0
EPS = 1e-6
NEG_BIG = -1e30

LANES = 128
HEAD_SLAB = 128
IN_COLS_PAD = 4 * RET_WIDTH + MLA_Q_LORA + MLA_KV_LORA + LANES
VMEM_LIMIT = 56 * 1024 * 1024

TM_PROJ = 512
RET_CHUNK = 128
TM_RET = 512
TQ = 512
TK = TQ // 2
ONES_ROWS = 16
TM_POST = 512
FF_CHUNK = 256

BF16 = jnp.bfloat16
F32 = jnp.float32


def _dot(a, b):
    return jnp.dot(a, b, preferred_element_type=F32)


def _dot_nt(a, b):
    return lax.dot_general(a, b, (((1,), (1,)), ((), ())), preferred_element_type=F32)


def _dot_tn(a, b):
    return lax.dot_general(a, b, (((0,), (0,)), ((), ())), preferred_element_type=F32)


def _rms(v, w):
    return v * lax.rsqrt(jnp.mean(v * v, axis=-1, keepdims=True) + EPS) * w


def _rope_tables_body(pos_ref, invr_ref, invm_ref, sgnr_ref, sgnm_ref, onesm_ref,
                      cr_ref, sr_ref, cm_ref, sm_ref):
    pos = pos_ref[0].astype(F32)
    ang_r = pos * invr_ref[...]
    cr_ref[0] = jnp.cos(ang_r)
    sr_ref[0] = jnp.sin(ang_r) * sgnr_ref[...]
    ang_m = pos * invm_ref[...]
    cm_ref[0] = jnp.cos(ang_m) * onesm_ref[...]
    sm_ref[0] = jnp.sin(ang_m) * sgnm_ref[...]


def _rope_tables(positions):
    B, S = positions.shape
    tm = 1024
    half_r = RET_HEAD_DIM // 2
    inv_r = 1.0 / (ROPE_BASE ** (jnp.arange(half_r, dtype=F32) / half_r))
    inv_r = jnp.concatenate([inv_r, inv_r])[None, :]
    sgn_r = jnp.concatenate([-jnp.ones((half_r,), F32), jnp.ones((half_r,), F32)])[None, :]
    half_m = MLA_ROPE_DIM // 2
    inv_m = 1.0 / (ROPE_BASE ** (jnp.arange(half_m, dtype=F32) / half_m))
    z64 = jnp.zeros((MLA_NOPE_DIM,), F32)
    z32 = jnp.zeros((HEAD_SLAB - MLA_QK_DIM,), F32)
    inv_m = jnp.concatenate([z64, inv_m, inv_m, z32])[None, :]
    sgn_m = jnp.concatenate([z64, -jnp.ones((half_m,), F32), jnp.ones((half_m,), F32), z32])[None, :]
    ones_m = jnp.concatenate([jnp.ones((MLA_QK_DIM,), F32), z32])[None, :]
    pos3 = positions.reshape(B, S, 1)
    vec = pl.BlockSpec((1, LANES), lambda b, i: (0, 0))
    tab = pl.BlockSpec((1, tm, LANES), lambda b, i: (b, i, 0))
    out = jax.ShapeDtypeStruct((B, S, LANES), F32)
    return pl.pallas_call(
        _rope_tables_body,
        grid=(B, S // tm),
        in_specs=[pl.BlockSpec((1, tm, 1), lambda b, i: (b, i, 0)), vec, vec, vec, vec, vec],
        out_specs=[tab, tab, tab, tab],
        out_shape=[out, out, out, out],
        compiler_params=pltpu.CompilerParams(dimension_semantics=("arbitrary", "arbitrary"),
                                             vmem_limit_bytes=VMEM_LIMIT),
        name="rope_tables",
    )(pos3, inv_r, inv_m, sgn_r, sgn_m, ones_m)


def _in_proj_body(x_ref, g_ref, win_ref, qn_ref, wuq_ref, kvn_ref, wuk_ref, wvt_ref,
                  cr_ref, sr_ref, cm_ref, sm_ref,
                  rq_ref, rk_ref, rv_ref, rg_ref, q_ref, k_ref, vt_ref, *, q_scale):
    xn = _rms(x_ref[0], g_ref[...]).astype(BF16)
    cr = cr_ref[0]
    sr = sr_ref[0]

    def ret_rope(v):
        return v * cr + pltpu.roll(v, RET_HEAD_DIM // 2, 1) * sr

    pq = _dot(xn, win_ref[:, 0:RET_WIDTH])
    pk = _dot(xn, win_ref[:, RET_WIDTH:2 * RET_WIDTH])
    for h in range(RET_HEADS):
        sl = slice(h * RET_HEAD_DIM, (h + 1) * RET_HEAD_DIM)
        rq_ref[0, :, sl] = ret_rope(pq[:, sl]).astype(BF16)
        rk_ref[0, :, sl] = (ret_rope(pk[:, sl]) * (RET_HEAD_DIM ** -0.5)).astype(BF16)
    rv_ref[0] = _dot(xn, win_ref[:, 2 * RET_WIDTH:3 * RET_WIDTH]).astype(BF16)
    rg_ref[0] = _dot(xn, win_ref[:, 3 * RET_WIDTH:4 * RET_WIDTH]).astype(BF16)

    cm = cm_ref[0]
    sm = sm_ref[0]
    lane = lax.broadcasted_iota(jnp.int32, cm.shape, 1)
    upper = lane >= (MLA_NOPE_DIM + MLA_ROPE_DIM // 2)

    def mla_rope(v):
        swapped = jnp.where(upper, pltpu.roll(v, MLA_ROPE_DIM // 2, 1),
                            pltpu.roll(v, LANES - MLA_ROPE_DIM // 2, 1))
        return v * cm + swapped * sm

    o = 4 * RET_WIDTH
    cq = _dot(xn, win_ref[:, o:o + MLA_Q_LORA])
    cqn = _rms(cq, qn_ref[...]).astype(BF16)
    qh = _dot(cqn, wuq_ref[...])
    for h in range(MLA_HEADS):
        sl = slice(h * HEAD_SLAB, (h + 1) * HEAD_SLAB)
        q_ref[0, :, sl] = (mla_rope(qh[:, sl]) * q_scale).astype(BF16)

    o += MLA_Q_LORA
    ckv = _dot(xn, win_ref[:, o:o + MLA_KV_LORA])
    ckvn = _rms(ckv, kvn_ref[...]).astype(BF16)
    o += MLA_KV_LORA
    kr = mla_rope(_dot(xn, win_ref[:, o:o + LANES]))
    kn = _dot(ckvn, wuk_ref[...])
    for h in range(MLA_HEADS):
        sl = slice(h * HEAD_SLAB, (h + 1) * HEAD_SLAB)
        k_ref[0, :, sl] = (kn[:, sl] + kr).astype(BF16)
    vt = _dot_nt(wvt_ref[...], ckvn).astype(BF16)
    for blk in range(TM_PROJ // TK):
        vt_ref[0, blk] = vt[:, blk * TK:(blk + 1) * TK]


def _in_proj(x, pre_mix_norm, w_in_p, q_norm, wuq_p, kv_norm, wuk_p, wvt, tables):
    B, S, D = x.shape
    tm = TM_PROJ
    cr, sr, cm, sm = tables
    q_scale = (1.0 / math.sqrt(MLA_QK_DIM)) * math.log2(math.e)
    const = lambda shape: pl.BlockSpec(shape, lambda b, i: (0,) * len(shape))
    tile = lambda w: pl.BlockSpec((1, tm, w), lambda b, i: (b, i, 0))
    bf = lambda w: jax.ShapeDtypeStruct((B, S, w), BF16)
    return pl.pallas_call(
        partial(_in_proj_body, q_scale=q_scale),
        grid=(B, S // tm),
        in_specs=[tile(D), const((1, D)), const((D, IN_COLS_PAD)),
                  const((1, MLA_Q_LORA)), const((MLA_Q_LORA, MLA_HEADS * HEAD_SLAB)),
                  const((1, MLA_KV_LORA)), const((MLA_KV_LORA, MLA_HEADS * HEAD_SLAB)),
                  const((MLA_WIDTH, MLA_KV_LORA)),
                  tile(LANES), tile(LANES), tile(LANES), tile(LANES)],
        out_specs=[tile(RET_WIDTH), tile(RET_WIDTH), tile(RET_WIDTH), tile(RET_WIDTH),
                   tile(MLA_HEADS * HEAD_SLAB), tile(MLA_HEADS * HEAD_SLAB),
                   pl.BlockSpec((1, tm // TK, MLA_WIDTH, TK), lambda b, i: (b, i, 0, 0))],
        out_shape=[bf(RET_WIDTH), bf(RET_WIDTH), bf(RET_WIDTH), bf(RET_WIDTH),
                   bf(MLA_HEADS * HEAD_SLAB), bf(MLA_HEADS * HEAD_SLAB),
                   jax.ShapeDtypeStruct((B, S // TK, MLA_WIDTH, TK), BF16)],
        compiler_params=pltpu.CompilerParams(dimension_semantics=("arbitrary", "arbitrary"),
                                             vmem_limit_bytes=VMEM_LIMIT),
        name="in_proj",
    )(x, pre_mix_norm, w_in_p, q_norm, wuq_p, kv_norm, wuk_p, wvt, cr, sr, cm, sm)


def _retention_body(rq_ref, rk_ref, rv_ref, rg_ref, dmask_ref, zeta_ref, xi_ref, gnw_ref,
                    o_ref, state_ref, *, g_chunk):
    @pl.when(pl.program_id(1) == 0)
    def _():
        state_ref[...] = jnp.zeros_like(state_ref)

    C = RET_CHUNK
    for c in range(TM_RET // C):
        rows = slice(c * C, (c + 1) * C)
        for h in range(RET_HEADS):
            cols = slice(h * RET_HEAD_DIM, (h + 1) * RET_HEAD_DIM)
            q = rq_ref[0, rows, cols]
            k = rk_ref[0, rows, cols]
            v = rv_ref[0, rows, cols]
            scores = _dot_nt(q, k) * dmask_ref[h]
            inner = _dot(scores.astype(BF16), v)
            r_prev = state_ref[h]
            cross = _dot(q, r_prev.astype(BF16)) * xi_ref[h]
            kz = (k.astype(F32) * zeta_ref[h]).astype(BF16)
            state_ref[h] = g_chunk[h] * r_prev + _dot_tn(kz, v)
            y = inner + cross
            mu = jnp.mean(y, axis=-1, keepdims=True)
            yc = y - mu
            var = jnp.mean(yc * yc, axis=-1, keepdims=True)
            yn = yc * lax.rsqrt(var + EPS) * gnw_ref[:, cols]
            gate = rg_ref[0, rows, cols].astype(F32)
            o_ref[0, rows, cols] = (gate * jax.nn.sigmoid(gate) * yn).astype(BF16)


def _retention(rq, rk, rv, rg, ret_gn_w):
    B, S, W = rq.shape
    C = RET_CHUNK
    H = RET_HEADS
    log_g = jnp.log(1.0 - 2.0 ** (-5.0 - jnp.arange(H, dtype=F32)))
    j = jnp.arange(C, dtype=F32)
    diff = j[:, None] - j[None, :]
    dmask = jnp.where(diff[None] >= 0, jnp.exp(jnp.maximum(diff, 0.0)[None] * log_g[:, None, None]), 0.0)
    zeta = jnp.exp((C - 1 - j)[None, :] * log_g[:, None])
    xi = jnp.exp((j + 1)[None, :] * log_g[:, None])
    zeta_b = jnp.broadcast_to(zeta[:, :, None], (H, C, RET_HEAD_DIM))
    xi_b = jnp.broadcast_to(xi[:, :, None], (H, C, RET_HEAD_DIM))
    g_chunk = tuple(float((1.0 - 2.0 ** (-5.0 - h)) ** C) for h in range(H))
    tile = pl.BlockSpec((1, TM_RET, W), lambda b, i: (b, i, 0))
    const3 = pl.BlockSpec((H, C, C), lambda b, i: (0, 0, 0))
    return pl.pallas_call(
        partial(_retention_body, g_chunk=g_chunk),
        grid=(B, S // TM_RET),
        in_specs=[tile, tile, tile, tile, const3, const3, const3,
                  pl.BlockSpec((1, W), lambda b, i: (0, 0))],
        out_specs=tile,
        out_shape=jax.ShapeDtypeStruct((B, S, W), BF16),
        scratch_shapes=[pltpu.VMEM((H, RET_HEAD_DIM, RET_HEAD_DIM), F32)],
        compiler_params=pltpu.CompilerParams(dimension_semantics=("arbitrary", "arbitrary"),
                                             vmem_limit_bytes=VMEM_LIMIT),
        name="retention",
    )(rq, rk, rv, rg, dmask, zeta_b, xi_b, ret_gn_w)


HEADS_PER_STEP = 2


def _attn_body(q_ref, k_ref, vt_ref, o_ref, s_ref, mb_ref):
    qi = pl.program_id(2)
    ones = jnp.ones((ONES_ROWS, TK), BF16)
    heads = range(HEADS_PER_STEP)

    def scores(j, slot):
        start = pl.multiple_of(j * TK, TK)
        for hh in heads:
            cols = slice(hh * HEAD_SLAB, (hh + 1) * HEAD_SLAB)
            st = _dot_nt(k_ref[0, pl.ds(start, TK), cols], q_ref[0, :, cols])
            s_ref[slot, hh] = st
            mb_ref[slot, hh] = jnp.max(st, axis=0, keepdims=True)

    def accumulate(j, slot, carry, masked):
        out = []
        for hh in heads:
            m, acc = carry[hh]
            st = s_ref[slot, hh]
            if masked:
                kpos = j * TK + lax.broadcasted_iota(jnp.int32, (TK, TQ), 0)
                qpos = qi * TQ + lax.broadcasted_iota(jnp.int32, (TK, TQ), 1)
                st = jnp.where(kpos <= qpos, st, NEG_BIG)
                blk_max = jnp.max(st, axis=0, keepdims=True)
            else:
                blk_max = mb_ref[slot, hh]
            m_new = jnp.maximum(m, blk_max)
            alpha = jnp.exp2(m - m_new)
            p = jnp.exp2(st - m_new).astype(BF16)
            vte = jnp.concatenate([vt_ref[0, j, hh * MLA_V_DIM:(hh + 1) * MLA_V_DIM, :], ones], axis=0)
            out.append((m_new, acc * alpha + _dot(vte, p)))
        return tuple(out)

    def pair(t, carry):
        scores(2 * t + 1, 1)
        carry = accumulate(2 * t, 0, carry, False)
        scores(2 * t + 2, 0)
        return accumulate(2 * t + 1, 1, carry, False)

    init = tuple((jnp.full((1, TQ), NEG_BIG, F32), jnp.zeros((MLA_V_DIM + ONES_ROWS, TQ), F32))
                 for _ in heads)
    scores(0, 0)
    carry = lax.fori_loop(0, qi, pair, init)
    d0 = 2 * qi
    scores(d0 + 1, 1)
    carry = accumulate(d0, 0, carry, True)
    carry = accumulate(d0 + 1, 1, carry, True)
    outs = []
    for hh in heads:
        acc = carry[hh][1]
        o_t = acc[:MLA_V_DIM] / acc[MLA_V_DIM:MLA_V_DIM + 1]
        outs.append(o_t.T)
    o_ref[0] = jnp.concatenate(outs, axis=1).astype(BF16)


def _mla_attention(q, k, vt):
    B, S, _ = q.shape
    nkb = S // TK
    hp = HEADS_PER_STEP
    return pl.pallas_call(
        _attn_body,
        grid=(B, MLA_HEADS // hp, S // TQ),
        in_specs=[pl.BlockSpec((1, TQ, hp * HEAD_SLAB), lambda b, g, i: (b, i, g)),
                  pl.BlockSpec((1, S, hp * HEAD_SLAB), lambda b, g, i: (b, 0, g)),
                  pl.BlockSpec((1, nkb, hp * MLA_V_DIM, TK), lambda b, g, i: (b, 0, g, 0))],
        out_specs=pl.BlockSpec((1, TQ, hp * MLA_V_DIM), lambda b, g, i: (b, i, g)),
        out_shape=jax.ShapeDtypeStruct((B, S, MLA_WIDTH), BF16),
        scratch_shapes=[pltpu.VMEM((2, hp, TK, TQ), F32), pltpu.VMEM((2, hp, 1, TQ), F32)],
        compiler_params=pltpu.CompilerParams(dimension_semantics=("arbitrary", "arbitrary", "arbitrary"),
                                             vmem_limit_bytes=VMEM_LIMIT),
        name="mla_attn",
    )(q, k, vt)


def _post_body(ret_ref, mla_ref, x_ref, p_ref, wo_ref, pmn_ref, pfn_ref, wg_ref, wu_ref, wd_ref,
               pofn_ref, wpp_ref, plen_ref, wpg_ref, bpg_ref, o_ref, act_ref):
    mix = _dot(ret_ref[0], wo_ref[0:RET_WIDTH, :]) + _dot(mla_ref[0], wo_ref[RET_WIDTH:, :])
    h1 = x_ref[0] + _rms(mix, pmn_ref[...])
    hn = _rms(h1, pfn_ref[...]).astype(BF16)
    for c in range(D_FF // FF_CHUNK):
        cols = slice(c * FF_CHUNK, (c + 1) * FF_CHUNK)
        g = _dot(hn, wg_ref[:, cols])
        u = _dot(hn, wu_ref[:, cols])
        act_ref[:, cols] = (g * jax.nn.sigmoid(g) * u).astype(BF16)
    ff = _dot(act_ref[...], wd_ref[...])
    h2 = h1 + _rms(ff, pofn_ref[...])
    e = _rms(_dot(p_ref[0].astype(BF16), wpp_ref[...]), plen_ref[...])
    gate = jax.nn.sigmoid(_dot(h2.astype(BF16), wpg_ref[...]) + bpg_ref[...])
    o_ref[0] = h2 + e * gate


def _post(ret_out, mla_out, x, p, w_o, post_mix_norm, pre_ffn_norm, w_gate, w_up, w_down,
          post_ffn_norm, w_ple_proj, ple_norm, w_ple_gate, b_ple_gate):
    B, S, D = x.shape
    tm = TM_POST
    const = lambda shape: pl.BlockSpec(shape, lambda b, i: (0,) * len(shape),
                                       pipeline_mode=pl.Buffered(1))
    tile = lambda w: pl.BlockSpec((1, tm, w), lambda b, i: (b, i, 0))
    return pl.pallas_call(
        _post_body,
        grid=(B, S // tm),
        in_specs=[tile(RET_WIDTH), tile(MLA_WIDTH), tile(D), tile(PLE_DIM),
                  const((D, D)), const((1, D)), const((1, D)),
                  const((D, D_FF)), const((D, D_FF)), const((D_FF, D)), const((1, D)),
                  const((PLE_DIM, D)), const((1, D)), const((D, D)), const((1, D))],
        out_specs=tile(D),
        out_shape=jax.ShapeDtypeStruct((B, S, D), F32),
        scratch_shapes=[pltpu.VMEM((tm, D_FF), BF16)],
        compiler_params=pltpu.CompilerParams(dimension_semantics=("arbitrary", "arbitrary"),
                                             vmem_limit_bytes=VMEM_LIMIT),
        name="post",
    )(ret_out, mla_out, x, p, w_o, post_mix_norm, pre_ffn_norm, w_gate, w_up, w_down,
      post_ffn_norm, w_ple_proj, ple_norm, w_ple_gate, b_ple_gate)


def _prep_w_in(w_in):
    base = 4 * RET_WIDTH + MLA_Q_LORA + MLA_KV_LORA
    kr = w_in[:, base:base + MLA_ROPE_DIM]
    z = jnp.zeros((w_in.shape[0], MLA_NOPE_DIM), w_in.dtype)
    zt = jnp.zeros((w_in.shape[0], LANES - MLA_QK_DIM), w_in.dtype)
    return jnp.concatenate([w_in[:, :base], z, kr, zt], axis=1).astype(BF16)


def _pad_heads(w, per_head, keep):
    K = w.shape[0]
    w = w.reshape(K, MLA_HEADS, per_head)[:, :, :keep]
    w = jnp.pad(w, ((0, 0), (0, 0), (0, HEAD_SLAB - keep)))
    return w.reshape(K, MLA_HEADS * HEAD_SLAB).astype(BF16)


def kernel(x, p, positions, pre_mix_norm, w_in, ret_gn_w, mla_q_norm, w_uq, mla_kv_norm, w_ukv, w_o,
           post_mix_norm, pre_ffn_norm, w_gate, w_up, w_down, post_ffn_norm, w_ple_proj, ple_norm,
           w_ple_gate, b_ple_gate):
    depth = pre_mix_norm.shape[0]
    tables = _rope_tables(positions)
    h = x
    for i in range(depth):
        wuq_p = _pad_heads(w_uq[i], MLA_QK_DIM, MLA_QK_DIM)
        wuk_p = _pad_heads(w_ukv[i], MLA_NOPE_DIM + MLA_V_DIM, MLA_NOPE_DIM)
        wv = w_ukv[i].reshape(MLA_KV_LORA, MLA_HEADS, MLA_NOPE_DIM + MLA_V_DIM)[:, :, MLA_NOPE_DIM:]
        wvt = wv.reshape(MLA_KV_LORA, MLA_WIDTH).T.astype(BF16)
        rq, rk, rv, rg, q, k, vt = _in_proj(h, pre_mix_norm[i][None], _prep_w_in(w_in[i]),
                                            mla_q_norm[i][None], wuq_p, mla_kv_norm[i][None], wuk_p, wvt,
                                            tables)
        ret_out = _retention(rq, rk, rv, rg, ret_gn_w[i][None])
        mla_out = _mla_attention(q, k, vt)
        h = _post(ret_out, mla_out, h, p[i], w_o[i].astype(BF16), post_mix_norm[i][None],
                  pre_ffn_norm[i][None], w_gate[i].astype(BF16), w_up[i].astype(BF16),
                  w_down[i].astype(BF16), post_ffn_norm[i][None], w_ple_proj[i].astype(BF16),
                  ple_norm[i][None], w_ple_gate[i].astype(BF16), b_ple_gate[i][None])
    return h
```

```python
import math
from functools import partial

import jax
import jax.numpy as jnp
from jax import lax
from jax.experimental import pallas as pl
from jax.experimental.pallas import tpu as pltpu

D_MODEL = 1024
PLE_DIM = 256
RET_HEADS = 4
RET_HEAD_DIM = 128
RET_WIDTH = RET_HEADS * RET_HEAD_DIM
MLA_HEADS = 8
MLA_NOPE_DIM = 64
MLA_ROPE_DIM = 32
MLA_QK_DIM = MLA_NOPE_DIM + MLA_ROPE_DIM
MLA_V_DIM = 64
MLA_WIDTH = MLA_HEADS * MLA_V_DIM
MLA_Q_LORA = 384
MLA_KV_LORA = 256
D_FF = 2816
ROPE_BASE = 10000.0
EPS = 1e-6
NEG_BIG = -1e30

LANES = 128
HEAD_SLAB = 128
IN_COLS_PAD = 4 * RET_WIDTH + MLA_Q_LORA + MLA_KV_LORA + LANES
VMEM_LIMIT = 56 * 1024 * 1024

TM_PROJ = 512
RET_CHUNK = 128
TM_RET = 512
TQ = 512
TK = TQ // 2
ONES_ROWS = 16
TM_POST = TQ
FF_CHUNK = 256

BF16 = jnp.bfloat16
F32 = jnp.float32


def _dot(a, b):
    return jnp.dot(a, b, preferred_element_type=F32)


def _dot_nt(a, b):
    return lax.dot_general(a, b, (((1,), (1,)), ((), ())), preferred_element_type=F32)


def _dot_tn(a, b):
    return lax.dot_general(a, b, (((0,), (0,)), ((), ())), preferred_element_type=F32)


def _rms(v, w):
    return v * lax.rsqrt(jnp.mean(v * v, axis=-1, keepdims=True) + EPS) * w


def _rope_tables_body(pos_ref, invr_ref, invm_ref, sgnr_ref, sgnm_ref, onesm_ref,
                      cr_ref, sr_ref, cm_ref, sm_ref):
    pos = pos_ref[0].astype(F32)
    ang_r = pos * invr_ref[...]
    cr_ref[0] = jnp.cos(ang_r)
    sr_ref[0] = jnp.sin(ang_r) * sgnr_ref[...]
    ang_m = pos * invm_ref[...]
    cm_ref[0] = jnp.cos(ang_m) * onesm_ref[...]
    sm_ref[0] = jnp.sin(ang_m) * sgnm_ref[...]


def _rope_tables(positions):
    B, S = positions.shape
    tm = 1024
    half_r = RET_HEAD_DIM // 2
    inv_r = 1.0 / (ROPE_BASE ** (jnp.arange(half_r, dtype=F32) / half_r))
    inv_r = jnp.concatenate([inv_r, inv_r])[None, :]
    sgn_r = jnp.concatenate([-jnp.ones((half_r,), F32), jnp.ones((half_r,), F32)])[None, :]
    half_m = MLA_ROPE_DIM // 2
    inv_m = 1.0 / (ROPE_BASE ** (jnp.arange(half_m, dtype=F32) / half_m))
    z64 = jnp.zeros((MLA_NOPE_DIM,), F32)
    z32 = jnp.zeros((HEAD_SLAB - MLA_QK_DIM,), F32)
    inv_m = jnp.concatenate([z64, inv_m, inv_m, z32])[None, :]
    sgn_m = jnp.concatenate([z64, -jnp.ones((half_m,), F32), jnp.ones((half_m,), F32), z32])[None, :]
    ones_m = jnp.concatenate([jnp.ones((MLA_QK_DIM,), F32), z32])[None, :]
    pos3 = positions.reshape(B, S, 1)
    vec = pl.BlockSpec((1, LANES), lambda b, i: (0, 0))
    tab = pl.BlockSpec((1, tm, LANES), lambda b, i: (b, i, 0))
    out = jax.ShapeDtypeStruct((B, S, LANES), F32)
    return pl.pallas_call(
        _rope_tables_body,
        grid=(B, S // tm),
        in_specs=[pl.BlockSpec((1, tm, 1), lambda b, i: (b, i, 0)), vec, vec, vec, vec, vec],
        out_specs=[tab, tab, tab, tab],
        out_shape=[out, out, out, out],
        compiler_params=pltpu.CompilerParams(dimension_semantics=("arbitrary", "arbitrary"),
                                             vmem_limit_bytes=VMEM_LIMIT),
        name="rope_tables",
    )(pos3, inv_r, inv_m, sgn_r, sgn_m, ones_m)


def _in_proj_body(x_ref, g_ref, win_ref, qn_ref, wuq_ref, kvn_ref, wuk_ref, wvt_ref,
                  cr_ref, sr_ref, cm_ref, sm_ref,
                  rq_ref, rk_ref, rv_ref, rg_ref, q_ref, k_ref, vt_ref, *, q_scale):
    xn = _rms(x_ref[0], g_ref[...]).astype(BF16)
    cr = cr_ref[0]
    sr = sr_ref[0]

    def ret_rope(v):
        return v * cr + pltpu.roll(v, RET_HEAD_DIM // 2, 1) * sr

    pq = _dot(xn, win_ref[:, 0:RET_WIDTH])
    pk = _dot(xn, win_ref[:, RET_WIDTH:2 * RET_WIDTH])
    for h in range(RET_HEADS):
        sl = slice(h * RET_HEAD_DIM, (h + 1) * RET_HEAD_DIM)
        rq_ref[0, :, sl] = ret_rope(pq[:, sl]).astype(BF16)
        rk_ref[0, :, sl] = (ret_rope(pk[:, sl]) * (RET_HEAD_DIM ** -0.5)).astype(BF16)
    rv_ref[0] = _dot(xn, win_ref[:, 2 * RET_WIDTH:3 * RET_WIDTH]).astype(BF16)
    rg_ref[0] = _dot(xn, win_ref[:, 3 * RET_WIDTH:4 * RET_WIDTH]).astype(BF16)

    cm = cm_ref[0]
    sm = sm_ref[0]
    lane = lax.broadcasted_iota(jnp.int32, cm.shape, 1)
    upper = lane >= (MLA_NOPE_DIM + MLA_ROPE_DIM // 2)

    def mla_rope(v):
        swapped = jnp.where(upper, pltpu.roll(v, MLA_ROPE_DIM // 2, 1),
                            pltpu.roll(v, LANES - MLA_ROPE_DIM // 2, 1))
        return v * cm + swapped * sm

    o = 4 * RET_WIDTH
    cq = _dot(xn, win_ref[:, o:o + MLA_Q_LORA])
    cqn = _rms(cq, qn_ref[...]).astype(BF16)
    qh = _dot(cqn, wuq_ref[...])
    for h in range(MLA_HEADS):
        sl = slice(h * HEAD_SLAB, (h + 1) * HEAD_SLAB)
        q_ref[0, :, sl] = (mla_rope(qh[:, sl]) * q_scale).astype(BF16)

    o += MLA_Q_LORA
    ckv = _dot(xn, win_ref[:, o:o + MLA_KV_LORA])
    ckvn = _rms(ckv, kvn_ref[...]).astype(BF16)
    o += MLA_KV_LORA
    kr = mla_rope(_dot(xn, win_ref[:, o:o + LANES]))
    kn = _dot(ckvn, wuk_ref[...])
    for h in range(MLA_HEADS):
        sl = slice(h * HEAD_SLAB, (h + 1) * HEAD_SLAB)
        k_ref[0, :, sl] = (kn[:, sl] + kr).astype(BF16)
    vt = _dot_nt(wvt_ref[...], ckvn).astype(BF16)
    for blk in range(TM_PROJ // TK):
        vt_ref[0, blk] = vt[:, blk * TK:(blk + 1) * TK]


def _in_proj(x, pre_mix_norm, w_in_p, q_norm, wuq_p, kv_norm, wuk_p, wvt, tables):
    B, S, D = x.shape
    tm = TM_PROJ
    cr, sr, cm, sm = tables
    q_scale = (1.0 / math.sqrt(MLA_QK_DIM)) * math.log2(math.e)
    const = lambda shape: pl.BlockSpec(shape, lambda b, i: (0,) * len(shape))
    tile = lambda w: pl.BlockSpec((1, tm, w), lambda b, i: (b, i, 0))
    bf = lambda w: jax.ShapeDtypeStruct((B, S, w), BF16)
    return pl.pallas_call(
        partial(_in_proj_body, q_scale=q_scale),
        grid=(B, S // tm),
        in_specs=[tile(D), const((1, D)), const((D, IN_COLS_PAD)),
                  const((1, MLA_Q_LORA)), const((MLA_Q_LORA, MLA_HEADS * HEAD_SLAB)),
                  const((1, MLA_KV_LORA)), const((MLA_KV_LORA, MLA_HEADS * HEAD_SLAB)),
                  const((MLA_WIDTH, MLA_KV_LORA)),
                  tile(LANES), tile(LANES), tile(LANES), tile(LANES)],
        out_specs=[tile(RET_WIDTH), tile(RET_WIDTH), tile(RET_WIDTH), tile(RET_WIDTH),
                   tile(MLA_HEADS * HEAD_SLAB), tile(MLA_HEADS * HEAD_SLAB),
                   pl.BlockSpec((1, tm // TK, MLA_WIDTH, TK), lambda b, i: (b, i, 0, 0))],
        out_shape=[bf(RET_WIDTH), bf(RET_WIDTH), bf(RET_WIDTH), bf(RET_WIDTH),
                   bf(MLA_HEADS * HEAD_SLAB), bf(MLA_HEADS * HEAD_SLAB),
                   jax.ShapeDtypeStruct((B, S // TK, MLA_WIDTH, TK), BF16)],
        compiler_params=pltpu.CompilerParams(dimension_semantics=("arbitrary", "arbitrary"),
                                             vmem_limit_bytes=VMEM_LIMIT),
        name="in_proj",
    )(x, pre_mix_norm, w_in_p, q_norm, wuq_p, kv_norm, wuk_p, wvt, cr, sr, cm, sm)


def _retention_body(rq_ref, rk_ref, rv_ref, rg_ref, dmask_ref, zeta_ref, xi_ref, gnw_ref,
                    o_ref, state_ref, *, g_chunk):
    @pl.when(pl.program_id(1) == 0)
    def _():
        state_ref[...] = jnp.zeros_like(state_ref)

    C = RET_CHUNK
    for c in range(TM_RET // C):
        rows = slice(c * C, (c + 1) * C)
        for h in range(RET_HEADS):
            cols = slice(h * RET_HEAD_DIM, (h + 1) * RET_HEAD_DIM)
            q = rq_ref[0, rows, cols]
            k = rk_ref[0, rows, cols]
            v = rv_ref[0, rows, cols]
            scores = _dot_nt(q, k) * dmask_ref[h]
            inner = _dot(scores.astype(BF16), v)
            r_prev = state_ref[h]
            cross = _dot(q, r_prev.astype(BF16)) * xi_ref[h]
            kz = (k.astype(F32) * zeta_ref[h]).astype(BF16)
            state_ref[h] = g_chunk[h] * r_prev + _dot_tn(kz, v)
            y = inner + cross
            mu = jnp.mean(y, axis=-1, keepdims=True)
            yc = y - mu
            var = jnp.mean(yc * yc, axis=-1, keepdims=True)
            yn = yc * lax.rsqrt(var + EPS) * gnw_ref[:, cols]
            gate = rg_ref[0, rows, cols].astype(F32)
            o_ref[0, rows, cols] = (gate * jax.nn.sigmoid(gate) * yn).astype(BF16)


def _retention(rq, rk, rv, rg, ret_gn_w):
    B, S, W = rq.shape
    C = RET_CHUNK
    H = RET_HEADS
    log_g = jnp.log(1.0 - 2.0 ** (-5.0 - jnp.arange(H, dtype=F32)))
    j = jnp.arange(C, dtype=F32)
    diff = j[:, None] - j[None, :]
    dmask = jnp.where(diff[None] >= 0, jnp.exp(jnp.maximum(diff, 0.0)[None] * log_g[:, None, None]), 0.0)
    zeta = jnp.exp((C - 1 - j)[None, :] * log_g[:, None])
    xi = jnp.exp((j + 1)[None, :] * log_g[:, None])
    zeta_b = jnp.broadcast_to(zeta[:, :, None], (H, C, RET_HEAD_DIM))
    xi_b = jnp.broadcast_to(xi[:, :, None], (H, C, RET_HEAD_DIM))
    g_chunk = tuple(float((1.0 - 2.0 ** (-5.0 - h)) ** C) for h in range(H))
    tile = pl.BlockSpec((1, TM_RET, W), lambda b, i: (b, i, 0))
    const3 = pl.BlockSpec((H, C, C), lambda b, i: (0, 0, 0))
    return pl.pallas_call(
        partial(_retention_body, g_chunk=g_chunk),
        grid=(B, S // TM_RET),
        in_specs=[tile, tile, tile, tile, const3, const3, const3,
                  pl.BlockSpec((1, W), lambda b, i: (0, 0))],
        out_specs=tile,
        out_shape=jax.ShapeDtypeStruct((B, S, W), BF16),
        scratch_shapes=[pltpu.VMEM((H, RET_HEAD_DIM, RET_HEAD_DIM), F32)],
        compiler_params=pltpu.CompilerParams(dimension_semantics=("arbitrary", "arbitrary"),
                                             vmem_limit_bytes=VMEM_LIMIT),
        name="retention",
    )(rq, rk, rv, rg, dmask, zeta_b, xi_b, ret_gn_w)


HEADS_PER_STEP = 2
TQH = TQ // 2


def _attn_body(q_ref, k_ref, vt_ref, o_ref, s_ref, mb_ref):
    nq = q_ref.shape[1] // TQ
    ones = jnp.ones((ONES_ROWS, TK), BF16)
    heads = range(HEADS_PER_STEP)
    row = lax.broadcasted_iota(jnp.int32, (TK, TQH), 0)
    col = lax.broadcasted_iota(jnp.int32, (TK, TQH), 1)
    tri = row <= col

    def scores(qstart, j, slot, width):
        kstart = pl.multiple_of(j * TK, TK)
        q0 = pl.multiple_of(qstart + (TQ - width), TQH)
        for hh in heads:
            cols = slice(hh * HEAD_SLAB, (hh + 1) * HEAD_SLAB)
            st = _dot_nt(k_ref[0, pl.ds(kstart, TK), cols], q_ref[0, pl.ds(q0, width), cols])
            s_ref[slot, hh, :, 0:width] = st
            mb_ref[slot, hh, :, 0:width] = jnp.max(st, axis=0, keepdims=True)

    def vte(j, hh):
        return jnp.concatenate([vt_ref[0, j, hh * MLA_V_DIM:(hh + 1) * MLA_V_DIM, :], ones], axis=0)

    def update(m, acc, st, blk_max, j, hh):
        m_new = jnp.maximum(m, blk_max)
        p = jnp.exp2(st - m_new).astype(BF16)
        return m_new, acc * jnp.exp2(m - m_new) + _dot(vte(j, hh), p)

    def accumulate(j, slot, carry):
        return tuple(update(m, acc, s_ref[slot, hh], mb_ref[slot, hh], j, hh)
                     for hh, (m, acc) in enumerate(carry))

    def accumulate_d0(j, slot, carry):
        out = []
        for hh, (m, acc) in enumerate(carry):
            st = s_ref[slot, hh]
            st = jnp.concatenate([jnp.where(tri, st[:, :TQH], NEG_BIG), st[:, TQH:]], axis=1)
            out.append(update(m, acc, st, jnp.max(st, axis=0, keepdims=True), j, hh))
        return tuple(out)

    def accumulate_d1(j, slot, carry):
        out = []
        for hh, (m, acc) in enumerate(carry):
            st = jnp.where(tri, s_ref[slot, hh, :, 0:TQH], NEG_BIG)
            m_hi, acc_hi = update(m[:, TQH:], acc[:, TQH:], st, jnp.max(st, axis=0, keepdims=True), j, hh)
            out.append((jnp.concatenate([m[:, :TQH], m_hi], axis=1),
                        jnp.concatenate([acc[:, :TQH], acc_hi], axis=1)))
        return tuple(out)

    def tile(qi, _):
        qstart = qi * TQ

        def pair(t, carry):
            scores(qstart, 2 * t + 1, 1, TQ)
            carry = accumulate(2 * t, 0, carry)
            scores(qstart, 2 * t + 2, 0, TQ)
            return accumulate(2 * t + 1, 1, carry)

        init = tuple((jnp.full((1, TQ), NEG_BIG, F32), jnp.zeros((MLA_V_DIM + ONES_ROWS, TQ), F32))
                     for _ in heads)
        carry = lax.fori_loop(0, qi, pair, init)
        d0 = 2 * qi
        scores(qstart, d0 + 1, 1, TQH)
        carry = accumulate_d0(d0, 0, carry)
        scores(jnp.minimum(qi + 1, nq - 1) * TQ, 0, 0, TQ)
        carry = accumulate_d1(d0 + 1, 1, carry)
        for hh, (_, acc) in enumerate(carry):
            o_ref[0, qi, hh * MLA_V_DIM:(hh + 1) * MLA_V_DIM, :] = (
                acc[:MLA_V_DIM] / acc[MLA_V_DIM:MLA_V_DIM + 1]).astype(BF16)
        return 0

    scores(0, 0, 0, TQ)
    lax.fori_loop(0, nq, tile, 0)


def _mla_attention(q, k, vt):
    B, S, _ = q.shape
    nkb = S // TK
    nq = S // TQ
    hp = HEADS_PER_STEP
    return pl.pallas_call(
        _attn_body,
        grid=(B, MLA_HEADS // hp),
        in_specs=[pl.BlockSpec((1, S, hp * HEAD_SLAB), lambda b, g: (b, 0, g)),
                  pl.BlockSpec((1, S, hp * HEAD_SLAB), lambda b, g: (b, 0, g)),
                  pl.BlockSpec((1, nkb, hp * MLA_V_DIM, TK), lambda b, g: (b, 0, g, 0))],
        out_specs=pl.BlockSpec((1, nq, hp * MLA_V_DIM, TQ), lambda b, g: (b, 0, g, 0)),
        out_shape=jax.ShapeDtypeStruct((B, nq, MLA_WIDTH, TQ), BF16),
        scratch_shapes=[pltpu.VMEM((2, hp, TK, TQ), F32), pltpu.VMEM((2, hp, 1, TQ), F32)],
        compiler_params=pltpu.CompilerParams(dimension_semantics=("arbitrary", "arbitrary"),
                                             vmem_limit_bytes=VMEM_LIMIT),
        name="mla_attn",
    )(q, k, vt)


def _post_body(ret_ref, mla_ref, x_ref, p_ref, wo_ref, pmn_ref, pfn_ref, wg_ref, wu_ref, wd_ref,
               pofn_ref, wpp_ref, plen_ref, wpg_ref, bpg_ref, o_ref, act_ref):
    mix = _dot(ret_ref[0], wo_ref[0:RET_WIDTH, :]) + _dot_tn(mla_ref[0, 0], wo_ref[RET_WIDTH:, :])
    h1 = x_ref[0] + _rms(mix, pmn_ref[...])
    hn = _rms(h1, pfn_ref[...]).astype(BF16)
    for c in range(D_FF // FF_CHUNK):
        cols = slice(c * FF_CHUNK, (c + 1) * FF_CHUNK)
        g = _dot(hn, wg_ref[:, cols])
        u = _dot(hn, wu_ref[:, cols])
        act_ref[:, cols] = (g * jax.nn.sigmoid(g) * u).astype(BF16)
    ff = _dot(act_ref[...], wd_ref[...])
    h2 = h1 + _rms(ff, pofn_ref[...])
    e = _rms(_dot(p_ref[0].astype(BF16), wpp_ref[...]), plen_ref[...])
    gate = jax.nn.sigmoid(_dot(h2.astype(BF16), wpg_ref[...]) + bpg_ref[...])
    o_ref[0] = h2 + e * gate


def _post(ret_out, mla_out, x, p, w_o, post_mix_norm, pre_ffn_norm, w_gate, w_up, w_down,
          post_ffn_norm, w_ple_proj, ple_norm, w_ple_gate, b_ple_gate):
    B, S, D = x.shape
    tm = TM_POST
    const = lambda shape: pl.BlockSpec(shape, lambda b, i: (0,) * len(shape),
                                       pipeline_mode=pl.Buffered(1))
    tile = lambda w: pl.BlockSpec((1, tm, w), lambda b, i: (b, i, 0))
    return pl.pallas_call(
        _post_body,
        grid=(B, S // tm),
        in_specs=[tile(RET_WIDTH), pl.BlockSpec((1, 1, MLA_WIDTH, tm), lambda b, i: (b, i, 0, 0)),
                  tile(D), tile(PLE_DIM),
                  const((D, D)), const((1, D)), const((1, D)),
                  const((D, D_FF)), const((D, D_FF)), const((D_FF, D)), const((1, D)),
                  const((PLE_DIM, D)), const((1, D)), const((D, D)), const((1, D))],
        out_specs=tile(D),
        out_shape=jax.ShapeDtypeStruct((B, S, D), F32),
        scratch_shapes=[pltpu.VMEM((tm, D_FF), BF16)],
        compiler_params=pltpu.CompilerParams(dimension_semantics=("arbitrary", "arbitrary"),
                                             vmem_limit_bytes=VMEM_LIMIT),
        name="post",
    )(ret_out, mla_out, x, p, w_o, post_mix_norm, pre_ffn_norm, w_gate, w_up, w_down,
      post_ffn_norm, w_ple_proj, ple_norm, w_ple_gate, b_ple_gate)


def _prep_w_in(w_in):
    base = 4 * RET_WIDTH + MLA_Q_LORA + MLA_KV_LORA
    kr = w_in[:, base:base + MLA_ROPE_DIM]
    z = jnp.zeros((w_in.shape[0], MLA_NOPE_DIM), w_in.dtype)
    zt = jnp.zeros((w_in.shape[0], LANES - MLA_QK_DIM), w_in.dtype)
    return jnp.concatenate([w_in[:, :base], z, kr, zt], axis=1).astype(BF16)


def _pad_heads(w, per_head, keep):
    K = w.shape[0]
    w = w.reshape(K, MLA_HEADS, per_head)[:, :, :keep]
    w = jnp.pad(w, ((0, 0), (0, 0), (0, HEAD_SLAB - keep)))
    return w.reshape(K, MLA_HEADS * HEAD_SLAB).astype(BF16)


def kernel(x, p, positions, pre_mix_norm, w_in, ret_gn_w, mla_q_norm, w_uq, mla_kv_norm, w_ukv, w_o,
           post_mix_norm, pre_ffn_norm, w_gate, w_up, w_down, post_ffn_norm, w_ple_proj, ple_norm,
           w_ple_gate, b_ple_gate):
    depth = pre_mix_norm.shape[0]
    tables = _rope_tables(positions)
    h = x
    for i in range(depth):
        wuq_p = _pad_heads(w_uq[i], MLA_QK_DIM, MLA_QK_DIM)
        wuk_p = _pad_heads(w_ukv[i], MLA_NOPE_DIM + MLA_V_DIM, MLA_NOPE_DIM)
        wv = w_ukv[i].reshape(MLA_KV_LORA, MLA_HEADS, MLA_NOPE_DIM + MLA_V_DIM)[:, :, MLA_NOPE_DIM:]
        wvt = wv.reshape(MLA_KV_LORA, MLA_WIDTH).T.astype(BF16)
        rq, rk, rv, rg, q, k, vt = _in_proj(h, pre_mix_norm[i][None], _prep_w_in(w_in[i]),
                                            mla_q_norm[i][None], wuq_p, mla_kv_norm[i][None], wuk_p, wvt,
                                            tables)
        ret_out = _retention(rq, rk, rv, rg, ret_gn_w[i][None])
        mla_out = _mla_attention(q, k, vt)
        h = _post(ret_out, mla_out, h, p[i], w_o[i].astype(BF16), post_mix_norm[i][None],
                  pre_ffn_norm[i][None], w_gate[i].astype(BF16), w_up[i].astype(BF16),
                  w_down[i].astype(BF16), post_ffn_norm[i][None], w_ple_proj[i].astype(BF16),
                  ple_norm[i][None], w_ple_gate[i].astype(BF16), b_ple_gate[i][None])
    return h
```

```python
import math
from functools import partial

import jax
import jax.numpy as jnp
from jax import lax
from jax.experimental import pallas as pl
from jax.experimental.pallas import tpu as pltpu

D_MODEL = 1024
PLE_DIM = 256
RET_HEADS = 4
RET_HEAD_DIM = 128
RET_WIDTH = RET_HEADS * RET_HEAD_DIM
MLA_HEADS = 8
MLA_NOPE_DIM = 64
MLA_ROPE_DIM = 32
MLA_QK_DIM = MLA_NOPE_DIM + MLA_ROPE_DIM
MLA_V_DIM = 64
MLA_WIDTH = MLA_HEADS * MLA_V_DIM
MLA_Q_LORA = 384
MLA_KV_LORA = 256
D_FF = 2816
ROPE_BASE = 10000.0
EPS = 1e-6
NEG_BIG = -1e30

LANES = 128
HEAD_SLAB = 128
IN_COLS_PAD = 4 * RET_WIDTH + MLA_Q_LORA + MLA_KV_LORA + LANES
VMEM_LIMIT = 56 * 1024 * 1024

TM_PROJ = 512
RET_CHUNK = 128
TM_RET = 512
TQ = 1024
TK = 256
ONES_ROWS = 16
TM_POST = 512
FF_CHUNK = 256

BF16 = jnp.bfloat16
F32 = jnp.float32


def _dot(a, b):
    return jnp.dot(a, b, preferred_element_type=F32)


def _dot_nt(a, b):
    return lax.dot_general(a, b, (((1,), (1,)), ((), ())), preferred_element_type=F32)


def _dot_tn(a, b):
    return lax.dot_general(a, b, (((0,), (0,)), ((), ())), preferred_element_type=F32)


def _rms(v, w):
    return v * lax.rsqrt(jnp.mean(v * v, axis=-1, keepdims=True) + EPS) * w


def _rope_tables_body(pos_ref, invr_ref, invm_ref, sgnr_ref, sgnm_ref, onesm_ref,
                      cr_ref, sr_ref, cm_ref, sm_ref):
    pos = pos_ref[0].astype(F32)
    ang_r = pos * invr_ref[...]
    cr_ref[0] = jnp.cos(ang_r)
    sr_ref[0] = jnp.sin(ang_r) * sgnr_ref[...]
    ang_m = pos * invm_ref[...]
    cm_ref[0] = jnp.cos(ang_m) * onesm_ref[...]
    sm_ref[0] = jnp.sin(ang_m) * sgnm_ref[...]


def _rope_tables(positions):
    B, S = positions.shape
    tm = 1024
    half_r = RET_HEAD_DIM // 2
    inv_r = 1.0 / (ROPE_BASE ** (jnp.arange(half_r, dtype=F32) / half_r))
    inv_r = jnp.concatenate([inv_r, inv_r])[None, :]
    sgn_r = jnp.concatenate([-jnp.ones((half_r,), F32), jnp.ones((half_r,), F32)])[None, :]
    half_m = MLA_ROPE_DIM // 2
    inv_m = 1.0 / (ROPE_BASE ** (jnp.arange(half_m, dtype=F32) / half_m))
    z64 = jnp.zeros((MLA_NOPE_DIM,), F32)
    z32 = jnp.zeros((HEAD_SLAB - MLA_QK_DIM,), F32)
    inv_m = jnp.concatenate([z64, inv_m, inv_m, z32])[None, :]
    sgn_m = jnp.concatenate([z64, -jnp.ones((half_m,), F32), jnp.ones((half_m,), F32), z32])[None, :]
    ones_m = jnp.concatenate([jnp.ones((MLA_QK_DIM,), F32), z32])[None, :]
    pos3 = positions.reshape(B, S, 1)
    vec = pl.BlockSpec((1, LANES), lambda b, i: (0, 0))
    tab = pl.BlockSpec((1, tm, LANES), lambda b, i: (b, i, 0))
    out = jax.ShapeDtypeStruct((B, S, LANES), F32)
    return pl.pallas_call(
        _rope_tables_body,
        grid=(B, S // tm),
        in_specs=[pl.BlockSpec((1, tm, 1), lambda b, i: (b, i, 0)), vec, vec, vec, vec, vec],
        out_specs=[tab, tab, tab, tab],
        out_shape=[out, out, out, out],
        compiler_params=pltpu.CompilerParams(dimension_semantics=("arbitrary", "arbitrary"),
                                             vmem_limit_bytes=VMEM_LIMIT),
        name="rope_tables",
    )(pos3, inv_r, inv_m, sgn_r, sgn_m, ones_m)


def _in_proj_body(x_ref, g_ref, win_ref, qn_ref, wuq_ref, kvn_ref, wuk_ref, wvt_ref,
                  cr_ref, sr_ref, cm_ref, sm_ref,
                  rq_ref, rk_ref, rv_ref, rg_ref, q_ref, k_ref, vt_ref, *, q_scale):
    xn = _rms(x_ref[0], g_ref[...]).astype(BF16)
    cr = cr_ref[0]
    sr = sr_ref[0]

    def ret_rope(v):
        return v * cr + pltpu.roll(v, RET_HEAD_DIM // 2, 1) * sr

    pq = _dot(xn, win_ref[:, 0:RET_WIDTH])
    pk = _dot(xn, win_ref[:, RET_WIDTH:2 * RET_WIDTH])
    for h in range(RET_HEADS):
        sl = slice(h * RET_HEAD_DIM, (h + 1) * RET_HEAD_DIM)
        rq_ref[0, :, sl] = ret_rope(pq[:, sl]).astype(BF16)
        rk_ref[0, :, sl] = (ret_rope(pk[:, sl]) * (RET_HEAD_DIM ** -0.5)).astype(BF16)
    rv_ref[0] = _dot(xn, win_ref[:, 2 * RET_WIDTH:3 * RET_WIDTH]).astype(BF16)
    rg_ref[0] = _dot(xn, win_ref[:, 3 * RET_WIDTH:4 * RET_WIDTH]).astype(BF16)

    cm = cm_ref[0]
    sm = sm_ref[0]
    lane = lax.broadcasted_iota(jnp.int32, cm.shape, 1)
    upper = lane >= (MLA_NOPE_DIM + MLA_ROPE_DIM // 2)

    def mla_rope(v):
        swapped = jnp.where(upper, pltpu.roll(v, MLA_ROPE_DIM // 2, 1),
                            pltpu.roll(v, LANES - MLA_ROPE_DIM // 2, 1))
        return v * cm + swapped * sm

    o = 4 * RET_WIDTH
    cq = _dot(xn, win_ref[:, o:o + MLA_Q_LORA])
    cqn = _rms(cq, qn_ref[...]).astype(BF16)
    qh = _dot(cqn, wuq_ref[...])
    for h in range(MLA_HEADS):
        sl = slice(h * HEAD_SLAB, (h + 1) * HEAD_SLAB)
        q_ref[0, :, sl] = (mla_rope(qh[:, sl]) * q_scale).astype(BF16)

    o += MLA_Q_LORA
    ckv = _dot(xn, win_ref[:, o:o + MLA_KV_LORA])
    ckvn = _rms(ckv, kvn_ref[...]).astype(BF16)
    o += MLA_KV_LORA
    kr = mla_rope(_dot(xn, win_ref[:, o:o + LANES]))
    kn = _dot(ckvn, wuk_ref[...])
    for h in range(MLA_HEADS):
        sl = slice(h * HEAD_SLAB, (h + 1) * HEAD_SLAB)
        k_ref[0, :, sl] = (kn[:, sl] + kr).astype(BF16)
    vt = _dot_nt(wvt_ref[...], ckvn).astype(BF16)
    for blk in range(TM_PROJ // TK):
        vt_ref[0, blk] = vt[:, blk * TK:(blk + 1) * TK]


def _in_proj(x, pre_mix_norm, w_in_p, q_norm, wuq_p, kv_norm, wuk_p, wvt, tables):
    B, S, D = x.shape
    tm = TM_PROJ
    cr, sr, cm, sm = tables
    q_scale = (1.0 / math.sqrt(MLA_QK_DIM)) * math.log2(math.e)
    const = lambda shape: pl.BlockSpec(shape, lambda b, i: (0,) * len(shape))
    tile = lambda w: pl.BlockSpec((1, tm, w), lambda b, i: (b, i, 0))
    bf = lambda w: jax.ShapeDtypeStruct((B, S, w), BF16)
    return pl.pallas_call(
        partial(_in_proj_body, q_scale=q_scale),
        grid=(B, S // tm),
        in_specs=[tile(D), const((1, D)), const((D, IN_COLS_PAD)),
                  const((1, MLA_Q_LORA)), const((MLA_Q_LORA, MLA_HEADS * HEAD_SLAB)),
                  const((1, MLA_KV_LORA)), const((MLA_KV_LORA, MLA_HEADS * HEAD_SLAB)),
                  const((MLA_WIDTH, MLA_KV_LORA)),
                  tile(LANES), tile(LANES), tile(LANES), tile(LANES)],
        out_specs=[tile(RET_WIDTH), tile(RET_WIDTH), tile(RET_WIDTH), tile(RET_WIDTH),
                   tile(MLA_HEADS * HEAD_SLAB), tile(MLA_HEADS * HEAD_SLAB),
                   pl.BlockSpec((1, tm // TK, MLA_WIDTH, TK), lambda b, i: (b, i, 0, 0))],
        out_shape=[bf(RET_WIDTH), bf(RET_WIDTH), bf(RET_WIDTH), bf(RET_WIDTH),
                   bf(MLA_HEADS * HEAD_SLAB), bf(MLA_HEADS * HEAD_SLAB),
                   jax.ShapeDtypeStruct((B, S // TK, MLA_WIDTH, TK), BF16)],
        compiler_params=pltpu.CompilerParams(dimension_semantics=("arbitrary", "arbitrary"),
                                             vmem_limit_bytes=VMEM_LIMIT),
        name="in_proj",
    )(x, pre_mix_norm, w_in_p, q_norm, wuq_p, kv_norm, wuk_p, wvt, cr, sr, cm, sm)


def _retention_body(rq_ref, rk_ref, rv_ref, rg_ref, dmask_ref, zeta_ref, xi_ref, gnw_ref,
                    o_ref, state_ref, *, g_chunk):
    @pl.when(pl.program_id(1) == 0)
    def _():
        state_ref[...] = jnp.zeros_like(state_ref)

    C = RET_CHUNK
    for c in range(TM_RET // C):
        rows = slice(c * C, (c + 1) * C)
        for h in range(RET_HEADS):
            cols = slice(h * RET_HEAD_DIM, (h + 1) * RET_HEAD_DIM)
            q = rq_ref[0, rows, cols]
            k = rk_ref[0, rows, cols]
            v = rv_ref[0, rows, cols]
            scores = _dot_nt(q, k) * dmask_ref[h]
            inner = _dot(scores.astype(BF16), v)
            r_prev = state_ref[h]
            cross = _dot(q, r_prev.astype(BF16)) * xi_ref[h]
            kz = (k.astype(F32) * zeta_ref[h]).astype(BF16)
            state_ref[h] = g_chunk[h] * r_prev + _dot_tn(kz, v)
            y = inner + cross
            mu = jnp.mean(y, axis=-1, keepdims=True)
            yc = y - mu
            var = jnp.mean(yc * yc, axis=-1, keepdims=True)
            yn = yc * lax.rsqrt(var + EPS) * gnw_ref[:, cols]
            gate = rg_ref[0, rows, cols].astype(F32)
            o_ref[0, rows, cols] = (gate * jax.nn.sigmoid(gate) * yn).astype(BF16)


def _retention(rq, rk, rv, rg, ret_gn_w):
    B, S, W = rq.shape
    C = RET_CHUNK
    H = RET_HEADS
    log_g = jnp.log(1.0 - 2.0 ** (-5.0 - jnp.arange(H, dtype=F32)))
    j = jnp.arange(C, dtype=F32)
    diff = j[:, None] - j[None, :]
    dmask = jnp.where(diff[None] >= 0, jnp.exp(jnp.maximum(diff, 0.0)[None] * log_g[:, None, None]), 0.0)
    zeta = jnp.exp((C - 1 - j)[None, :] * log_g[:, None])
    xi = jnp.exp((j + 1)[None, :] * log_g[:, None])
    zeta_b = jnp.broadcast_to(zeta[:, :, None], (H, C, RET_HEAD_DIM))
    xi_b = jnp.broadcast_to(xi[:, :, None], (H, C, RET_HEAD_DIM))
    g_chunk = tuple(float((1.0 - 2.0 ** (-5.0 - h)) ** C) for h in range(H))
    tile = pl.BlockSpec((1, TM_RET, W), lambda b, i: (b, i, 0))
    const3 = pl.BlockSpec((H, C, C), lambda b, i: (0, 0, 0))
    return pl.pallas_call(
        partial(_retention_body, g_chunk=g_chunk),
        grid=(B, S // TM_RET),
        in_specs=[tile, tile, tile, tile, const3, const3, const3,
                  pl.BlockSpec((1, W), lambda b, i: (0, 0))],
        out_specs=tile,
        out_shape=jax.ShapeDtypeStruct((B, S, W), BF16),
        scratch_shapes=[pltpu.VMEM((H, RET_HEAD_DIM, RET_HEAD_DIM), F32)],
        compiler_params=pltpu.CompilerParams(dimension_semantics=("arbitrary", "arbitrary"),
                                             vmem_limit_bytes=VMEM_LIMIT),
        name="retention",
    )(rq, rk, rv, rg, dmask, zeta_b, xi_b, ret_gn_w)


HEADS_PER_STEP = 2
DIAG_BLOCKS = TQ // TK


def _attn_body(q_ref, k_ref, vt_ref, o_ref, s_ref, mb_ref):
    nq = q_ref.shape[1] // TQ
    ones = jnp.ones((ONES_ROWS, TK), BF16)
    heads = range(HEADS_PER_STEP)
    row = lax.broadcasted_iota(jnp.int32, (TK, TK), 0)
    col = lax.broadcasted_iota(jnp.int32, (TK, TK), 1)
    tri = row <= col

    def scores(qstart, j, slot, width):
        kstart = pl.multiple_of(j * TK, TK)
        q0 = pl.multiple_of(qstart + (TQ - width), TK)
        for hh in heads:
            cols = slice(hh * HEAD_SLAB, (hh + 1) * HEAD_SLAB)
            st = _dot_nt(k_ref[0, pl.ds(kstart, TK), cols], q_ref[0, pl.ds(q0, width), cols])
            s_ref[slot, hh, :, 0:width] = st
            mb_ref[slot, hh, :, 0:width] = jnp.max(st, axis=0, keepdims=True)

    def vte(j, hh):
        return jnp.concatenate([vt_ref[0, j, hh * MLA_V_DIM:(hh + 1) * MLA_V_DIM, :], ones], axis=0)

    def update(m, acc, st, blk_max, j, hh):
        m_new = jnp.maximum(m, blk_max)
        p = jnp.exp2(st - m_new).astype(BF16)
        return m_new, acc * jnp.exp2(m - m_new) + _dot(vte(j, hh), p)

    def accumulate(j, slot, carry):
        return tuple(update(m, acc, s_ref[slot, hh], mb_ref[slot, hh], j, hh)
                     for hh, (m, acc) in enumerate(carry))

    def accumulate_diag(j, slot, carry, width):
        off = TQ - width
        out = []
        for hh, (m, acc) in enumerate(carry):
            st = s_ref[slot, hh, :, 0:width]
            masked = jnp.where(tri, st[:, :TK], NEG_BIG)
            st = masked if width == TK else jnp.concatenate([masked, st[:, TK:]], axis=1)
            m_w, acc_w = update(m[:, off:], acc[:, off:], st, jnp.max(st, axis=0, keepdims=True), j, hh)
            if off:
                m_w = jnp.concatenate([m[:, :off], m_w], axis=1)
                acc_w = jnp.concatenate([acc[:, :off], acc_w], axis=1)
            out.append((m_w, acc_w))
        return tuple(out)

    def tile(qi, _):
        qstart = qi * TQ

        def pair(t, carry):
            scores(qstart, 2 * t + 1, 1, TQ)
            carry = accumulate(2 * t, 0, carry)
            scores(qstart, 2 * t + 2, 0, TQ)
            return accumulate(2 * t + 1, 1, carry)

        init = tuple((jnp.full((1, TQ), NEG_BIG, F32), jnp.zeros((MLA_V_DIM + ONES_ROWS, TQ), F32))
                     for _ in heads)
        d0 = DIAG_BLOCKS * qi
        carry = lax.fori_loop(0, d0 // 2, pair, init)
        for d in range(DIAG_BLOCKS):
            if d + 1 < DIAG_BLOCKS:
                scores(qstart, d0 + d + 1, (d + 1) % 2, TQ - (d + 1) * TK)
            else:
                scores(jnp.minimum(qi + 1, nq - 1) * TQ, 0, (d + 1) % 2, TQ)
            carry = accumulate_diag(d0 + d, d % 2, carry, TQ - d * TK)
        for hh, (_, acc) in enumerate(carry):
            o_ref[0, qi, hh * MLA_V_DIM:(hh + 1) * MLA_V_DIM, :] = (
                acc[:MLA_V_DIM] / acc[MLA_V_DIM:MLA_V_DIM + 1]).astype(BF16)
        return 0

    scores(0, 0, 0, TQ)
    lax.fori_loop(0, nq, tile, 0)


def _mla_attention(q, k, vt):
    B, S, _ = q.shape
    nkb = S // TK
    nq = S // TQ
    hp = HEADS_PER_STEP
    return pl.pallas_call(
        _attn_body,
        grid=(B, MLA_HEADS // hp),
        in_specs=[pl.BlockSpec((1, S, hp * HEAD_SLAB), lambda b, g: (b, 0, g)),
                  pl.BlockSpec((1, S, hp * HEAD_SLAB), lambda b, g: (b, 0, g)),
                  pl.BlockSpec((1, nkb, hp * MLA_V_DIM, TK), lambda b, g: (b, 0, g, 0))],
        out_specs=pl.BlockSpec((1, nq, hp * MLA_V_DIM, TQ), lambda b, g: (b, 0, g, 0)),
        out_shape=jax.ShapeDtypeStruct((B, nq, MLA_WIDTH, TQ), BF16),
        scratch_shapes=[pltpu.VMEM((2, hp, TK, TQ), F32), pltpu.VMEM((2, hp, 1, TQ), F32)],
        compiler_params=pltpu.CompilerParams(dimension_semantics=("arbitrary", "arbitrary"),
                                             vmem_limit_bytes=VMEM_LIMIT),
        name="mla_attn",
    )(q, k, vt)


def _post_body(ret_ref, mla_ref, x_ref, p_ref, wo_ref, pmn_ref, pfn_ref, wg_ref, wu_ref, wd_ref,
               pofn_ref, wpp_ref, plen_ref, wpg_ref, bpg_ref, o_ref, act_ref):
    mix = _dot(ret_ref[0], wo_ref[0:RET_WIDTH, :]) + _dot_tn(mla_ref[0, 0], wo_ref[RET_WIDTH:, :])
    h1 = x_ref[0] + _rms(mix, pmn_ref[...])
    hn = _rms(h1, pfn_ref[...]).astype(BF16)
    for c in range(D_FF // FF_CHUNK):
        cols = slice(c * FF_CHUNK, (c + 1) * FF_CHUNK)
        g = _dot(hn, wg_ref[:, cols])
        u = _dot(hn, wu_ref[:, cols])
        act_ref[:, cols] = (g * jax.nn.sigmoid(g) * u).astype(BF16)
    ff = _dot(act_ref[...], wd_ref[...])
    h2 = h1 + _rms(ff, pofn_ref[...])
    e = _rms(_dot(p_ref[0].astype(BF16), wpp_ref[...]), plen_ref[...])
    gate = jax.nn.sigmoid(_dot(h2.astype(BF16), wpg_ref[...]) + bpg_ref[...])
    o_ref[0] = h2 + e * gate


def _post(ret_out, mla_out, x, p, w_o, post_mix_norm, pre_ffn_norm, w_gate, w_up, w_down,
          post_ffn_norm, w_ple_proj, ple_norm, w_ple_gate, b_ple_gate):
    B, S, D = x.shape
    tm = TM_POST
    const = lambda shape: pl.BlockSpec(shape, lambda b, i: (0,) * len(shape),
                                       pipeline_mode=pl.Buffered(1))
    tile = lambda w: pl.BlockSpec((1, tm, w), lambda b, i: (b, i, 0))
    return pl.pallas_call(
        _post_body,
        grid=(B, S // tm),
        in_specs=[tile(RET_WIDTH),
                  pl.BlockSpec((1, 1, MLA_WIDTH, tm), lambda b, i: (b, i // (TQ // tm), 0, i % (TQ // tm))),
                  tile(D), tile(PLE_DIM),
                  const((D, D)), const((1, D)), const((1, D)),
                  const((D, D_FF)), const((D, D_FF)), const((D_FF, D)), const((1, D)),
                  const((PLE_DIM, D)), const((1, D)), const((D, D)), const((1, D))],
        out_specs=tile(D),
        out_shape=jax.ShapeDtypeStruct((B, S, D), F32),
        scratch_shapes=[pltpu.VMEM((tm, D_FF), BF16)],
        compiler_params=pltpu.CompilerParams(dimension_semantics=("arbitrary", "arbitrary"),
                                             vmem_limit_bytes=VMEM_LIMIT),
        name="post",
    )(ret_out, mla_out, x, p, w_o, post_mix_norm, pre_ffn_norm, w_gate, w_up, w_down,
      post_ffn_norm, w_ple_proj, ple_norm, w_ple_gate, b_ple_gate)


def _prep_w_in(w_in):
    base = 4 * RET_WIDTH + MLA_Q_LORA + MLA_KV_LORA
    kr = w_in[:, base:base + MLA_ROPE_DIM]
    z = jnp.zeros((w_in.shape[0], MLA_NOPE_DIM), w_in.dtype)
    zt = jnp.zeros((w_in.shape[0], LANES - MLA_QK_DIM), w_in.dtype)
    return jnp.concatenate([w_in[:, :base], z, kr, zt], axis=1).astype(BF16)


def _pad_heads(w, per_head, keep):
    K = w.shape[0]
    w = w.reshape(K, MLA_HEADS, per_head)[:, :, :keep]
    w = jnp.pad(w, ((0, 0), (0, 0), (0, HEAD_SLAB - keep)))
    return w.reshape(K, MLA_HEADS * HEAD_SLAB).astype(BF16)


def kernel(x, p, positions, pre_mix_norm, w_in, ret_gn_w, mla_q_norm, w_uq, mla_kv_norm, w_ukv, w_o,
           post_mix_norm, pre_ffn_norm, w_gate, w_up, w_down, post_ffn_norm, w_ple_proj, ple_norm,
           w_ple_gate, b_ple_gate):
    depth = pre_mix_norm.shape[0]
    tables = _rope_tables(positions)
    h = x
    for i in range(depth):
        wuq_p = _pad_heads(w_uq[i], MLA_QK_DIM, MLA_QK_DIM)
        wuk_p = _pad_heads(w_ukv[i], MLA_NOPE_DIM + MLA_V_DIM, MLA_NOPE_DIM)
        wv = w_ukv[i].reshape(MLA_KV_LORA, MLA_HEADS, MLA_NOPE_DIM + MLA_V_DIM)[:, :, MLA_NOPE_DIM:]
        wvt = wv.reshape(MLA_KV_LORA, MLA_WIDTH).T.astype(BF16)
        rq, rk, rv, rg, q, k, vt = _in_proj(h, pre_mix_norm[i][None], _prep_w_in(w_in[i]),
                                            mla_q_norm[i][None], wuq_p, mla_kv_norm[i][None], wuk_p, wvt,
                                            tables)
        ret_out = _retention(rq, rk, rv, rg, ret_gn_w[i][None])
        mla_out = _mla_attention(q, k, vt)
        h = _post(ret_out, mla_out, h, p[i], w_o[i].astype(BF16), post_mix_norm[i][None],
                  pre_ffn_norm[i][None], w_gate[i].astype(BF16), w_up[i].astype(BF16),
                  w_down[i].astype(BF16), post_ffn_norm[i][None], w_ple_proj[i].astype(BF16),
                  ple_norm[i][None], w_ple_gate[i].astype(BF16), b_ple_gate[i][None])
    return h
```

```python
import math
from functools import partial

import jax
import jax.numpy as jnp
from jax import lax
from jax.experimental import pallas as pl
from jax.experimental.pallas import tpu as pltpu

D_MODEL = 1024
PLE_DIM = 256
RET_HEADS = 4
RET_HEAD_DIM = 128
RET_WIDTH = RET_HEADS * RET_HEAD_DIM
MLA_HEADS = 8
MLA_NOPE_DIM = 64
MLA_ROPE_DIM = 32
MLA_QK_DIM = MLA_NOPE_DIM + MLA_ROPE_DIM
MLA_V_DIM = 64
MLA_WIDTH = MLA_HEADS * MLA_V_DIM
MLA_Q_LORA = 384
MLA_KV_LORA = 256
D_FF = 2816
ROPE_BASE = 10000.0
EPS = 1e-6
NEG_BIG = -1e30

LANES = 128
HEAD_SLAB = 128
IN_COLS_PAD = 4 * RET_WIDTH + MLA_Q_LORA + MLA_KV_LORA + LANES
VMEM_LIMIT = 56 * 1024 * 1024

TM_PROJ = 512
RET_CHUNK = 128
TM_RET = 512
TQ = 1024
TK = 512
ONES_ROWS = 16
TM_POST = 512
FF_CHUNK = 256

BF16 = jnp.bfloat16
F32 = jnp.float32


def _dot(a, b):
    return jnp.dot(a, b, preferred_element_type=F32)


def _dot_nt(a, b):
    return lax.dot_general(a, b, (((1,), (1,)), ((), ())), preferred_element_type=F32)


def _dot_tn(a, b):
    return lax.dot_general(a, b, (((0,), (0,)), ((), ())), preferred_element_type=F32)


def _rms(v, w):
    return v * lax.rsqrt(jnp.mean(v * v, axis=-1, keepdims=True) + EPS) * w


def _rope_tables_body(pos_ref, invr_ref, invm_ref, sgnr_ref, sgnm_ref, onesm_ref,
                      cr_ref, sr_ref, cm_ref, sm_ref):
    pos = pos_ref[0].astype(F32)
    ang_r = pos * invr_ref[...]
    cr_ref[0] = jnp.cos(ang_r)
    sr_ref[0] = jnp.sin(ang_r) * sgnr_ref[...]
    ang_m = pos * invm_ref[...]
    cm_ref[0] = jnp.cos(ang_m) * onesm_ref[...]
    sm_ref[0] = jnp.sin(ang_m) * sgnm_ref[...]


def _rope_tables(positions):
    B, S = positions.shape
    tm = 1024
    half_r = RET_HEAD_DIM // 2
    inv_r = 1.0 / (ROPE_BASE ** (jnp.arange(half_r, dtype=F32) / half_r))
    inv_r = jnp.concatenate([inv_r, inv_r])[None, :]
    sgn_r = jnp.concatenate([-jnp.ones((half_r,), F32), jnp.ones((half_r,), F32)])[None, :]
    half_m = MLA_ROPE_DIM // 2
    inv_m = 1.0 / (ROPE_BASE ** (jnp.arange(half_m, dtype=F32) / half_m))
    z64 = jnp.zeros((MLA_NOPE_DIM,), F32)
    z32 = jnp.zeros((HEAD_SLAB - MLA_QK_DIM,), F32)
    inv_m = jnp.concatenate([z64, inv_m, inv_m, z32])[None, :]
    sgn_m = jnp.concatenate([z64, -jnp.ones((half_m,), F32), jnp.ones((half_m,), F32), z32])[None, :]
    ones_m = jnp.concatenate([jnp.ones((MLA_QK_DIM,), F32), z32])[None, :]
    pos3 = positions.reshape(B, S, 1)
    vec = pl.BlockSpec((1, LANES), lambda b, i: (0, 0))
    tab = pl.BlockSpec((1, tm, LANES), lambda b, i: (b, i, 0))
    out = jax.ShapeDtypeStruct((B, S, LANES), F32)
    return pl.pallas_call(
        _rope_tables_body,
        grid=(B, S // tm),
        in_specs=[pl.BlockSpec((1, tm, 1), lambda b, i: (b, i, 0)), vec, vec, vec, vec, vec],
        out_specs=[tab, tab, tab, tab],
        out_shape=[out, out, out, out],
        compiler_params=pltpu.CompilerParams(dimension_semantics=("arbitrary", "arbitrary"),
                                             vmem_limit_bytes=VMEM_LIMIT),
        name="rope_tables",
    )(pos3, inv_r, inv_m, sgn_r, sgn_m, ones_m)


def _in_proj_body(x_ref, g_ref, win_ref, qn_ref, wuq_ref, kvn_ref, wuk_ref, wvt_ref,
                  cr_ref, sr_ref, cm_ref, sm_ref,
                  rq_ref, rk_ref, rv_ref, rg_ref, q_ref, k_ref, vt_ref, *, q_scale):
    xn = _rms(x_ref[0], g_ref[...]).astype(BF16)
    cr = cr_ref[0]
    sr = sr_ref[0]

    def ret_rope(v):
        return v * cr + pltpu.roll(v, RET_HEAD_DIM // 2, 1) * sr

    pq = _dot(xn, win_ref[:, 0:RET_WIDTH])
    pk = _dot(xn, win_ref[:, RET_WIDTH:2 * RET_WIDTH])
    for h in range(RET_HEADS):
        sl = slice(h * RET_HEAD_DIM, (h + 1) * RET_HEAD_DIM)
        rq_ref[0, :, sl] = ret_rope(pq[:, sl]).astype(BF16)
        rk_ref[0, :, sl] = (ret_rope(pk[:, sl]) * (RET_HEAD_DIM ** -0.5)).astype(BF16)
    rv_ref[0] = _dot(xn, win_ref[:, 2 * RET_WIDTH:3 * RET_WIDTH]).astype(BF16)
    rg_ref[0] = _dot(xn, win_ref[:, 3 * RET_WIDTH:4 * RET_WIDTH]).astype(BF16)

    cm = cm_ref[0]
    sm = sm_ref[0]
    lane = lax.broadcasted_iota(jnp.int32, cm.shape, 1)
    upper = lane >= (MLA_NOPE_DIM + MLA_ROPE_DIM // 2)

    def mla_rope(v):
        swapped = jnp.where(upper, pltpu.roll(v, MLA_ROPE_DIM // 2, 1),
                            pltpu.roll(v, LANES - MLA_ROPE_DIM // 2, 1))
        return v * cm + swapped * sm

    o = 4 * RET_WIDTH
    cq = _dot(xn, win_ref[:, o:o + MLA_Q_LORA])
    cqn = _rms(cq, qn_ref[...]).astype(BF16)
    qh = _dot(cqn, wuq_ref[...])
    for h in range(MLA_HEADS):
        sl = slice(h * HEAD_SLAB, (h + 1) * HEAD_SLAB)
        q_ref[0, :, sl] = (mla_rope(qh[:, sl]) * q_scale).astype(BF16)

    o += MLA_Q_LORA
    ckv = _dot(xn, win_ref[:, o:o + MLA_KV_LORA])
    ckvn = _rms(ckv, kvn_ref[...]).astype(BF16)
    o += MLA_KV_LORA
    kr = mla_rope(_dot(xn, win_ref[:, o:o + LANES]))
    kn = _dot(ckvn, wuk_ref[...])
    for h in range(MLA_HEADS):
        sl = slice(h * HEAD_SLAB, (h + 1) * HEAD_SLAB)
        k_ref[0, :, sl] = (kn[:, sl] + kr).astype(BF16)
    vt = _dot_nt(wvt_ref[...], ckvn).astype(BF16)
    for blk in range(TM_PROJ // TK):
        vt_ref[0, blk] = vt[:, blk * TK:(blk + 1) * TK]


def _in_proj(x, pre_mix_norm, w_in_p, q_norm, wuq_p, kv_norm, wuk_p, wvt, tables):
    B, S, D = x.shape
    tm = TM_PROJ
    cr, sr, cm, sm = tables
    q_scale = (1.0 / math.sqrt(MLA_QK_DIM)) * math.log2(math.e)
    const = lambda shape: pl.BlockSpec(shape, lambda b, i: (0,) * len(shape))
    tile = lambda w: pl.BlockSpec((1, tm, w), lambda b, i: (b, i, 0))
    bf = lambda w: jax.ShapeDtypeStruct((B, S, w), BF16)
    return pl.pallas_call(
        partial(_in_proj_body, q_scale=q_scale),
        grid=(B, S // tm),
        in_specs=[tile(D), const((1, D)), const((D, IN_COLS_PAD)),
                  const((1, MLA_Q_LORA)), const((MLA_Q_LORA, MLA_HEADS * HEAD_SLAB)),
                  const((1, MLA_KV_LORA)), const((MLA_KV_LORA, MLA_HEADS * HEAD_SLAB)),
                  const((MLA_WIDTH, MLA_KV_LORA)),
                  tile(LANES), tile(LANES), tile(LANES), tile(LANES)],
        out_specs=[tile(RET_WIDTH), tile(RET_WIDTH), tile(RET_WIDTH), tile(RET_WIDTH),
                   tile(MLA_HEADS * HEAD_SLAB), tile(MLA_HEADS * HEAD_SLAB),
                   pl.BlockSpec((1, tm // TK, MLA_WIDTH, TK), lambda b, i: (b, i, 0, 0))],
        out_shape=[bf(RET_WIDTH), bf(RET_WIDTH), bf(RET_WIDTH), bf(RET_WIDTH),
                   bf(MLA_HEADS * HEAD_SLAB), bf(MLA_HEADS * HEAD_SLAB),
                   jax.ShapeDtypeStruct((B, S // TK, MLA_WIDTH, TK), BF16)],
        compiler_params=pltpu.CompilerParams(dimension_semantics=("arbitrary", "arbitrary"),
                                             vmem_limit_bytes=VMEM_LIMIT),
        name="in_proj",
    )(x, pre_mix_norm, w_in_p, q_norm, wuq_p, kv_norm, wuk_p, wvt, cr, sr, cm, sm)


def _retention_body(rq_ref, rk_ref, rv_ref, rg_ref, dmask_ref, zeta_ref, xi_ref, gnw_ref,
                    o_ref, state_ref, *, g_chunk):
    @pl.when(pl.program_id(1) == 0)
    def _():
        state_ref[...] = jnp.zeros_like(state_ref)

    C = RET_CHUNK
    for c in range(TM_RET // C):
        rows = slice(c * C, (c + 1) * C)
        for h in range(RET_HEADS):
            cols = slice(h * RET_HEAD_DIM, (h + 1) * RET_HEAD_DIM)
            q = rq_ref[0, rows, cols]
            k = rk_ref[0, rows, cols]
            v = rv_ref[0, rows, cols]
            scores = _dot_nt(q, k) * dmask_ref[h]
            inner = _dot(scores.astype(BF16), v)
            r_prev = state_ref[h]
            cross = _dot(q, r_prev.astype(BF16)) * xi_ref[h]
            kz = (k.astype(F32) * zeta_ref[h]).astype(BF16)
            state_ref[h] = g_chunk[h] * r_prev + _dot_tn(kz, v)
            y = inner + cross
            mu = jnp.mean(y, axis=-1, keepdims=True)
            yc = y - mu
            var = jnp.mean(yc * yc, axis=-1, keepdims=True)
            yn = yc * lax.rsqrt(var + EPS) * gnw_ref[:, cols]
            gate = rg_ref[0, rows, cols].astype(F32)
            o_ref[0, rows, cols] = (gate * jax.nn.sigmoid(gate) * yn).astype(BF16)


def _retention(rq, rk, rv, rg, ret_gn_w):
    B, S, W = rq.shape
    C = RET_CHUNK
    H = RET_HEADS
    log_g = jnp.log(1.0 - 2.0 ** (-5.0 - jnp.arange(H, dtype=F32)))
    j = jnp.arange(C, dtype=F32)
    diff = j[:, None] - j[None, :]
    dmask = jnp.where(diff[None] >= 0, jnp.exp(jnp.maximum(diff, 0.0)[None] * log_g[:, None, None]), 0.0)
    zeta = jnp.exp((C - 1 - j)[None, :] * log_g[:, None])
    xi = jnp.exp((j + 1)[None, :] * log_g[:, None])
    zeta_b = jnp.broadcast_to(zeta[:, :, None], (H, C, RET_HEAD_DIM))
    xi_b = jnp.broadcast_to(xi[:, :, None], (H, C, RET_HEAD_DIM))
    g_chunk = tuple(float((1.0 - 2.0 ** (-5.0 - h)) ** C) for h in range(H))
    tile = pl.BlockSpec((1, TM_RET, W), lambda b, i: (b, i, 0))
    const3 = pl.BlockSpec((H, C, C), lambda b, i: (0, 0, 0))
    return pl.pallas_call(
        partial(_retention_body, g_chunk=g_chunk),
        grid=(B, S // TM_RET),
        in_specs=[tile, tile, tile, tile, const3, const3, const3,
                  pl.BlockSpec((1, W), lambda b, i: (0, 0))],
        out_specs=tile,
        out_shape=jax.ShapeDtypeStruct((B, S, W), BF16),
        scratch_shapes=[pltpu.VMEM((H, RET_HEAD_DIM, RET_HEAD_DIM), F32)],
        compiler_params=pltpu.CompilerParams(dimension_semantics=("arbitrary", "arbitrary"),
                                             vmem_limit_bytes=VMEM_LIMIT),
        name="retention",
    )(rq, rk, rv, rg, dmask, zeta_b, xi_b, ret_gn_w)


HEADS_PER_STEP = 2
DIAG_BLOCKS = TQ // TK


def _attn_body(q_ref, k_ref, vt_ref, o_ref, s_ref, mb_ref):
    nq = q_ref.shape[1] // TQ
    ones = jnp.ones((ONES_ROWS, TK), BF16)
    heads = range(HEADS_PER_STEP)
    row = lax.broadcasted_iota(jnp.int32, (TK, TK), 0)
    col = lax.broadcasted_iota(jnp.int32, (TK, TK), 1)
    tri = row <= col

    def scores(qstart, j, slot, width):
        kstart = pl.multiple_of(j * TK, TK)
        q0 = pl.multiple_of(qstart + (TQ - width), TK)
        for hh in heads:
            cols = slice(hh * HEAD_SLAB, (hh + 1) * HEAD_SLAB)
            st = _dot_nt(k_ref[0, pl.ds(kstart, TK), cols], q_ref[0, pl.ds(q0, width), cols])
            s_ref[slot, hh, :, 0:width] = st
            mb_ref[slot, hh, :, 0:width] = jnp.max(st, axis=0, keepdims=True)

    def vte(j, hh):
        return jnp.concatenate([vt_ref[0, j, hh * MLA_V_DIM:(hh + 1) * MLA_V_DIM, :], ones], axis=0)

    def update(m, acc, st, blk_max, j, hh):
        m_new = jnp.maximum(m, blk_max)
        p = jnp.exp2(st - m_new).astype(BF16)
        return m_new, acc * jnp.exp2(m - m_new) + _dot(vte(j, hh), p)

    def accumulate(j, slot, carry):
        return tuple(update(m, acc, s_ref[slot, hh], mb_ref[slot, hh], j, hh)
                     for hh, (m, acc) in enumerate(carry))

    def accumulate_diag(j, slot, carry, width):
        off = TQ - width
        out = []
        for hh, (m, acc) in enumerate(carry):
            st = s_ref[slot, hh, :, 0:width]
            masked = jnp.where(tri, st[:, :TK], NEG_BIG)
            st = masked if width == TK else jnp.concatenate([masked, st[:, TK:]], axis=1)
            m_w, acc_w = update(m[:, off:], acc[:, off:], st, jnp.max(st, axis=0, keepdims=True), j, hh)
            if off:
                m_w = jnp.concatenate([m[:, :off], m_w], axis=1)
                acc_w = jnp.concatenate([acc[:, :off], acc_w], axis=1)
            out.append((m_w, acc_w))
        return tuple(out)

    def tile(qi, _):
        qstart = qi * TQ

        def pair(t, carry):
            scores(qstart, 2 * t + 1, 1, TQ)
            carry = accumulate(2 * t, 0, carry)
            scores(qstart, 2 * t + 2, 0, TQ)
            return accumulate(2 * t + 1, 1, carry)

        init = tuple((jnp.full((1, TQ), NEG_BIG, F32), jnp.zeros((MLA_V_DIM + ONES_ROWS, TQ), F32))
                     for _ in heads)
        d0 = DIAG_BLOCKS * qi
        carry = lax.fori_loop(0, d0 // 2, pair, init)
        for d in range(DIAG_BLOCKS):
            if d + 1 < DIAG_BLOCKS:
                scores(qstart, d0 + d + 1, (d + 1) % 2, TQ - (d + 1) * TK)
            else:
                scores(jnp.minimum(qi + 1, nq - 1) * TQ, 0, (d + 1) % 2, TQ)
            carry = accumulate_diag(d0 + d, d % 2, carry, TQ - d * TK)
        for hh, (_, acc) in enumerate(carry):
            o_ref[0, qi, hh * MLA_V_DIM:(hh + 1) * MLA_V_DIM, :] = (
                acc[:MLA_V_DIM] / acc[MLA_V_DIM:MLA_V_DIM + 1]).astype(BF16)
        return 0

    scores(0, 0, 0, TQ)
    lax.fori_loop(0, nq, tile, 0)


def _mla_attention(q, k, vt):
    B, S, _ = q.shape
    nkb = S // TK
    nq = S // TQ
    hp = HEADS_PER_STEP
    return pl.pallas_call(
        _attn_body,
        grid=(B, MLA_HEADS // hp),
        in_specs=[pl.BlockSpec((1, S, hp * HEAD_SLAB), lambda b, g: (b, 0, g)),
                  pl.BlockSpec((1, S, hp * HEAD_SLAB), lambda b, g: (b, 0, g)),
                  pl.BlockSpec((1, nkb, hp * MLA_V_DIM, TK), lambda b, g: (b, 0, g, 0))],
        out_specs=pl.BlockSpec((1, nq, hp * MLA_V_DIM, TQ), lambda b, g: (b, 0, g, 0)),
        out_shape=jax.ShapeDtypeStruct((B, nq, MLA_WIDTH, TQ), BF16),
        scratch_shapes=[pltpu.VMEM((2, hp, TK, TQ), F32), pltpu.VMEM((2, hp, 1, TQ), F32)],
        compiler_params=pltpu.CompilerParams(dimension_semantics=("arbitrary", "arbitrary"),
                                             vmem_limit_bytes=VMEM_LIMIT),
        name="mla_attn",
    )(q, k, vt)


def _post_body(ret_ref, mla_ref, x_ref, p_ref, wo_ref, pmn_ref, pfn_ref, wg_ref, wu_ref, wd_ref,
               pofn_ref, wpp_ref, plen_ref, wpg_ref, bpg_ref, o_ref, act_ref):
    mix = _dot(ret_ref[0], wo_ref[0:RET_WIDTH, :]) + _dot_tn(mla_ref[0, 0], wo_ref[RET_WIDTH:, :])
    h1 = x_ref[0] + _rms(mix, pmn_ref[...])
    hn = _rms(h1, pfn_ref[...]).astype(BF16)
    for c in range(D_FF // FF_CHUNK):
        cols = slice(c * FF_CHUNK, (c + 1) * FF_CHUNK)
        g = _dot(hn, wg_ref[:, cols])
        u = _dot(hn, wu_ref[:, cols])
        act_ref[:, cols] = (g * jax.nn.sigmoid(g) * u).astype(BF16)
    ff = _dot(act_ref[...], wd_ref[...])
    h2 = h1 + _rms(ff, pofn_ref[...])
    e = _rms(_dot(p_ref[0].astype(BF16), wpp_ref[...]), plen_ref[...])
    gate = jax.nn.sigmoid(_dot(h2.astype(BF16), wpg_ref[...]) + bpg_ref[...])
    o_ref[0] = h2 + e * gate


def _post(ret_out, mla_out, x, p, w_o, post_mix_norm, pre_ffn_norm, w_gate, w_up, w_down,
          post_ffn_norm, w_ple_proj, ple_norm, w_ple_gate, b_ple_gate):
    B, S, D = x.shape
    tm = TM_POST
    const = lambda shape: pl.BlockSpec(shape, lambda b, i: (0,) * len(shape),
                                       pipeline_mode=pl.Buffered(1))
    tile = lambda w: pl.BlockSpec((1, tm, w), lambda b, i: (b, i, 0))
    return pl.pallas_call(
        _post_body,
        grid=(B, S // tm),
        in_specs=[tile(RET_WIDTH),
                  pl.BlockSpec((1, 1, MLA_WIDTH, tm), lambda b, i: (b, i // (TQ // tm), 0, i % (TQ // tm))),
                  tile(D), tile(PLE_DIM),
                  const((D, D)), const((1, D)), const((1, D)),
                  const((D, D_FF)), const((D, D_FF)), const((D_FF, D)), const((1, D)),
                  const((PLE_DIM, D)), const((1, D)), const((D, D)), const((1, D))],
        out_specs=tile(D),
        out_shape=jax.ShapeDtypeStruct((B, S, D), F32),
        scratch_shapes=[pltpu.VMEM((tm, D_FF), BF16)],
        compiler_params=pltpu.CompilerParams(dimension_semantics=("arbitrary", "arbitrary"),
                                             vmem_limit_bytes=VMEM_LIMIT),
        name="post",
    )(ret_out, mla_out, x, p, w_o, post_mix_norm, pre_ffn_norm, w_gate, w_up, w_down,
      post_ffn_norm, w_ple_proj, ple_norm, w_ple_gate, b_ple_gate)


def _prep_w_in(w_in):
    base = 4 * RET_WIDTH + MLA_Q_LORA + MLA_KV_LORA
    kr = w_in[:, base:base + MLA_ROPE_DIM]
    z = jnp.zeros((w_in.shape[0], MLA_NOPE_DIM), w_in.dtype)
    zt = jnp.zeros((w_in.shape[0], LANES - MLA_QK_DIM), w_in.dtype)
    return jnp.concatenate([w_in[:, :base], z, kr, zt], axis=1).astype(BF16)


def _pad_heads(w, per_head, keep):
    K = w.shape[0]
    w = w.reshape(K, MLA_HEADS, per_head)[:, :, :keep]
    w = jnp.pad(w, ((0, 0), (0, 0), (0, HEAD_SLAB - keep)))
    return w.reshape(K, MLA_HEADS * HEAD_SLAB).astype(BF16)


def kernel(x, p, positions, pre_mix_norm, w_in, ret_gn_w, mla_q_norm, w_uq, mla_kv_norm, w_ukv, w_o,
           post_mix_norm, pre_ffn_norm, w_gate, w_up, w_down, post_ffn_norm, w_ple_proj, ple_norm,
           w_ple_gate, b_ple_gate):
    depth = pre_mix_norm.shape[0]
    tables = _rope_tables(positions)
    h = x
    for i in range(depth):
        wuq_p = _pad_heads(w_uq[i], MLA_QK_DIM, MLA_QK_DIM)
        wuk_p = _pad_heads(w_ukv[i], MLA_NOPE_DIM + MLA_V_DIM, MLA_NOPE_DIM)
        wv = w_ukv[i].reshape(MLA_KV_LORA, MLA_HEADS, MLA_NOPE_DIM + MLA_V_DIM)[:, :, MLA_NOPE_DIM:]
        wvt = wv.reshape(MLA_KV_LORA, MLA_WIDTH).T.astype(BF16)
        rq, rk, rv, rg, q, k, vt = _in_proj(h, pre_mix_norm[i][None], _prep_w_in(w_in[i]),
                                            mla_q_norm[i][None], wuq_p, mla_kv_norm[i][None], wuk_p, wvt,
                                            tables)
        ret_out = _retention(rq, rk, rv, rg, ret_gn_w[i][None])
        mla_out = _mla_attention(q, k, vt)
        h = _post(ret_out, mla_out, h, p[i], w_o[i].astype(BF16), post_mix_norm[i][None],
                  pre_ffn_norm[i][None], w_gate[i].astype(BF16), w_up[i].astype(BF16),
                  w_down[i].astype(BF16), post_ffn_norm[i][None], w_ple_proj[i].astype(BF16),
                  ple_norm[i][None], w_ple_gate[i].astype(BF16), b_ple_gate[i][None])
    return h
```

```python
import math
from functools import partial

import jax
import jax.numpy as jnp
from jax import lax
from jax.experimental import pallas as pl
from jax.experimental.pallas import tpu as pltpu

D_MODEL = 1024
PLE_DIM = 256
RET_HEADS = 4
RET_HEAD_DIM = 128
RET_WIDTH = RET_HEADS * RET_HEAD_DIM
MLA_HEADS = 8
MLA_NOPE_DIM = 64
MLA_ROPE_DIM = 32
MLA_QK_DIM = MLA_NOPE_DIM + MLA_ROPE_DIM
MLA_V_DIM = 64
MLA_WIDTH = MLA_HEADS * MLA_V_DIM
MLA_Q_LORA = 384
MLA_KV_LORA = 256
D_FF = 2816
ROPE_BASE = 10000.0
EPS = 1e-6
NEG_BIG = -1e30

LANES = 128
HEAD_SLAB = 128
IN_COLS_PAD = 4 * RET_WIDTH + MLA_Q_LORA + MLA_KV_LORA + LANES
VMEM_LIMIT = 56 * 1024 * 1024

TM_PROJ = 512
RET_CHUNK = 128
TM_RET = 512
TQ = 2048
TK = 512
ONES_ROWS = 16
TM_POST = 512
FF_CHUNK = 256

BF16 = jnp.bfloat16
F32 = jnp.float32


def _dot(a, b):
    return jnp.dot(a, b, preferred_element_type=F32)


def _dot_nt(a, b):
    return lax.dot_general(a, b, (((1,), (1,)), ((), ())), preferred_element_type=F32)


def _dot_tn(a, b):
    return lax.dot_general(a, b, (((0,), (0,)), ((), ())), preferred_element_type=F32)


def _rms(v, w):
    return v * lax.rsqrt(jnp.mean(v * v, axis=-1, keepdims=True) + EPS) * w


def _rope_tables_body(pos_ref, invr_ref, invm_ref, sgnr_ref, sgnm_ref, onesm_ref,
                      cr_ref, sr_ref, cm_ref, sm_ref):
    pos = pos_ref[0].astype(F32)
    ang_r = pos * invr_ref[...]
    cr_ref[0] = jnp.cos(ang_r)
    sr_ref[0] = jnp.sin(ang_r) * sgnr_ref[...]
    ang_m = pos * invm_ref[...]
    cm_ref[0] = jnp.cos(ang_m) * onesm_ref[...]
    sm_ref[0] = jnp.sin(ang_m) * sgnm_ref[...]


def _rope_tables(positions):
    B, S = positions.shape
    tm = 1024
    half_r = RET_HEAD_DIM // 2
    inv_r = 1.0 / (ROPE_BASE ** (jnp.arange(half_r, dtype=F32) / half_r))
    inv_r = jnp.concatenate([inv_r, inv_r])[None, :]
    sgn_r = jnp.concatenate([-jnp.ones((half_r,), F32), jnp.ones((half_r,), F32)])[None, :]
    half_m = MLA_ROPE_DIM // 2
    inv_m = 1.0 / (ROPE_BASE ** (jnp.arange(half_m, dtype=F32) / half_m))
    z64 = jnp.zeros((MLA_NOPE_DIM,), F32)
    z32 = jnp.zeros((HEAD_SLAB - MLA_QK_DIM,), F32)
    inv_m = jnp.concatenate([z64, inv_m, inv_m, z32])[None, :]
    sgn_m = jnp.concatenate([z64, -jnp.ones((half_m,), F32), jnp.ones((half_m,), F32), z32])[None, :]
    ones_m = jnp.concatenate([jnp.ones((MLA_QK_DIM,), F32), z32])[None, :]
    pos3 = positions.reshape(B, S, 1)
    vec = pl.BlockSpec((1, LANES), lambda b, i: (0, 0))
    tab = pl.BlockSpec((1, tm, LANES), lambda b, i: (b, i, 0))
    out = jax.ShapeDtypeStruct((B, S, LANES), F32)
    return pl.pallas_call(
        _rope_tables_body,
        grid=(B, S // tm),
        in_specs=[pl.BlockSpec((1, tm, 1), lambda b, i: (b, i, 0)), vec, vec, vec, vec, vec],
        out_specs=[tab, tab, tab, tab],
        out_shape=[out, out, out, out],
        compiler_params=pltpu.CompilerParams(dimension_semantics=("arbitrary", "arbitrary"),
                                             vmem_limit_bytes=VMEM_LIMIT),
        name="rope_tables",
    )(pos3, inv_r, inv_m, sgn_r, sgn_m, ones_m)


def _in_proj_body(x_ref, g_ref, win_ref, qn_ref, wuq_ref, kvn_ref, wuk_ref, wvt_ref,
                  cr_ref, sr_ref, cm_ref, sm_ref,
                  rq_ref, rk_ref, rv_ref, rg_ref, q_ref, k_ref, vt_ref, *, q_scale):
    xn = _rms(x_ref[0], g_ref[...]).astype(BF16)
    cr = cr_ref[0]
    sr = sr_ref[0]

    def ret_rope(v):
        return v * cr + pltpu.roll(v, RET_HEAD_DIM // 2, 1) * sr

    pq = _dot(xn, win_ref[:, 0:RET_WIDTH])
    pk = _dot(xn, win_ref[:, RET_WIDTH:2 * RET_WIDTH])
    for h in range(RET_HEADS):
        sl = slice(h * RET_HEAD_DIM, (h + 1) * RET_HEAD_DIM)
        rq_ref[0, :, sl] = ret_rope(pq[:, sl]).astype(BF16)
        rk_ref[0, :, sl] = (ret_rope(pk[:, sl]) * (RET_HEAD_DIM ** -0.5)).astype(BF16)
    rv_ref[0] = _dot(xn, win_ref[:, 2 * RET_WIDTH:3 * RET_WIDTH]).astype(BF16)
    rg_ref[0] = _dot(xn, win_ref[:, 3 * RET_WIDTH:4 * RET_WIDTH]).astype(BF16)

    cm = cm_ref[0]
    sm = sm_ref[0]
    lane = lax.broadcasted_iota(jnp.int32, cm.shape, 1)
    upper = lane >= (MLA_NOPE_DIM + MLA_ROPE_DIM // 2)

    def mla_rope(v):
        swapped = jnp.where(upper, pltpu.roll(v, MLA_ROPE_DIM // 2, 1),
                            pltpu.roll(v, LANES - MLA_ROPE_DIM // 2, 1))
        return v * cm + swapped * sm

    o = 4 * RET_WIDTH
    cq = _dot(xn, win_ref[:, o:o + MLA_Q_LORA])
    cqn = _rms(cq, qn_ref[...]).astype(BF16)
    qh = _dot(cqn, wuq_ref[...])
    for h in range(MLA_HEADS):
        sl = slice(h * HEAD_SLAB, (h + 1) * HEAD_SLAB)
        q_ref[0, :, sl] = (mla_rope(qh[:, sl]) * q_scale).astype(BF16)

    o += MLA_Q_LORA
    ckv = _dot(xn, win_ref[:, o:o + MLA_KV_LORA])
    ckvn = _rms(ckv, kvn_ref[...]).astype(BF16)
    o += MLA_KV_LORA
    kr = mla_rope(_dot(xn, win_ref[:, o:o + LANES]))
    kn = _dot(ckvn, wuk_ref[...])
    for h in range(MLA_HEADS):
        sl = slice(h * HEAD_SLAB, (h + 1) * HEAD_SLAB)
        k_ref[0, :, sl] = (kn[:, sl] + kr).astype(BF16)
    vt = _dot_nt(wvt_ref[...], ckvn).astype(BF16)
    for blk in range(TM_PROJ // TK):
        vt_ref[0, blk] = vt[:, blk * TK:(blk + 1) * TK]


def _in_proj(x, pre_mix_norm, w_in_p, q_norm, wuq_p, kv_norm, wuk_p, wvt, tables):
    B, S, D = x.shape
    tm = TM_PROJ
    cr, sr, cm, sm = tables
    q_scale = (1.0 / math.sqrt(MLA_QK_DIM)) * math.log2(math.e)
    const = lambda shape: pl.BlockSpec(shape, lambda b, i: (0,) * len(shape))
    tile = lambda w: pl.BlockSpec((1, tm, w), lambda b, i: (b, i, 0))
    bf = lambda w: jax.ShapeDtypeStruct((B, S, w), BF16)
    return pl.pallas_call(
        partial(_in_proj_body, q_scale=q_scale),
        grid=(B, S // tm),
        in_specs=[tile(D), const((1, D)), const((D, IN_COLS_PAD)),
                  const((1, MLA_Q_LORA)), const((MLA_Q_LORA, MLA_HEADS * HEAD_SLAB)),
                  const((1, MLA_KV_LORA)), const((MLA_KV_LORA, MLA_HEADS * HEAD_SLAB)),
                  const((MLA_WIDTH, MLA_KV_LORA)),
                  tile(LANES), tile(LANES), tile(LANES), tile(LANES)],
        out_specs=[tile(RET_WIDTH), tile(RET_WIDTH), tile(RET_WIDTH), tile(RET_WIDTH),
                   tile(MLA_HEADS * HEAD_SLAB), tile(MLA_HEADS * HEAD_SLAB),
                   pl.BlockSpec((1, tm // TK, MLA_WIDTH, TK), lambda b, i: (b, i, 0, 0))],
        out_shape=[bf(RET_WIDTH), bf(RET_WIDTH), bf(RET_WIDTH), bf(RET_WIDTH),
                   bf(MLA_HEADS * HEAD_SLAB), bf(MLA_HEADS * HEAD_SLAB),
                   jax.ShapeDtypeStruct((B, S // TK, MLA_WIDTH, TK), BF16)],
        compiler_params=pltpu.CompilerParams(dimension_semantics=("arbitrary", "arbitrary"),
                                             vmem_limit_bytes=VMEM_LIMIT),
        name="in_proj",
    )(x, pre_mix_norm, w_in_p, q_norm, wuq_p, kv_norm, wuk_p, wvt, cr, sr, cm, sm)


def _retention_body(rq_ref, rk_ref, rv_ref, rg_ref, dmask_ref, zeta_ref, xi_ref, gnw_ref,
                    o_ref, state_ref, *, g_chunk):
    @pl.when(pl.program_id(1) == 0)
    def _():
        state_ref[...] = jnp.zeros_like(state_ref)

    C = RET_CHUNK
    for c in range(TM_RET // C):
        rows = slice(c * C, (c + 1) * C)
        for h in range(RET_HEADS):
            cols = slice(h * RET_HEAD_DIM, (h + 1) * RET_HEAD_DIM)
            q = rq_ref[0, rows, cols]
            k = rk_ref[0, rows, cols]
            v = rv_ref[0, rows, cols]
            scores = _dot_nt(q, k) * dmask_ref[h]
            inner = _dot(scores.astype(BF16), v)
            r_prev = state_ref[h]
            cross = _dot(q, r_prev.astype(BF16)) * xi_ref[h]
            kz = (k.astype(F32) * zeta_ref[h]).astype(BF16)
            state_ref[h] = g_chunk[h] * r_prev + _dot_tn(kz, v)
            y = inner + cross
            mu = jnp.mean(y, axis=-1, keepdims=True)
            yc = y - mu
            var = jnp.mean(yc * yc, axis=-1, keepdims=True)
            yn = yc * lax.rsqrt(var + EPS) * gnw_ref[:, cols]
            gate = rg_ref[0, rows, cols].astype(F32)
            o_ref[0, rows, cols] = (gate * jax.nn.sigmoid(gate) * yn).astype(BF16)


def _retention(rq, rk, rv, rg, ret_gn_w):
    B, S, W = rq.shape
    C = RET_CHUNK
    H = RET_HEADS
    log_g = jnp.log(1.0 - 2.0 ** (-5.0 - jnp.arange(H, dtype=F32)))
    j = jnp.arange(C, dtype=F32)
    diff = j[:, None] - j[None, :]
    dmask = jnp.where(diff[None] >= 0, jnp.exp(jnp.maximum(diff, 0.0)[None] * log_g[:, None, None]), 0.0)
    zeta = jnp.exp((C - 1 - j)[None, :] * log_g[:, None])
    xi = jnp.exp((j + 1)[None, :] * log_g[:, None])
    zeta_b = jnp.broadcast_to(zeta[:, :, None], (H, C, RET_HEAD_DIM))
    xi_b = jnp.broadcast_to(xi[:, :, None], (H, C, RET_HEAD_DIM))
    g_chunk = tuple(float((1.0 - 2.0 ** (-5.0 - h)) ** C) for h in range(H))
    tile = pl.BlockSpec((1, TM_RET, W), lambda b, i: (b, i, 0))
    const3 = pl.BlockSpec((H, C, C), lambda b, i: (0, 0, 0))
    return pl.pallas_call(
        partial(_retention_body, g_chunk=g_chunk),
        grid=(B, S // TM_RET),
        in_specs=[tile, tile, tile, tile, const3, const3, const3,
                  pl.BlockSpec((1, W), lambda b, i: (0, 0))],
        out_specs=tile,
        out_shape=jax.ShapeDtypeStruct((B, S, W), BF16),
        scratch_shapes=[pltpu.VMEM((H, RET_HEAD_DIM, RET_HEAD_DIM), F32)],
        compiler_params=pltpu.CompilerParams(dimension_semantics=("arbitrary", "arbitrary"),
                                             vmem_limit_bytes=VMEM_LIMIT),
        name="retention",
    )(rq, rk, rv, rg, dmask, zeta_b, xi_b, ret_gn_w)


HEADS_PER_STEP = 2
DIAG_BLOCKS = TQ // TK


def _attn_body(q_ref, k_ref, vt_ref, o_ref, s_ref, mb_ref):
    nq = q_ref.shape[1] // TQ
    ones = jnp.ones((ONES_ROWS, TK), BF16)
    heads = range(HEADS_PER_STEP)
    row = lax.broadcasted_iota(jnp.int32, (TK, TK), 0)
    col = lax.broadcasted_iota(jnp.int32, (TK, TK), 1)
    tri = row <= col

    def scores(qstart, j, slot, width):
        kstart = pl.multiple_of(j * TK, TK)
        q0 = pl.multiple_of(qstart + (TQ - width), TK)
        for hh in heads:
            cols = slice(hh * HEAD_SLAB, (hh + 1) * HEAD_SLAB)
            st = _dot_nt(k_ref[0, pl.ds(kstart, TK), cols], q_ref[0, pl.ds(q0, width), cols])
            s_ref[slot, hh, :, 0:width] = st
            mb_ref[slot, hh, :, 0:width] = jnp.max(st, axis=0, keepdims=True)

    def vte(j, hh):
        return jnp.concatenate([vt_ref[0, j, hh * MLA_V_DIM:(hh + 1) * MLA_V_DIM, :], ones], axis=0)

    def update(m, acc, st, blk_max, j, hh):
        m_new = jnp.maximum(m, blk_max)
        p = jnp.exp2(st - m_new).astype(BF16)
        return m_new, acc * jnp.exp2(m - m_new) + _dot(vte(j, hh), p)

    def accumulate(j, slot, carry):
        return tuple(update(m, acc, s_ref[slot, hh], mb_ref[slot, hh], j, hh)
                     for hh, (m, acc) in enumerate(carry))

    def accumulate_diag(j, slot, carry, width):
        off = TQ - width
        out = []
        for hh, (m, acc) in enumerate(carry):
            st = s_ref[slot, hh, :, 0:width]
            masked = jnp.where(tri, st[:, :TK], NEG_BIG)
            st = masked if width == TK else jnp.concatenate([masked, st[:, TK:]], axis=1)
            m_w, acc_w = update(m[:, off:], acc[:, off:], st, jnp.max(st, axis=0, keepdims=True), j, hh)
            if off:
                m_w = jnp.concatenate([m[:, :off], m_w], axis=1)
                acc_w = jnp.concatenate([acc[:, :off], acc_w], axis=1)
            out.append((m_w, acc_w))
        return tuple(out)

    def tile(qi, _):
        qstart = qi * TQ

        def pair(t, carry):
            scores(qstart, 2 * t + 1, 1, TQ)
            carry = accumulate(2 * t, 0, carry)
            scores(qstart, 2 * t + 2, 0, TQ)
            return accumulate(2 * t + 1, 1, carry)

        init = tuple((jnp.full((1, TQ), NEG_BIG, F32), jnp.zeros((MLA_V_DIM + ONES_ROWS, TQ), F32))
                     for _ in heads)
        d0 = DIAG_BLOCKS * qi
        carry = lax.fori_loop(0, d0 // 2, pair, init)
        for d in range(DIAG_BLOCKS):
            if d + 1 < DIAG_BLOCKS:
                scores(qstart, d0 + d + 1, (d + 1) % 2, TQ - (d + 1) * TK)
            else:
                scores(jnp.minimum(qi + 1, nq - 1) * TQ, 0, (d + 1) % 2, TQ)
            carry = accumulate_diag(d0 + d, d % 2, carry, TQ - d * TK)
        for hh, (_, acc) in enumerate(carry):
            o_ref[0, qi, hh * MLA_V_DIM:(hh + 1) * MLA_V_DIM, :] = (
                acc[:MLA_V_DIM] / acc[MLA_V_DIM:MLA_V_DIM + 1]).astype(BF16)
        return 0

    scores(0, 0, 0, TQ)
    lax.fori_loop(0, nq, tile, 0)


def _mla_attention(q, k, vt):
    B, S, _ = q.shape
    nkb = S // TK
    nq = S // TQ
    hp = HEADS_PER_STEP
    return pl.pallas_call(
        _attn_body,
        grid=(B, MLA_HEADS // hp),
        in_specs=[pl.BlockSpec((1, S, hp * HEAD_SLAB), lambda b, g: (b, 0, g)),
                  pl.BlockSpec((1, S, hp * HEAD_SLAB), lambda b, g: (b, 0, g)),
                  pl.BlockSpec((1, nkb, hp * MLA_V_DIM, TK), lambda b, g: (b, 0, g, 0))],
        out_specs=pl.BlockSpec((1, nq, hp * MLA_V_DIM, TQ), lambda b, g: (b, 0, g, 0)),
        out_shape=jax.ShapeDtypeStruct((B, nq, MLA_WIDTH, TQ), BF16),
        scratch_shapes=[pltpu.VMEM((2, hp, TK, TQ), F32), pltpu.VMEM((2, hp, 1, TQ), F32)],
        compiler_params=pltpu.CompilerParams(dimension_semantics=("arbitrary", "arbitrary"),
                                             vmem_limit_bytes=VMEM_LIMIT),
        name="mla_attn",
    )(q, k, vt)


def _post_body(ret_ref, mla_ref, x_ref, p_ref, wo_ref, pmn_ref, pfn_ref, wg_ref, wu_ref, wd_ref,
               pofn_ref, wpp_ref, plen_ref, wpg_ref, bpg_ref, o_ref, act_ref):
    mix = _dot(ret_ref[0], wo_ref[0:RET_WIDTH, :]) + _dot_tn(mla_ref[0, 0], wo_ref[RET_WIDTH:, :])
    h1 = x_ref[0] + _rms(mix, pmn_ref[...])
    hn = _rms(h1, pfn_ref[...]).astype(BF16)
    for c in range(D_FF // FF_CHUNK):
        cols = slice(c * FF_CHUNK, (c + 1) * FF_CHUNK)
        g = _dot(hn, wg_ref[:, cols])
        u = _dot(hn, wu_ref[:, cols])
        act_ref[:, cols] = (g * jax.nn.sigmoid(g) * u).astype(BF16)
    ff = _dot(act_ref[...], wd_ref[...])
    h2 = h1 + _rms(ff, pofn_ref[...])
    e = _rms(_dot(p_ref[0].astype(BF16), wpp_ref[...]), plen_ref[...])
    gate = jax.nn.sigmoid(_dot(h2.astype(BF16), wpg_ref[...]) + bpg_ref[...])
    o_ref[0] = h2 + e * gate


def _post(ret_out, mla_out, x, p, w_o, post_mix_norm, pre_ffn_norm, w_gate, w_up, w_down,
          post_ffn_norm, w_ple_proj, ple_norm, w_ple_gate, b_ple_gate):
    B, S, D = x.shape
    tm = TM_POST
    const = lambda shape: pl.BlockSpec(shape, lambda b, i: (0,) * len(shape),
                                       pipeline_mode=pl.Buffered(1))
    tile = lambda w: pl.BlockSpec((1, tm, w), lambda b, i: (b, i, 0))
    return pl.pallas_call(
        _post_body,
        grid=(B, S // tm),
        in_specs=[tile(RET_WIDTH),
                  pl.BlockSpec((1, 1, MLA_WIDTH, tm), lambda b, i: (b, i // (TQ // tm), 0, i % (TQ // tm))),
                  tile(D), tile(PLE_DIM),
                  const((D, D)), const((1, D)), const((1, D)),
                  const((D, D_FF)), const((D, D_FF)), const((D_FF, D)), const((1, D)),
                  const((PLE_DIM, D)), const((1, D)), const((D, D)), const((1, D))],
        out_specs=tile(D),
        out_shape=jax.ShapeDtypeStruct((B, S, D), F32),
        scratch_shapes=[pltpu.VMEM((tm, D_FF), BF16)],
        compiler_params=pltpu.CompilerParams(dimension_semantics=("arbitrary", "arbitrary"),
                                             vmem_limit_bytes=VMEM_LIMIT),
        name="post",
    )(ret_out, mla_out, x, p, w_o, post_mix_norm, pre_ffn_norm, w_gate, w_up, w_down,
      post_ffn_norm, w_ple_proj, ple_norm, w_ple_gate, b_ple_gate)


def _prep_w_in(w_in):
    base = 4 * RET_WIDTH + MLA_Q_LORA + MLA_KV_LORA
    kr = w_in[:, base:base + MLA_ROPE_DIM]
    z = jnp.zeros((w_in.shape[0], MLA_NOPE_DIM), w_in.dtype)
    zt = jnp.zeros((w_in.shape[0], LANES - MLA_QK_DIM), w_in.dtype)
    return jnp.concatenate([w_in[:, :base], z, kr, zt], axis=1).astype(BF16)


def _pad_heads(w, per_head, keep):
    K = w.shape[0]
    w = w.reshape(K, MLA_HEADS, per_head)[:, :, :keep]
    w = jnp.pad(w, ((0, 0), (0, 0), (0, HEAD_SLAB - keep)))
    return w.reshape(K, MLA_HEADS * HEAD_SLAB).astype(BF16)


def kernel(x, p, positions, pre_mix_norm, w_in, ret_gn_w, mla_q_norm, w_uq, mla_kv_norm, w_ukv, w_o,
           post_mix_norm, pre_ffn_norm, w_gate, w_up, w_down, post_ffn_norm, w_ple_proj, ple_norm,
           w_ple_gate, b_ple_gate):
    depth = pre_mix_norm.shape[0]
    tables = _rope_tables(positions)
    h = x
    for i in range(depth):
        wuq_p = _pad_heads(w_uq[i], MLA_QK_DIM, MLA_QK_DIM)
        wuk_p = _pad_heads(w_ukv[i], MLA_NOPE_DIM + MLA_V_DIM, MLA_NOPE_DIM)
        wv = w_ukv[i].reshape(MLA_KV_LORA, MLA_HEADS, MLA_NOPE_DIM + MLA_V_DIM)[:, :, MLA_NOPE_DIM:]
        wvt = wv.reshape(MLA_KV_LORA, MLA_WIDTH).T.astype(BF16)
        rq, rk, rv, rg, q, k, vt = _in_proj(h, pre_mix_norm[i][None], _prep_w_in(w_in[i]),
                                            mla_q_norm[i][None], wuq_p, mla_kv_norm[i][None], wuk_p, wvt,
                                            tables)
        ret_out = _retention(rq, rk, rv, rg, ret_gn_w[i][None])
        mla_out = _mla_attention(q, k, vt)
        h = _post(ret_out, mla_out, h, p[i], w_o[i].astype(BF16), post_mix_norm[i][None],
                  pre_ffn_norm[i][None], w_gate[i].astype(BF16), w_up[i].astype(BF16),
                  w_down[i].astype(BF16), post_ffn_norm[i][None], w_ple_proj[i].astype(BF16),
                  ple_norm[i][None], w_ple_gate[i].astype(BF16), b_ple_gate[i][None])
    return h
```

```python
import math
from functools import partial

import jax
import jax.numpy as jnp
from jax import lax
from jax.experimental import pallas as pl
from jax.experimental.pallas import tpu as pltpu

D_MODEL = 1024
PLE_DIM = 256
RET_HEADS = 4
RET_HEAD_DIM = 128
RET_WIDTH = RET_HEADS * RET_HEAD_DIM
MLA_HEADS = 8
MLA_NOPE_DIM = 64
MLA_ROPE_DIM = 32
MLA_QK_DIM = MLA_NOPE_DIM + MLA_ROPE_DIM
MLA_V_DIM = 64
MLA_WIDTH = MLA_HEADS * MLA_V_DIM
MLA_Q_LORA = 384
MLA_KV_LORA = 256
D_FF = 2816
ROPE_BASE = 10000.0
EPS = 1e-6
NEG_BIG = -1e30

LANES = 128
HEAD_SLAB = 128
IN_COLS_PAD = 4 * RET_WIDTH + MLA_Q_LORA + MLA_KV_LORA + LANES
VMEM_LIMIT = 56 * 1024 * 1024

TM_PROJ = 512
RET_CHUNK = 128
TM_RET = 512
TQ = 2048
TK = 512
ONES_ROWS = 16
TM_POST = 512
FF_CHUNK = 256

BF16 = jnp.bfloat16
F32 = jnp.float32


def _dot(a, b):
    return jnp.dot(a, b, preferred_element_type=F32)


def _dot_nt(a, b):
    return lax.dot_general(a, b, (((1,), (1,)), ((), ())), preferred_element_type=F32)


def _dot_tn(a, b):
    return lax.dot_general(a, b, (((0,), (0,)), ((), ())), preferred_element_type=F32)


def _rms(v, w):
    return v * lax.rsqrt(jnp.mean(v * v, axis=-1, keepdims=True) + EPS) * w


def _rope_tables_body(pos_ref, invr_ref, invm_ref, sgnr_ref, sgnm_ref, onesm_ref,
                      cr_ref, sr_ref, cm_ref, sm_ref):
    pos = pos_ref[0].astype(F32)
    ang_r = pos * invr_ref[...]
    cr_ref[0] = jnp.cos(ang_r)
    sr_ref[0] = jnp.sin(ang_r) * sgnr_ref[...]
    ang_m = pos * invm_ref[...]
    cm_ref[0] = jnp.cos(ang_m) * onesm_ref[...]
    sm_ref[0] = jnp.sin(ang_m) * sgnm_ref[...]


def _rope_tables(positions):
    B, S = positions.shape
    tm = 1024
    half_r = RET_HEAD_DIM // 2
    inv_r = 1.0 / (ROPE_BASE ** (jnp.arange(half_r, dtype=F32) / half_r))
    inv_r = jnp.concatenate([inv_r, inv_r])[None, :]
    sgn_r = jnp.concatenate([-jnp.ones((half_r,), F32), jnp.ones((half_r,), F32)])[None, :]
    half_m = MLA_ROPE_DIM // 2
    inv_m = 1.0 / (ROPE_BASE ** (jnp.arange(half_m, dtype=F32) / half_m))
    z64 = jnp.zeros((MLA_NOPE_DIM,), F32)
    z32 = jnp.zeros((HEAD_SLAB - MLA_QK_DIM,), F32)
    inv_m = jnp.concatenate([z64, inv_m, inv_m, z32])[None, :]
    sgn_m = jnp.concatenate([z64, -jnp.ones((half_m,), F32), jnp.ones((half_m,), F32), z32])[None, :]
    ones_m = jnp.concatenate([jnp.ones((MLA_QK_DIM,), F32), z32])[None, :]
    pos3 = positions.reshape(B, S, 1)
    vec = pl.BlockSpec((1, LANES), lambda b, i: (0, 0))
    tab = pl.BlockSpec((1, tm, LANES), lambda b, i: (b, i, 0))
    out = jax.ShapeDtypeStruct((B, S, LANES), F32)
    return pl.pallas_call(
        _rope_tables_body,
        grid=(B, S // tm),
        in_specs=[pl.BlockSpec((1, tm, 1), lambda b, i: (b, i, 0)), vec, vec, vec, vec, vec],
        out_specs=[tab, tab, tab, tab],
        out_shape=[out, out, out, out],
        compiler_params=pltpu.CompilerParams(dimension_semantics=("arbitrary", "arbitrary"),
                                             vmem_limit_bytes=VMEM_LIMIT),
        name="rope_tables",
    )(pos3, inv_r, inv_m, sgn_r, sgn_m, ones_m)


def _in_proj_body(x_ref, g_ref, win_ref, qn_ref, wuq_ref, kvn_ref, wuk_ref, wvt_ref,
                  cr_ref, sr_ref, cm_ref, sm_ref,
                  rq_ref, rk_ref, rv_ref, rg_ref, q_ref, k_ref, vt_ref, *, q_scale):
    xn = _rms(x_ref[0], g_ref[...]).astype(BF16)
    cr = cr_ref[0]
    sr = sr_ref[0]

    def ret_rope(v):
        return v * cr + pltpu.roll(v, RET_HEAD_DIM // 2, 1) * sr

    pq = _dot(xn, win_ref[:, 0:RET_WIDTH])
    pk = _dot(xn, win_ref[:, RET_WIDTH:2 * RET_WIDTH])
    for h in range(RET_HEADS):
        sl = slice(h * RET_HEAD_DIM, (h + 1) * RET_HEAD_DIM)
        rq_ref[0, :, sl] = ret_rope(pq[:, sl]).astype(BF16)
        rk_ref[0, :, sl] = (ret_rope(pk[:, sl]) * (RET_HEAD_DIM ** -0.5)).astype(BF16)
    rv_ref[0] = _dot(xn, win_ref[:, 2 * RET_WIDTH:3 * RET_WIDTH]).astype(BF16)
    rg_ref[0] = _dot(xn, win_ref[:, 3 * RET_WIDTH:4 * RET_WIDTH]).astype(BF16)

    cm = cm_ref[0]
    sm = sm_ref[0]
    lane = lax.broadcasted_iota(jnp.int32, cm.shape, 1)
    upper = lane >= (MLA_NOPE_DIM + MLA_ROPE_DIM // 2)

    def mla_rope(v):
        swapped = jnp.where(upper, pltpu.roll(v, MLA_ROPE_DIM // 2, 1),
                            pltpu.roll(v, LANES - MLA_ROPE_DIM // 2, 1))
        return v * cm + swapped * sm

    o = 4 * RET_WIDTH
    cq = _dot(xn, win_ref[:, o:o + MLA_Q_LORA])
    cqn = _rms(cq, qn_ref[...]).astype(BF16)
    qh = _dot(cqn, wuq_ref[...])
    for h in range(MLA_HEADS):
        sl = slice(h * HEAD_SLAB, (h + 1) * HEAD_SLAB)
        q_ref[0, :, sl] = (mla_rope(qh[:, sl]) * q_scale).astype(BF16)

    o += MLA_Q_LORA
    ckv = _dot(xn, win_ref[:, o:o + MLA_KV_LORA])
    ckvn = _rms(ckv, kvn_ref[...]).astype(BF16)
    o += MLA_KV_LORA
    kr = mla_rope(_dot(xn, win_ref[:, o:o + LANES]))
    kn = _dot(ckvn, wuk_ref[...])
    for h in range(MLA_HEADS):
        sl = slice(h * HEAD_SLAB, (h + 1) * HEAD_SLAB)
        k_ref[0, :, sl] = (kn[:, sl] + kr).astype(BF16)
    vt = _dot_nt(wvt_ref[...], ckvn).astype(BF16)
    for blk in range(TM_PROJ // TK):
        vt_ref[0, blk] = vt[:, blk * TK:(blk + 1) * TK]


def _in_proj(x, pre_mix_norm, w_in_p, q_norm, wuq_p, kv_norm, wuk_p, wvt, tables):
    B, S, D = x.shape
    tm = TM_PROJ
    cr, sr, cm, sm = tables
    q_scale = (1.0 / math.sqrt(MLA_QK_DIM)) * math.log2(math.e)
    const = lambda shape: pl.BlockSpec(shape, lambda b, i: (0,) * len(shape))
    tile = lambda w: pl.BlockSpec((1, tm, w), lambda b, i: (b, i, 0))
    bf = lambda w: jax.ShapeDtypeStruct((B, S, w), BF16)
    return pl.pallas_call(
        partial(_in_proj_body, q_scale=q_scale),
        grid=(B, S // tm),
        in_specs=[tile(D), const((1, D)), const((D, IN_COLS_PAD)),
                  const((1, MLA_Q_LORA)), const((MLA_Q_LORA, MLA_HEADS * HEAD_SLAB)),
                  const((1, MLA_KV_LORA)), const((MLA_KV_LORA, MLA_HEADS * HEAD_SLAB)),
                  const((MLA_WIDTH, MLA_KV_LORA)),
                  tile(LANES), tile(LANES), tile(LANES), tile(LANES)],
        out_specs=[tile(RET_WIDTH), tile(RET_WIDTH), tile(RET_WIDTH), tile(RET_WIDTH),
                   tile(MLA_HEADS * HEAD_SLAB), tile(MLA_HEADS * HEAD_SLAB),
                   pl.BlockSpec((1, tm // TK, MLA_WIDTH, TK), lambda b, i: (b, i, 0, 0))],
        out_shape=[bf(RET_WIDTH), bf(RET_WIDTH), bf(RET_WIDTH), bf(RET_WIDTH),
                   bf(MLA_HEADS * HEAD_SLAB), bf(MLA_HEADS * HEAD_SLAB),
                   jax.ShapeDtypeStruct((B, S // TK, MLA_WIDTH, TK), BF16)],
        compiler_params=pltpu.CompilerParams(dimension_semantics=("arbitrary", "arbitrary"),
                                             vmem_limit_bytes=VMEM_LIMIT),
        name="in_proj",
    )(x, pre_mix_norm, w_in_p, q_norm, wuq_p, kv_norm, wuk_p, wvt, cr, sr, cm, sm)


def _retention_body(rq_ref, rk_ref, rv_ref, rg_ref, dmask_ref, zeta_ref, xi_ref, gnw_ref,
                    o_ref, state_ref, *, g_chunk):
    @pl.when(pl.program_id(1) == 0)
    def _():
        state_ref[...] = jnp.zeros_like(state_ref)

    C = RET_CHUNK
    for c in range(TM_RET // C):
        rows = slice(c * C, (c + 1) * C)
        for h in range(RET_HEADS):
            cols = slice(h * RET_HEAD_DIM, (h + 1) * RET_HEAD_DIM)
            q = rq_ref[0, rows, cols]
            k = rk_ref[0, rows, cols]
            v = rv_ref[0, rows, cols]
            scores = _dot_nt(q, k) * dmask_ref[h]
            inner = _dot(scores.astype(BF16), v)
            r_prev = state_ref[h]
            cross = _dot(q, r_prev.astype(BF16)) * xi_ref[h]
            kz = (k.astype(F32) * zeta_ref[h]).astype(BF16)
            state_ref[h] = g_chunk[h] * r_prev + _dot_tn(kz, v)
            y = inner + cross
            mu = jnp.mean(y, axis=-1, keepdims=True)
            yc = y - mu
            var = jnp.mean(yc * yc, axis=-1, keepdims=True)
            yn = yc * lax.rsqrt(var + EPS) * gnw_ref[:, cols]
            gate = rg_ref[0, rows, cols].astype(F32)
            o_ref[0, rows, cols] = (gate * jax.nn.sigmoid(gate) * yn).astype(BF16)


def _retention(rq, rk, rv, rg, ret_gn_w):
    B, S, W = rq.shape
    C = RET_CHUNK
    H = RET_HEADS
    log_g = jnp.log(1.0 - 2.0 ** (-5.0 - jnp.arange(H, dtype=F32)))
    j = jnp.arange(C, dtype=F32)
    diff = j[:, None] - j[None, :]
    dmask = jnp.where(diff[None] >= 0, jnp.exp(jnp.maximum(diff, 0.0)[None] * log_g[:, None, None]), 0.0)
    zeta = jnp.exp((C - 1 - j)[None, :] * log_g[:, None])
    xi = jnp.exp((j + 1)[None, :] * log_g[:, None])
    zeta_b = jnp.broadcast_to(zeta[:, :, None], (H, C, RET_HEAD_DIM))
    xi_b = jnp.broadcast_to(xi[:, :, None], (H, C, RET_HEAD_DIM))
    g_chunk = tuple(float((1.0 - 2.0 ** (-5.0 - h)) ** C) for h in range(H))
    tile = pl.BlockSpec((1, TM_RET, W), lambda b, i: (b, i, 0))
    const3 = pl.BlockSpec((H, C, C), lambda b, i: (0, 0, 0))
    return pl.pallas_call(
        partial(_retention_body, g_chunk=g_chunk),
        grid=(B, S // TM_RET),
        in_specs=[tile, tile, tile, tile, const3, const3, const3,
                  pl.BlockSpec((1, W), lambda b, i: (0, 0))],
        out_specs=tile,
        out_shape=jax.ShapeDtypeStruct((B, S, W), BF16),
        scratch_shapes=[pltpu.VMEM((H, RET_HEAD_DIM, RET_HEAD_DIM), F32)],
        compiler_params=pltpu.CompilerParams(dimension_semantics=("arbitrary", "arbitrary"),
                                             vmem_limit_bytes=VMEM_LIMIT),
        name="retention",
    )(rq, rk, rv, rg, dmask, zeta_b, xi_b, ret_gn_w)


HEADS_PER_STEP = 2
PIECES = TQ // TK


def _attn_body(q_ref, k_ref, vt_ref, o_ref, *scratch):
    n = 2 * HEADS_PER_STEP
    s_ref = {(i // HEADS_PER_STEP, i % HEADS_PER_STEP): scratch[i] for i in range(n)}
    mb_ref = {(i // HEADS_PER_STEP, i % HEADS_PER_STEP): scratch[n + i] for i in range(n)}
    nq = q_ref.shape[1] // TQ
    ones = jnp.ones((ONES_ROWS, TK), BF16)
    chains = [(hh, c) for c in range(PIECES) for hh in range(HEADS_PER_STEP)]
    row = lax.broadcasted_iota(jnp.int32, (TK, TK), 0)
    col = lax.broadcasted_iota(jnp.int32, (TK, TK), 1)
    tri = row <= col

    def score_piece(qstart, j, slot, hh, c):
        kstart = pl.multiple_of(j * TK, TK)
        q0 = pl.multiple_of(qstart + c * TK, TK)
        cols = slice(hh * HEAD_SLAB, (hh + 1) * HEAD_SLAB)
        lanes = slice(c * TK, (c + 1) * TK)
        st = _dot_nt(k_ref[0, pl.ds(kstart, TK), cols], q_ref[0, pl.ds(q0, TK), cols])
        s_ref[slot, hh][:, lanes] = st
        mb_ref[slot, hh][:, lanes] = jnp.max(st, axis=0, keepdims=True)

    def acc_piece(j, slot, hh, c, state, masked):
        m, acc = state
        lanes = slice(c * TK, (c + 1) * TK)
        st = s_ref[slot, hh][:, lanes]
        if masked:
            st = jnp.where(tri, st, NEG_BIG)
            blk_max = jnp.max(st, axis=0, keepdims=True)
        else:
            blk_max = mb_ref[slot, hh][:, lanes]
        m_new = jnp.maximum(m, blk_max)
        p = jnp.exp2(st - m_new).astype(BF16)
        vte = jnp.concatenate([vt_ref[0, j, hh * MLA_V_DIM:(hh + 1) * MLA_V_DIM, :], ones], axis=0)
        return m_new, acc * jnp.exp2(m - m_new) + _dot(vte, p)

    def stage(carry, j, slot, first, diag, nxt_qstart, nxt_j, nxt_first):
        carry = dict(carry)
        for hh, c in chains:
            if c >= nxt_first:
                score_piece(nxt_qstart, nxt_j, 1 - slot, hh, c)
            if c >= first:
                carry[hh, c] = acc_piece(j, slot, hh, c, carry[hh, c], diag and c == first)
        return carry

    def tile(qi, _):
        qstart = qi * TQ

        def pair(t, flat):
            carry = dict(zip(chains, flat))
            carry = stage(carry, 2 * t, 0, 0, False, qstart, 2 * t + 1, 0)
            carry = stage(carry, 2 * t + 1, 1, 0, False, qstart, 2 * t + 2, 0)
            return tuple(carry[ch] for ch in chains)

        init = tuple((jnp.full((1, TK), NEG_BIG, F32), jnp.zeros((MLA_V_DIM + ONES_ROWS, TK), F32))
                     for _ in chains)
        d0 = PIECES * qi
        carry = dict(zip(chains, lax.fori_loop(0, d0 // 2, pair, init)))
        for d in range(PIECES):
            if d + 1 < PIECES:
                carry = stage(carry, d0 + d, d % 2, d, True, qstart, d0 + d + 1, d + 1)
            else:
                carry = stage(carry, d0 + d, d % 2, d, True, jnp.minimum(qi + 1, nq - 1) * TQ, 0, 0)
        for hh, c in chains:
            acc = carry[hh, c][1]
            o_ref[0, qi, hh * MLA_V_DIM:(hh + 1) * MLA_V_DIM, c * TK:(c + 1) * TK] = (
                acc[:MLA_V_DIM] / acc[MLA_V_DIM:MLA_V_DIM + 1]).astype(BF16)
        return 0

    for hh, c in chains:
        score_piece(0, 0, 0, hh, c)
    lax.fori_loop(0, nq, tile, 0)


def _mla_attention(q, k, vt):
    B, S, _ = q.shape
    nkb = S // TK
    nq = S // TQ
    hp = HEADS_PER_STEP
    return pl.pallas_call(
        _attn_body,
        grid=(B, MLA_HEADS // hp),
        in_specs=[pl.BlockSpec((1, S, hp * HEAD_SLAB), lambda b, g: (b, 0, g)),
                  pl.BlockSpec((1, S, hp * HEAD_SLAB), lambda b, g: (b, 0, g)),
                  pl.BlockSpec((1, nkb, hp * MLA_V_DIM, TK), lambda b, g: (b, 0, g, 0))],
        out_specs=pl.BlockSpec((1, nq, hp * MLA_V_DIM, TQ), lambda b, g: (b, 0, g, 0)),
        out_shape=jax.ShapeDtypeStruct((B, nq, MLA_WIDTH, TQ), BF16),
        scratch_shapes=[pltpu.VMEM((TK, TQ), F32)] * (2 * hp) + [pltpu.VMEM((1, TQ), F32)] * (2 * hp),
        compiler_params=pltpu.CompilerParams(dimension_semantics=("arbitrary", "arbitrary"),
                                             vmem_limit_bytes=VMEM_LIMIT),
        name="mla_attn",
    )(q, k, vt)


def _post_body(ret_ref, mla_ref, x_ref, p_ref, wo_ref, pmn_ref, pfn_ref, wg_ref, wu_ref, wd_ref,
               pofn_ref, wpp_ref, plen_ref, wpg_ref, bpg_ref, o_ref, act_ref):
    mix = _dot(ret_ref[0], wo_ref[0:RET_WIDTH, :]) + _dot_tn(mla_ref[0, 0], wo_ref[RET_WIDTH:, :])
    h1 = x_ref[0] + _rms(mix, pmn_ref[...])
    hn = _rms(h1, pfn_ref[...]).astype(BF16)
    for c in range(D_FF // FF_CHUNK):
        cols = slice(c * FF_CHUNK, (c + 1) * FF_CHUNK)
        g = _dot(hn, wg_ref[:, cols])
        u = _dot(hn, wu_ref[:, cols])
        act_ref[:, cols] = (g * jax.nn.sigmoid(g) * u).astype(BF16)
    ff = _dot(act_ref[...], wd_ref[...])
    h2 = h1 + _rms(ff, pofn_ref[...])
    e = _rms(_dot(p_ref[0].astype(BF16), wpp_ref[...]), plen_ref[...])
    gate = jax.nn.sigmoid(_dot(h2.astype(BF16), wpg_ref[...]) + bpg_ref[...])
    o_ref[0] = h2 + e * gate


def _post(ret_out, mla_out, x, p, w_o, post_mix_norm, pre_ffn_norm, w_gate, w_up, w_down,
          post_ffn_norm, w_ple_proj, ple_norm, w_ple_gate, b_ple_gate):
    B, S, D = x.shape
    tm = TM_POST
    const = lambda shape: pl.BlockSpec(shape, lambda b, i: (0,) * len(shape),
                                       pipeline_mode=pl.Buffered(1))
    tile = lambda w: pl.BlockSpec((1, tm, w), lambda b, i: (b, i, 0))
    return pl.pallas_call(
        _post_body,
        grid=(B, S // tm),
        in_specs=[tile(RET_WIDTH),
                  pl.BlockSpec((1, 1, MLA_WIDTH, tm), lambda b, i: (b, i // (TQ // tm), 0, i % (TQ // tm))),
                  tile(D), tile(PLE_DIM),
                  const((D, D)), const((1, D)), const((1, D)),
                  const((D, D_FF)), const((D, D_FF)), const((D_FF, D)), const((1, D)),
                  const((PLE_DIM, D)), const((1, D)), const((D, D)), const((1, D))],
        out_specs=tile(D),
        out_shape=jax.ShapeDtypeStruct((B, S, D), F32),
        scratch_shapes=[pltpu.VMEM((tm, D_FF), BF16)],
        compiler_params=pltpu.CompilerParams(dimension_semantics=("arbitrary", "arbitrary"),
                                             vmem_limit_bytes=VMEM_LIMIT),
        name="post",
    )(ret_out, mla_out, x, p, w_o, post_mix_norm, pre_ffn_norm, w_gate, w_up, w_down,
      post_ffn_norm, w_ple_proj, ple_norm, w_ple_gate, b_ple_gate)


def _prep_w_in(w_in):
    base = 4 * RET_WIDTH + MLA_Q_LORA + MLA_KV_LORA
    kr = w_in[:, base:base + MLA_ROPE_DIM]
    z = jnp.zeros((w_in.shape[0], MLA_NOPE_DIM), w_in.dtype)
    zt = jnp.zeros((w_in.shape[0], LANES - MLA_QK_DIM), w_in.dtype)
    return jnp.concatenate([w_in[:, :base], z, kr, zt], axis=1).astype(BF16)


def _pad_heads(w, per_head, keep):
    K = w.shape[0]
    w = w.reshape(K, MLA_HEADS, per_head)[:, :, :keep]
    w = jnp.pad(w, ((0, 0), (0, 0), (0, HEAD_SLAB - keep)))
    return w.reshape(K, MLA_HEADS * HEAD_SLAB).astype(BF16)


def kernel(x, p, positions, pre_mix_norm, w_in, ret_gn_w, mla_q_norm, w_uq, mla_kv_norm, w_ukv, w_o,
           post_mix_norm, pre_ffn_norm, w_gate, w_up, w_down, post_ffn_norm, w_ple_proj, ple_norm,
           w_ple_gate, b_ple_gate):
    depth = pre_mix_norm.shape[0]
    tables = _rope_tables(positions)
    h = x
    for i in range(depth):
        wuq_p = _pad_heads(w_uq[i], MLA_QK_DIM, MLA_QK_DIM)
        wuk_p = _pad_heads(w_ukv[i], MLA_NOPE_DIM + MLA_V_DIM, MLA_NOPE_DIM)
        wv = w_ukv[i].reshape(MLA_KV_LORA, MLA_HEADS, MLA_NOPE_DIM + MLA_V_DIM)[:, :, MLA_NOPE_DIM:]
        wvt = wv.reshape(MLA_KV_LORA, MLA_WIDTH).T.astype(BF16)
        rq, rk, rv, rg, q, k, vt = _in_proj(h, pre_mix_norm[i][None], _prep_w_in(w_in[i]),
                                            mla_q_norm[i][None], wuq_p, mla_kv_norm[i][None], wuk_p, wvt,
                                            tables)
        ret_out = _retention(rq, rk, rv, rg, ret_gn_w[i][None])
        mla_out = _mla_attention(q, k, vt)
        h = _post(ret_out, mla_out, h, p[i], w_o[i].astype(BF16), post_mix_norm[i][None],
                  pre_ffn_norm[i][None], w_gate[i].astype(BF16), w_up[i].astype(BF16),
                  w_down[i].astype(BF16), post_ffn_norm[i][None], w_ple_proj[i].astype(BF16),
                  ple_norm[i][None], w_ple_gate[i].astype(BF16), b_ple_gate[i][None])
    return h
```

```python
import math
from functools import partial

import jax
import jax.numpy as jnp
from jax import lax
from jax.experimental import pallas as pl
from jax.experimental.pallas import tpu as pltpu

D_MODEL = 1024
PLE_DIM = 256
RET_HEADS = 4
RET_HEAD_DIM = 128
RET_WIDTH = RET_HEADS * RET_HEAD_DIM
MLA_HEADS = 8
MLA_NOPE_DIM = 64
MLA_ROPE_DIM = 32
MLA_QK_DIM = MLA_NOPE_DIM + MLA_ROPE_DIM
MLA_V_DIM = 64
MLA_WIDTH = MLA_HEADS * MLA_V_DIM
MLA_Q_LORA = 384
MLA_KV_LORA = 256
D_FF = 2816
ROPE_BASE = 10000.0
EPS = 1e-6
NEG_BIG = -1e30

LANES = 128
HEAD_SLAB = 128
IN_COLS_PAD = 4 * RET_WIDTH + MLA_Q_LORA + MLA_KV_LORA + LANES
VMEM_LIMIT = 56 * 1024 * 1024

TM_PROJ = 512
RET_CHUNK = 128
TQ = 2048
TK = 512
ONES_ROWS = 16
TM_POST = 512
FF_CHUNK = 256

BF16 = jnp.bfloat16
F32 = jnp.float32


def _dot(a, b):
    return jnp.dot(a, b, preferred_element_type=F32)


def _dot_nt(a, b):
    return lax.dot_general(a, b, (((1,), (1,)), ((), ())), preferred_element_type=F32)


def _dot_tn(a, b):
    return lax.dot_general(a, b, (((0,), (0,)), ((), ())), preferred_element_type=F32)


def _rms(v, w):
    return v * lax.rsqrt(jnp.mean(v * v, axis=-1, keepdims=True) + EPS) * w


TM_ROPE = 1024


def _rope_tables_body(pos_ref, invr_ref, invm_ref, cr_ref, sr_ref, cm_ref, sm_ref):
    pos = pos_ref[0].astype(F32)
    ang_r = invr_ref[...] * pos
    c = jnp.cos(ang_r)
    s = jnp.sin(ang_r)
    ang_m = invm_ref[...] * pos
    c16 = jnp.cos(ang_m)
    s16 = jnp.sin(ang_m)
    tm = pos.shape[1]
    pad = HEAD_SLAB - MLA_QK_DIM
    cr_ref[0] = jnp.concatenate([c, c], axis=0).T
    sr_ref[0] = jnp.concatenate([-s, s], axis=0).T
    cm_ref[0] = jnp.concatenate([jnp.ones((MLA_NOPE_DIM, tm), F32), c16, c16, jnp.zeros((pad, tm), F32)], axis=0).T
    sm_ref[0] = jnp.concatenate([jnp.zeros((MLA_NOPE_DIM, tm), F32), -s16, s16, jnp.zeros((pad, tm), F32)], axis=0).T


def _rope_tables(positions):
    B, S = positions.shape
    tm = TM_ROPE
    half_r = RET_HEAD_DIM // 2
    inv_r = 1.0 / (ROPE_BASE ** (jnp.arange(half_r, dtype=F32) / half_r))
    half_m = MLA_ROPE_DIM // 2
    inv_m = 1.0 / (ROPE_BASE ** (jnp.arange(half_m, dtype=F32) / half_m))
    inv_r = jnp.broadcast_to(inv_r[:, None], (half_r, tm))
    inv_m = jnp.broadcast_to(inv_m[:, None], (half_m, tm))
    tab = pl.BlockSpec((1, tm, LANES), lambda b, i: (b, i, 0))
    out = jax.ShapeDtypeStruct((B, S, LANES), F32)
    return pl.pallas_call(
        _rope_tables_body,
        grid=(B, S // tm),
        in_specs=[pl.BlockSpec((1, 1, tm), lambda b, i: (b, 0, i)),
                  pl.BlockSpec((half_r, tm), lambda b, i: (0, 0)),
                  pl.BlockSpec((half_m, tm), lambda b, i: (0, 0))],
        out_specs=[tab, tab, tab, tab],
        out_shape=[out, out, out, out],
        compiler_params=pltpu.CompilerParams(dimension_semantics=("arbitrary", "arbitrary"),
                                             vmem_limit_bytes=VMEM_LIMIT),
        name="rope_tables",
    )(positions.reshape(B, 1, S), inv_r, inv_m)


def _in_proj_body(x_ref, g_ref, win_ref, qn_ref, wuq_ref, kvn_ref, wuk_ref, wvt_ref,
                  cr_ref, sr_ref, cm_ref, sm_ref, dmask_ref, zeta_ref, xi_ref, gnw_ref,
                  ret_ref, q_ref, k_ref, vt_ref, state_ref, *, q_scale, g_chunk):
    @pl.when(pl.program_id(1) == 0)
    def _():
        state_ref[...] = jnp.zeros_like(state_ref)

    xn = _rms(x_ref[0], g_ref[...]).astype(BF16)
    cr = cr_ref[0]
    sr = sr_ref[0]

    def ret_rope(v):
        return v * cr + pltpu.roll(v, RET_HEAD_DIM // 2, 1) * sr

    pq = _dot(xn, win_ref[:, 0:RET_WIDTH])
    pk = _dot(xn, win_ref[:, RET_WIDTH:2 * RET_WIDTH])
    pv = _dot(xn, win_ref[:, 2 * RET_WIDTH:3 * RET_WIDTH]).astype(BF16)
    pg = _dot(xn, win_ref[:, 3 * RET_WIDTH:4 * RET_WIDTH])

    C = RET_CHUNK
    head_cols = [slice(h * RET_HEAD_DIM, (h + 1) * RET_HEAD_DIM) for h in range(RET_HEADS)]
    rq = [ret_rope(pq[:, cols]).astype(BF16) for cols in head_cols]
    rk = [(ret_rope(pk[:, cols]) * (RET_HEAD_DIM ** -0.5)).astype(BF16) for cols in head_cols]

    def ret_chunk(c):
        rows = slice(c * C, (c + 1) * C)
        for h, cols in enumerate(head_cols):
            q = rq[h][rows]
            k = rk[h][rows]
            v = pv[rows, cols]
            scores = _dot_nt(q, k) * dmask_ref[h]
            inner = _dot(scores.astype(BF16), v)
            r_prev = state_ref[h]
            cross = _dot(q, r_prev.astype(BF16)) * xi_ref[h]
            kz = (k.astype(F32) * zeta_ref[h]).astype(BF16)
            state_ref[h] = g_chunk[h] * r_prev + _dot_tn(kz, v)
            y = inner + cross
            mu = jnp.mean(y, axis=-1, keepdims=True)
            yc = y - mu
            var = jnp.mean(yc * yc, axis=-1, keepdims=True)
            yn = yc * lax.rsqrt(var + EPS) * gnw_ref[:, cols]
            gate = pg[rows, cols]
            ret_ref[0, rows, cols] = (gate * jax.nn.sigmoid(gate) * yn).astype(BF16)

    cm = cm_ref[0]
    sm = sm_ref[0]
    lane = lax.broadcasted_iota(jnp.int32, cm.shape, 1)
    upper = lane >= (MLA_NOPE_DIM + MLA_ROPE_DIM // 2)

    def mla_rope(v):
        swapped = jnp.where(upper, pltpu.roll(v, MLA_ROPE_DIM // 2, 1),
                            pltpu.roll(v, LANES - MLA_ROPE_DIM // 2, 1))
        return v * cm + swapped * sm

    for c in range(TM_PROJ // C):
        ret_chunk(c)
    o = 4 * RET_WIDTH
    cq = _dot(xn, win_ref[:, o:o + MLA_Q_LORA])
    cqn = _rms(cq, qn_ref[...]).astype(BF16)
    qh = _dot(cqn, wuq_ref[...])
    for h in range(MLA_HEADS):
        sl = slice(h * HEAD_SLAB, (h + 1) * HEAD_SLAB)
        q_ref[0, :, sl] = (mla_rope(qh[:, sl]) * q_scale).astype(BF16)

    o += MLA_Q_LORA
    ckv = _dot(xn, win_ref[:, o:o + MLA_KV_LORA])
    ckvn = _rms(ckv, kvn_ref[...]).astype(BF16)
    o += MLA_KV_LORA
    kr = mla_rope(_dot(xn, win_ref[:, o:o + LANES]))
    kn = _dot(ckvn, wuk_ref[...])
    for h in range(MLA_HEADS):
        sl = slice(h * HEAD_SLAB, (h + 1) * HEAD_SLAB)
        k_ref[0, :, sl] = (kn[:, sl] + kr).astype(BF16)
    vt = _dot_nt(wvt_ref[...], ckvn).astype(BF16)
    for blk in range(TM_PROJ // TK):
        vt_ref[0, blk] = vt[:, blk * TK:(blk + 1) * TK]


def _in_proj(x, pre_mix_norm, w_in_p, q_norm, wuq_p, kv_norm, wuk_p, wvt, tables, ret_gn_w):
    B, S, D = x.shape
    tm = TM_PROJ
    cr, sr, cm, sm = tables
    q_scale = (1.0 / math.sqrt(MLA_QK_DIM)) * math.log2(math.e)
    C = RET_CHUNK
    H = RET_HEADS
    log_g = jnp.log(1.0 - 2.0 ** (-5.0 - jnp.arange(H, dtype=F32)))
    j = jnp.arange(C, dtype=F32)
    diff = j[:, None] - j[None, :]
    dmask = jnp.where(diff[None] >= 0, jnp.exp(jnp.maximum(diff, 0.0)[None] * log_g[:, None, None]), 0.0)
    zeta = jnp.exp((C - 1 - j)[None, :] * log_g[:, None])
    xi = jnp.exp((j + 1)[None, :] * log_g[:, None])
    zeta_b = jnp.broadcast_to(zeta[:, :, None], (H, C, RET_HEAD_DIM))
    xi_b = jnp.broadcast_to(xi[:, :, None], (H, C, RET_HEAD_DIM))
    g_chunk = tuple(float((1.0 - 2.0 ** (-5.0 - h)) ** C) for h in range(H))
    const = lambda shape: pl.BlockSpec(shape, lambda b, i: (0,) * len(shape))
    tile = lambda w: pl.BlockSpec((1, tm, w), lambda b, i: (b, i, 0))
    bf = lambda w: jax.ShapeDtypeStruct((B, S, w), BF16)
    return pl.pallas_call(
        partial(_in_proj_body, q_scale=q_scale, g_chunk=g_chunk),
        grid=(B, S // tm),
        in_specs=[tile(D), const((1, D)), const((D, IN_COLS_PAD)),
                  const((1, MLA_Q_LORA)), const((MLA_Q_LORA, MLA_HEADS * HEAD_SLAB)),
                  const((1, MLA_KV_LORA)), const((MLA_KV_LORA, MLA_HEADS * HEAD_SLAB)),
                  const((MLA_WIDTH, MLA_KV_LORA)),
                  tile(LANES), tile(LANES), tile(LANES), tile(LANES),
                  const((H, C, C)), const((H, C, C)), const((H, C, C)), const((1, RET_WIDTH))],
        out_specs=[tile(RET_WIDTH), tile(MLA_HEADS * HEAD_SLAB), tile(MLA_HEADS * HEAD_SLAB),
                   pl.BlockSpec((1, tm // TK, MLA_WIDTH, TK), lambda b, i: (b, i, 0, 0))],
        out_shape=[bf(RET_WIDTH), bf(MLA_HEADS * HEAD_SLAB), bf(MLA_HEADS * HEAD_SLAB),
                   jax.ShapeDtypeStruct((B, S // TK, MLA_WIDTH, TK), BF16)],
        scratch_shapes=[pltpu.VMEM((H, RET_HEAD_DIM, RET_HEAD_DIM), F32)],
        compiler_params=pltpu.CompilerParams(dimension_semantics=("arbitrary", "arbitrary"),
                                             vmem_limit_bytes=VMEM_LIMIT),
        name="in_proj",
    )(x, pre_mix_norm, w_in_p, q_norm, wuq_p, kv_norm, wuk_p, wvt, cr, sr, cm, sm,
      dmask, zeta_b, xi_b, ret_gn_w)


HEADS_PER_STEP = 2
PIECES = TQ // TK


def _attn_body(q_ref, k_ref, vt_ref, o_ref, *scratch):
    n = 2 * HEADS_PER_STEP
    s_ref = {(i // HEADS_PER_STEP, i % HEADS_PER_STEP): scratch[i] for i in range(n)}
    mb_ref = {(i // HEADS_PER_STEP, i % HEADS_PER_STEP): scratch[n + i] for i in range(n)}
    nq = q_ref.shape[1] // TQ
    ones = jnp.ones((ONES_ROWS, TK), BF16)
    chains = [(hh, c) for c in range(PIECES) for hh in range(HEADS_PER_STEP)]
    row = lax.broadcasted_iota(jnp.int32, (TK, TK), 0)
    col = lax.broadcasted_iota(jnp.int32, (TK, TK), 1)
    tri = row <= col

    def score_piece(qstart, j, slot, hh, c):
        kstart = pl.multiple_of(j * TK, TK)
        q0 = pl.multiple_of(qstart + c * TK, TK)
        cols = slice(hh * HEAD_SLAB, (hh + 1) * HEAD_SLAB)
        lanes = slice(c * TK, (c + 1) * TK)
        st = _dot_nt(k_ref[0, pl.ds(kstart, TK), cols], q_ref[0, pl.ds(q0, TK), cols])
        s_ref[slot, hh][:, lanes] = st
        mb_ref[slot, hh][:, lanes] = jnp.max(st, axis=0, keepdims=True)

    def acc_piece(j, slot, hh, c, state, masked):
        m, acc = state
        lanes = slice(c * TK, (c + 1) * TK)
        st = s_ref[slot, hh][:, lanes]
        if masked:
            st = jnp.where(tri, st, NEG_BIG)
            blk_max = jnp.max(st, axis=0, keepdims=True)
        else:
            blk_max = mb_ref[slot, hh][:, lanes]
        m_new = jnp.maximum(m, blk_max)
        p = jnp.exp2(st - m_new).astype(BF16)
        vte = jnp.concatenate([vt_ref[0, j, hh * MLA_V_DIM:(hh + 1) * MLA_V_DIM, :], ones], axis=0)
        return m_new, acc * jnp.exp2(m - m_new) + _dot(vte, p)

    def stage(carry, j, slot, first, diag, nxt_qstart, nxt_j, nxt_first):
        carry = dict(carry)
        for hh, c in chains:
            if c >= nxt_first:
                score_piece(nxt_qstart, nxt_j, 1 - slot, hh, c)
            if c >= first:
                carry[hh, c] = acc_piece(j, slot, hh, c, carry[hh, c], diag and c == first)
        return carry

    def tile(qi, _):
        qstart = qi * TQ

        def pair(t, flat):
            carry = dict(zip(chains, flat))
            carry = stage(carry, 2 * t, 0, 0, False, qstart, 2 * t + 1, 0)
            carry = stage(carry, 2 * t + 1, 1, 0, False, qstart, 2 * t + 2, 0)
            return tuple(carry[ch] for ch in chains)

        init = tuple((jnp.full((1, TK), NEG_BIG, F32), jnp.zeros((MLA_V_DIM + ONES_ROWS, TK), F32))
                     for _ in chains)
        d0 = PIECES * qi
        carry = dict(zip(chains, lax.fori_loop(0, d0 // 2, pair, init)))
        for d in range(PIECES):
            if d + 1 < PIECES:
                carry = stage(carry, d0 + d, d % 2, d, True, qstart, d0 + d + 1, d + 1)
            else:
                carry = stage(carry, d0 + d, d % 2, d, True, jnp.minimum(qi + 1, nq - 1) * TQ, 0, 0)
        for hh, c in chains:
            acc = carry[hh, c][1]
            o_ref[0, qi, hh * MLA_V_DIM:(hh + 1) * MLA_V_DIM, c * TK:(c + 1) * TK] = (
                acc[:MLA_V_DIM] / acc[MLA_V_DIM:MLA_V_DIM + 1]).astype(BF16)
        return 0

    for hh, c in chains:
        score_piece(0, 0, 0, hh, c)
    lax.fori_loop(0, nq, tile, 0)


def _mla_attention(q, k, vt):
    B, S, _ = q.shape
    nkb = S // TK
    nq = S // TQ
    hp = HEADS_PER_STEP
    return pl.pallas_call(
        _attn_body,
        grid=(B, MLA_HEADS // hp),
        in_specs=[pl.BlockSpec((1, S, hp * HEAD_SLAB), lambda b, g: (b, 0, g)),
                  pl.BlockSpec((1, S, hp * HEAD_SLAB), lambda b, g: (b, 0, g)),
                  pl.BlockSpec((1, nkb, hp * MLA_V_DIM, TK), lambda b, g: (b, 0, g, 0))],
        out_specs=pl.BlockSpec((1, nq, hp * MLA_V_DIM, TQ), lambda b, g: (b, 0, g, 0)),
        out_shape=jax.ShapeDtypeStruct((B, nq, MLA_WIDTH, TQ), BF16),
        scratch_shapes=[pltpu.VMEM((TK, TQ), F32)] * (2 * hp) + [pltpu.VMEM((1, TQ), F32)] * (2 * hp),
        compiler_params=pltpu.CompilerParams(dimension_semantics=("arbitrary", "arbitrary"),
                                             vmem_limit_bytes=VMEM_LIMIT),
        name="mla_attn",
    )(q, k, vt)


def _post_body(ret_ref, mla_ref, x_ref, p_ref, wo_ref, pmn_ref, pfn_ref, wg_ref, wu_ref, wd_ref,
               pofn_ref, wpp_ref, plen_ref, wpg_ref, bpg_ref, o_ref, act_ref):
    mix = _dot(ret_ref[0], wo_ref[0:RET_WIDTH, :]) + _dot_tn(mla_ref[0, 0], wo_ref[RET_WIDTH:, :])
    h1 = x_ref[0] + _rms(mix, pmn_ref[...])
    hn = _rms(h1, pfn_ref[...]).astype(BF16)
    for c in range(D_FF // FF_CHUNK):
        cols = slice(c * FF_CHUNK, (c + 1) * FF_CHUNK)
        g = _dot(hn, wg_ref[:, cols])
        u = _dot(hn, wu_ref[:, cols])
        act_ref[:, cols] = (g * jax.nn.sigmoid(g) * u).astype(BF16)
    ff = _dot(act_ref[...], wd_ref[...])
    h2 = h1 + _rms(ff, pofn_ref[...])
    e = _rms(_dot(p_ref[0].astype(BF16), wpp_ref[...]), plen_ref[...])
    gate = jax.nn.sigmoid(_dot(h2.astype(BF16), wpg_ref[...]) + bpg_ref[...])
    o_ref[0] = h2 + e * gate


def _post(ret_out, mla_out, x, p, w_o, post_mix_norm, pre_ffn_norm, w_gate, w_up, w_down,
          post_ffn_norm, w_ple_proj, ple_norm, w_ple_gate, b_ple_gate):
    B, S, D = x.shape
    tm = TM_POST
    const = lambda shape: pl.BlockSpec(shape, lambda b, i: (0,) * len(shape),
                                       pipeline_mode=pl.Buffered(1))
    tile = lambda w: pl.BlockSpec((1, tm, w), lambda b, i: (b, i, 0))
    return pl.pallas_call(
        _post_body,
        grid=(B, S // tm),
        in_specs=[tile(RET_WIDTH),
                  pl.BlockSpec((1, 1, MLA_WIDTH, tm), lambda b, i: (b, i // (TQ // tm), 0, i % (TQ // tm))),
                  tile(D), tile(PLE_DIM),
                  const((D, D)), const((1, D)), const((1, D)),
                  const((D, D_FF)), const((D, D_FF)), const((D_FF, D)), const((1, D)),
                  const((PLE_DIM, D)), const((1, D)), const((D, D)), const((1, D))],
        out_specs=tile(D),
        out_shape=jax.ShapeDtypeStruct((B, S, D), F32),
        scratch_shapes=[pltpu.VMEM((tm, D_FF), BF16)],
        compiler_params=pltpu.CompilerParams(dimension_semantics=("arbitrary", "arbitrary"),
                                             vmem_limit_bytes=VMEM_LIMIT),
        name="post",
    )(ret_out, mla_out, x, p, w_o, post_mix_norm, pre_ffn_norm, w_gate, w_up, w_down,
      post_ffn_norm, w_ple_proj, ple_norm, w_ple_gate, b_ple_gate)


def _prep_w_in(w_in):
    base = 4 * RET_WIDTH + MLA_Q_LORA + MLA_KV_LORA
    kr = w_in[:, base:base + MLA_ROPE_DIM]
    z = jnp.zeros((w_in.shape[0], MLA_NOPE_DIM), w_in.dtype)
    zt = jnp.zeros((w_in.shape[0], LANES - MLA_QK_DIM), w_in.dtype)
    return jnp.concatenate([w_in[:, :base], z, kr, zt], axis=1).astype(BF16)


def _pad_heads(w, per_head, keep):
    K = w.shape[0]
    w = w.reshape(K, MLA_HEADS, per_head)[:, :, :keep]
    w = jnp.pad(w, ((0, 0), (0, 0), (0, HEAD_SLAB - keep)))
    return w.reshape(K, MLA_HEADS * HEAD_SLAB).astype(BF16)


def kernel(x, p, positions, pre_mix_norm, w_in, ret_gn_w, mla_q_norm, w_uq, mla_kv_norm, w_ukv, w_o,
           post_mix_norm, pre_ffn_norm, w_gate, w_up, w_down, post_ffn_norm, w_ple_proj, ple_norm,
           w_ple_gate, b_ple_gate):
    depth = pre_mix_norm.shape[0]
    tables = _rope_tables(positions)
    h = x
    for i in range(depth):
        wuq_p = _pad_heads(w_uq[i], MLA_QK_DIM, MLA_QK_DIM)
        wuk_p = _pad_heads(w_ukv[i], MLA_NOPE_DIM + MLA_V_DIM, MLA_NOPE_DIM)
        wv = w_ukv[i].reshape(MLA_KV_LORA, MLA_HEADS, MLA_NOPE_DIM + MLA_V_DIM)[:, :, MLA_NOPE_DIM:]
        wvt = wv.reshape(MLA_KV_LORA, MLA_WIDTH).T.astype(BF16)
        ret_out, q, k, vt = _in_proj(h, pre_mix_norm[i][None], _prep_w_in(w_in[i]), mla_q_norm[i][None],
                                     wuq_p, mla_kv_norm[i][None], wuk_p, wvt, tables, ret_gn_w[i][None])
        mla_out = _mla_attention(q, k, vt)
        h = _post(ret_out, mla_out, h, p[i], w_o[i].astype(BF16), post_mix_norm[i][None],
                  pre_ffn_norm[i][None], w_gate[i].astype(BF16), w_up[i].astype(BF16),
                  w_down[i].astype(BF16), post_ffn_norm[i][None], w_ple_proj[i].astype(BF16),
                  ple_norm[i][None], w_ple_gate[i].astype(BF16), b_ple_gate[i][None])
    return h
```

```python
import math
from functools import partial

import jax
import jax.numpy as jnp
from jax import lax
from jax.experimental import pallas as pl
from jax.experimental.pallas import tpu as pltpu

D_MODEL = 1024
PLE_DIM = 256
RET_HEADS = 4
RET_HEAD_DIM = 128
RET_WIDTH = RET_HEADS * RET_HEAD_DIM
MLA_HEADS = 8
MLA_NOPE_DIM = 64
MLA_ROPE_DIM = 32
MLA_QK_DIM = MLA_NOPE_DIM + MLA_ROPE_DIM
MLA_V_DIM = 64
MLA_WIDTH = MLA_HEADS * MLA_V_DIM
MLA_Q_LORA = 384
MLA_KV_LORA = 256
D_FF = 2816
ROPE_BASE = 10000.0
EPS = 1e-6
NEG_BIG = -1e30

LANES = 128
HEAD_SLAB = 128
IN_COLS_MAIN = 4 * RET_WIDTH + MLA_Q_LORA + MLA_KV_LORA
VMEM_LIMIT = 56 * 1024 * 1024

TM_PROJ = 512
RET_CHUNK = 256
TQ = 2048
TK = 512
ONES_ROWS = 16
TM_POST = 512
FF_CHUNK = 256

BF16 = jnp.bfloat16
F32 = jnp.float32


def _dot(a, b):
    return jnp.dot(a, b, preferred_element_type=F32)


def _dot_nt(a, b):
    return lax.dot_general(a, b, (((1,), (1,)), ((), ())), preferred_element_type=F32)


def _dot_tn(a, b):
    return lax.dot_general(a, b, (((0,), (0,)), ((), ())), preferred_element_type=F32)


def _rms(v, w):
    return v * lax.rsqrt(jnp.mean(v * v, axis=-1, keepdims=True) + EPS) * w


TM_ROPE = 1024


def _rope_tables_body(pos_ref, invr_ref, invm_ref, cr_ref, sr_ref, cm_ref, sm_ref):
    pos = pos_ref[0].astype(F32)
    ang_r = invr_ref[...] * pos
    c = jnp.cos(ang_r)
    s = jnp.sin(ang_r)
    ang_m = invm_ref[...] * pos
    c16 = jnp.cos(ang_m)
    s16 = jnp.sin(ang_m)
    tm = pos.shape[1]
    pad = HEAD_SLAB - MLA_QK_DIM
    cr_ref[0] = jnp.concatenate([c, c], axis=0).T
    sr_ref[0] = jnp.concatenate([-s, s], axis=0).T
    cm_ref[0] = jnp.concatenate([jnp.ones((MLA_NOPE_DIM, tm), F32), c16, c16, jnp.zeros((pad, tm), F32)], axis=0).T
    sm_ref[0] = jnp.concatenate([jnp.zeros((MLA_NOPE_DIM, tm), F32), -s16, s16, jnp.zeros((pad, tm), F32)], axis=0).T


def _rope_tables(positions):
    B, S = positions.shape
    tm = TM_ROPE
    half_r = RET_HEAD_DIM // 2
    inv_r = 1.0 / (ROPE_BASE ** (jnp.arange(half_r, dtype=F32) / half_r))
    half_m = MLA_ROPE_DIM // 2
    inv_m = 1.0 / (ROPE_BASE ** (jnp.arange(half_m, dtype=F32) / half_m))
    inv_r = jnp.broadcast_to(inv_r[:, None], (half_r, tm))
    inv_m = jnp.broadcast_to(inv_m[:, None], (half_m, tm))
    tab = pl.BlockSpec((1, tm, LANES), lambda b, i: (b, i, 0))
    out = jax.ShapeDtypeStruct((B, S, LANES), F32)
    return pl.pallas_call(
        _rope_tables_body,
        grid=(B, S // tm),
        in_specs=[pl.BlockSpec((1, 1, tm), lambda b, i: (b, 0, i)),
                  pl.BlockSpec((half_r, tm), lambda b, i: (0, 0)),
                  pl.BlockSpec((half_m, tm), lambda b, i: (0, 0))],
        out_specs=[tab, tab, tab, tab],
        out_shape=[out, out, out, out],
        compiler_params=pltpu.CompilerParams(dimension_semantics=("arbitrary", "arbitrary"),
                                             vmem_limit_bytes=VMEM_LIMIT),
        name="rope_tables",
    )(positions.reshape(B, 1, S), inv_r, inv_m)


def _in_proj_body(x_ref, g_ref, win_ref, wkr_ref, qn_ref, wuq_ref, kvn_ref, wuk_ref, wvt_ref,
                  cr_ref, sr_ref, cm_ref, sm_ref, dmask_ref, zeta_ref, xi_ref, gnw_ref,
                  ret_ref, q_ref, k_ref, vt_ref, state_ref, *, q_scale, g_chunk):
    @pl.when(pl.program_id(1) == 0)
    def _():
        state_ref[...] = jnp.zeros_like(state_ref)

    xn = _rms(x_ref[0], g_ref[...]).astype(BF16)
    cr = cr_ref[0]
    sr = sr_ref[0]

    def ret_rope(v):
        return v * cr + pltpu.roll(v, RET_HEAD_DIM // 2, 1) * sr

    pq = _dot(xn, win_ref[:, 0:RET_WIDTH])
    pk = _dot(xn, win_ref[:, RET_WIDTH:2 * RET_WIDTH])
    pv = _dot(xn, win_ref[:, 2 * RET_WIDTH:3 * RET_WIDTH]).astype(BF16)
    pg = _dot(xn, win_ref[:, 3 * RET_WIDTH:4 * RET_WIDTH])

    C = RET_CHUNK
    head_cols = [slice(h * RET_HEAD_DIM, (h + 1) * RET_HEAD_DIM) for h in range(RET_HEADS)]
    rq = [ret_rope(pq[:, cols]).astype(BF16) for cols in head_cols]
    rk = [(ret_rope(pk[:, cols]) * (RET_HEAD_DIM ** -0.5)).astype(BF16) for cols in head_cols]

    def ret_chunk(c):
        rows = slice(c * C, (c + 1) * C)
        for h, cols in enumerate(head_cols):
            q = rq[h][rows]
            k = rk[h][rows]
            v = pv[rows, cols]
            scores = _dot_nt(q, k) * dmask_ref[h]
            inner = _dot(scores.astype(BF16), v)
            r_prev = state_ref[h]
            cross = _dot(q, r_prev.astype(BF16)) * xi_ref[h]
            kz = (k.astype(F32) * zeta_ref[h]).astype(BF16)
            state_ref[h] = g_chunk[h] * r_prev + _dot_tn(kz, v)
            y = inner + cross
            mu = jnp.mean(y, axis=-1, keepdims=True)
            yc = y - mu
            var = jnp.mean(yc * yc, axis=-1, keepdims=True)
            yn = yc * lax.rsqrt(var + EPS) * gnw_ref[:, cols]
            gate = pg[rows, cols]
            ret_ref[0, rows, cols] = (gate * jax.nn.sigmoid(gate) * yn).astype(BF16)

    cm = cm_ref[0]
    sm = sm_ref[0]
    lane = lax.broadcasted_iota(jnp.int32, cm.shape, 1)
    upper = lane >= (MLA_NOPE_DIM + MLA_ROPE_DIM // 2)

    def mla_rope(v):
        swapped = jnp.where(upper, pltpu.roll(v, MLA_ROPE_DIM // 2, 1),
                            pltpu.roll(v, LANES - MLA_ROPE_DIM // 2, 1))
        return v * cm + swapped * sm

    for c in range(TM_PROJ // C):
        ret_chunk(c)
    o = 4 * RET_WIDTH
    cq = _dot(xn, win_ref[:, o:o + MLA_Q_LORA])
    cqn = _rms(cq, qn_ref[...]).astype(BF16)
    qh = _dot(cqn, wuq_ref[...])
    for h in range(MLA_HEADS):
        sl = slice(h * HEAD_SLAB, (h + 1) * HEAD_SLAB)
        q_ref[0, :, sl] = (mla_rope(qh[:, sl]) * q_scale).astype(BF16)

    o += MLA_Q_LORA
    ckv = _dot(xn, win_ref[:, o:o + MLA_KV_LORA])
    ckvn = _rms(ckv, kvn_ref[...]).astype(BF16)
    kr = mla_rope(_dot(xn, wkr_ref[...]))
    kn = _dot(ckvn, wuk_ref[...])
    for h in range(MLA_HEADS):
        sl = slice(h * HEAD_SLAB, (h + 1) * HEAD_SLAB)
        k_ref[0, :, sl] = (kn[:, sl] + kr).astype(BF16)
    vt = _dot_nt(wvt_ref[...], ckvn).astype(BF16)
    for blk in range(TM_PROJ // TK):
        vt_ref[0, blk] = vt[:, blk * TK:(blk + 1) * TK]


def _in_proj(x, pre_mix_norm, w_in_p, w_kr, q_norm, wuq_p, kv_norm, wuk_p, wvt, tables, ret_gn_w):
    B, S, D = x.shape
    tm = TM_PROJ
    cr, sr, cm, sm = tables
    q_scale = (1.0 / math.sqrt(MLA_QK_DIM)) * math.log2(math.e)
    C = RET_CHUNK
    H = RET_HEADS
    log_g = jnp.log(1.0 - 2.0 ** (-5.0 - jnp.arange(H, dtype=F32)))
    j = jnp.arange(C, dtype=F32)
    diff = j[:, None] - j[None, :]
    dmask = jnp.where(diff[None] >= 0, jnp.exp(jnp.maximum(diff, 0.0)[None] * log_g[:, None, None]), 0.0)
    zeta = jnp.exp((C - 1 - j)[None, :] * log_g[:, None])
    xi = jnp.exp((j + 1)[None, :] * log_g[:, None])
    zeta_b = jnp.broadcast_to(zeta[:, :, None], (H, C, RET_HEAD_DIM))
    xi_b = jnp.broadcast_to(xi[:, :, None], (H, C, RET_HEAD_DIM))
    g_chunk = tuple(float((1.0 - 2.0 ** (-5.0 - h)) ** C) for h in range(H))
    const = lambda shape: pl.BlockSpec(shape, lambda b, i: (0,) * len(shape))
    tile = lambda w: pl.BlockSpec((1, tm, w), lambda b, i: (b, i, 0))
    bf = lambda w: jax.ShapeDtypeStruct((B, S, w), BF16)
    return pl.pallas_call(
        partial(_in_proj_body, q_scale=q_scale, g_chunk=g_chunk),
        grid=(B, S // tm),
        in_specs=[tile(D), const((1, D)), const((D, IN_COLS_MAIN)), const((D, LANES)),
                  const((1, MLA_Q_LORA)), const((MLA_Q_LORA, MLA_HEADS * HEAD_SLAB)),
                  const((1, MLA_KV_LORA)), const((MLA_KV_LORA, MLA_HEADS * HEAD_SLAB)),
                  const((MLA_WIDTH, MLA_KV_LORA)),
                  tile(LANES), tile(LANES), tile(LANES), tile(LANES),
                  const((H, C, C)), const((H, C, RET_HEAD_DIM)), const((H, C, RET_HEAD_DIM)),
                  const((1, RET_WIDTH))],
        out_specs=[tile(RET_WIDTH), tile(MLA_HEADS * HEAD_SLAB), tile(MLA_HEADS * HEAD_SLAB),
                   pl.BlockSpec((1, tm // TK, MLA_WIDTH, TK), lambda b, i: (b, i, 0, 0))],
        out_shape=[bf(RET_WIDTH), bf(MLA_HEADS * HEAD_SLAB), bf(MLA_HEADS * HEAD_SLAB),
                   jax.ShapeDtypeStruct((B, S // TK, MLA_WIDTH, TK), BF16)],
        scratch_shapes=[pltpu.VMEM((H, RET_HEAD_DIM, RET_HEAD_DIM), F32)],
        compiler_params=pltpu.CompilerParams(dimension_semantics=("arbitrary", "arbitrary"),
                                             vmem_limit_bytes=VMEM_LIMIT),
        name="in_proj",
    )(x, pre_mix_norm, w_in_p, w_kr, q_norm, wuq_p, kv_norm, wuk_p, wvt, cr, sr, cm, sm,
      dmask, zeta_b, xi_b, ret_gn_w)


HEADS_PER_STEP = 2
PIECES = TQ // TK


def _attn_body(q_ref, k_ref, vt_ref, o_ref, *scratch):
    n = 2 * HEADS_PER_STEP
    s_ref = {(i // HEADS_PER_STEP, i % HEADS_PER_STEP): scratch[i] for i in range(n)}
    mb_ref = {(i // HEADS_PER_STEP, i % HEADS_PER_STEP): scratch[n + i] for i in range(n)}
    nq = q_ref.shape[1] // TQ
    ones = jnp.ones((ONES_ROWS, TK), BF16)
    chains = [(hh, c) for c in range(PIECES) for hh in range(HEADS_PER_STEP)]
    row = lax.broadcasted_iota(jnp.int32, (TK, TK), 0)
    col = lax.broadcasted_iota(jnp.int32, (TK, TK), 1)
    tri = row <= col

    def score_piece(qstart, j, slot, hh, c):
        kstart = pl.multiple_of(j * TK, TK)
        q0 = pl.multiple_of(qstart + c * TK, TK)
        cols = slice(hh * HEAD_SLAB, (hh + 1) * HEAD_SLAB)
        lanes = slice(c * TK, (c + 1) * TK)
        st = _dot_nt(k_ref[0, pl.ds(kstart, TK), cols], q_ref[0, pl.ds(q0, TK), cols])
        s_ref[slot, hh][:, lanes] = st
        mb_ref[slot, hh][:, lanes] = jnp.max(st, axis=0, keepdims=True)

    def acc_piece(j, slot, hh, c, state, masked):
        m, acc = state
        lanes = slice(c * TK, (c + 1) * TK)
        st = s_ref[slot, hh][:, lanes]
        if masked:
            st = jnp.where(tri, st, NEG_BIG)
            blk_max = jnp.max(st, axis=0, keepdims=True)
        else:
            blk_max = mb_ref[slot, hh][:, lanes]
        m_new = jnp.maximum(m, blk_max)
        p = jnp.exp2(st - m_new).astype(BF16)
        vte = jnp.concatenate([vt_ref[0, j, hh * MLA_V_DIM:(hh + 1) * MLA_V_DIM, :], ones], axis=0)
        return m_new, acc * jnp.exp2(m - m_new) + _dot(vte, p)

    def stage(carry, j, slot, first, diag, nxt_qstart, nxt_j, nxt_first):
        carry = dict(carry)
        for hh, c in chains:
            if c >= nxt_first:
                score_piece(nxt_qstart, nxt_j, 1 - slot, hh, c)
            if c >= first:
                carry[hh, c] = acc_piece(j, slot, hh, c, carry[hh, c], diag and c == first)
        return carry

    def tile(qi, _):
        qstart = qi * TQ

        def pair(t, flat):
            carry = dict(zip(chains, flat))
            carry = stage(carry, 2 * t, 0, 0, False, qstart, 2 * t + 1, 0)
            carry = stage(carry, 2 * t + 1, 1, 0, False, qstart, 2 * t + 2, 0)
            return tuple(carry[ch] for ch in chains)

        init = tuple((jnp.full((1, TK), NEG_BIG, F32), jnp.zeros((MLA_V_DIM + ONES_ROWS, TK), F32))
                     for _ in chains)
        d0 = PIECES * qi
        carry = dict(zip(chains, lax.fori_loop(0, d0 // 2, pair, init)))
        for d in range(PIECES):
            if d + 1 < PIECES:
                carry = stage(carry, d0 + d, d % 2, d, True, qstart, d0 + d + 1, d + 1)
            else:
                carry = stage(carry, d0 + d, d % 2, d, True, jnp.minimum(qi + 1, nq - 1) * TQ, 0, 0)
        for hh, c in chains:
            acc = carry[hh, c][1]
            o_ref[0, qi, hh * MLA_V_DIM:(hh + 1) * MLA_V_DIM, c * TK:(c + 1) * TK] = (
                acc[:MLA_V_DIM] / acc[MLA_V_DIM:MLA_V_DIM + 1]).astype(BF16)
        return 0

    for hh, c in chains:
        score_piece(0, 0, 0, hh, c)
    lax.fori_loop(0, nq, tile, 0)


def _mla_attention(q, k, vt):
    B, S, _ = q.shape
    nkb = S // TK
    nq = S // TQ
    hp = HEADS_PER_STEP
    return pl.pallas_call(
        _attn_body,
        grid=(B, MLA_HEADS // hp),
        in_specs=[pl.BlockSpec((1, S, hp * HEAD_SLAB), lambda b, g: (b, 0, g)),
                  pl.BlockSpec((1, S, hp * HEAD_SLAB), lambda b, g: (b, 0, g)),
                  pl.BlockSpec((1, nkb, hp * MLA_V_DIM, TK), lambda b, g: (b, 0, g, 0))],
        out_specs=pl.BlockSpec((1, nq, hp * MLA_V_DIM, TQ), lambda b, g: (b, 0, g, 0)),
        out_shape=jax.ShapeDtypeStruct((B, nq, MLA_WIDTH, TQ), BF16),
        scratch_shapes=[pltpu.VMEM((TK, TQ), F32)] * (2 * hp) + [pltpu.VMEM((1, TQ), F32)] * (2 * hp),
        compiler_params=pltpu.CompilerParams(dimension_semantics=("arbitrary", "arbitrary"),
                                             vmem_limit_bytes=VMEM_LIMIT),
        name="mla_attn",
    )(q, k, vt)


def _post_body(ret_ref, mla_ref, x_ref, p_ref, wo_ref, pmn_ref, pfn_ref, wg_ref, wu_ref, wd_ref,
               pofn_ref, wpp_ref, plen_ref, wpg_ref, bpg_ref, o_ref, act_ref):
    mix = _dot(ret_ref[0], wo_ref[0:RET_WIDTH, :]) + _dot_tn(mla_ref[0, 0], wo_ref[RET_WIDTH:, :])
    h1 = x_ref[0] + _rms(mix, pmn_ref[...])
    hn = _rms(h1, pfn_ref[...]).astype(BF16)
    for c in range(D_FF // FF_CHUNK):
        cols = slice(c * FF_CHUNK, (c + 1) * FF_CHUNK)
        g = _dot(hn, wg_ref[:, cols])
        u = _dot(hn, wu_ref[:, cols])
        act_ref[:, cols] = (g * jax.nn.sigmoid(g) * u).astype(BF16)
    ff = _dot(act_ref[...], wd_ref[...])
    h2 = h1 + _rms(ff, pofn_ref[...])
    e = _rms(_dot(p_ref[0].astype(BF16), wpp_ref[...]), plen_ref[...])
    gate = jax.nn.sigmoid(_dot(h2.astype(BF16), wpg_ref[...]) + bpg_ref[...])
    o_ref[0] = h2 + e * gate


def _post(ret_out, mla_out, x, p, w_o, post_mix_norm, pre_ffn_norm, w_gate, w_up, w_down,
          post_ffn_norm, w_ple_proj, ple_norm, w_ple_gate, b_ple_gate):
    B, S, D = x.shape
    tm = TM_POST
    const = lambda shape: pl.BlockSpec(shape, lambda b, i: (0,) * len(shape),
                                       pipeline_mode=pl.Buffered(1))
    tile = lambda w: pl.BlockSpec((1, tm, w), lambda b, i: (b, i, 0))
    return pl.pallas_call(
        _post_body,
        grid=(B, S // tm),
        in_specs=[tile(RET_WIDTH),
                  pl.BlockSpec((1, 1, MLA_WIDTH, tm), lambda b, i: (b, i // (TQ // tm), 0, i % (TQ // tm))),
                  tile(D), tile(PLE_DIM),
                  const((D, D)), const((1, D)), const((1, D)),
                  const((D, D_FF)), const((D, D_FF)), const((D_FF, D)), const((1, D)),
                  const((PLE_DIM, D)), const((1, D)), const((D, D)), const((1, D))],
        out_specs=tile(D),
        out_shape=jax.ShapeDtypeStruct((B, S, D), F32),
        scratch_shapes=[pltpu.VMEM((tm, D_FF), BF16)],
        compiler_params=pltpu.CompilerParams(dimension_semantics=("arbitrary", "arbitrary"),
                                             vmem_limit_bytes=VMEM_LIMIT),
        name="post",
    )(ret_out, mla_out, x, p, w_o, post_mix_norm, pre_ffn_norm, w_gate, w_up, w_down,
      post_ffn_norm, w_ple_proj, ple_norm, w_ple_gate, b_ple_gate)


def _prep_w_in(w_in):
    kr = w_in[:, IN_COLS_MAIN:IN_COLS_MAIN + MLA_ROPE_DIM].astype(BF16)
    kr = jnp.pad(kr, ((0, 0), (MLA_NOPE_DIM, LANES - MLA_QK_DIM)))
    return w_in[:, :IN_COLS_MAIN].astype(BF16), kr


def _pad_heads(w, per_head, keep):
    K = w.shape[0]
    w = w.reshape(K, MLA_HEADS, per_head)[:, :, :keep]
    w = jnp.pad(w, ((0, 0), (0, 0), (0, HEAD_SLAB - keep)))
    return w.reshape(K, MLA_HEADS * HEAD_SLAB).astype(BF16)


def kernel(x, p, positions, pre_mix_norm, w_in, ret_gn_w, mla_q_norm, w_uq, mla_kv_norm, w_ukv, w_o,
           post_mix_norm, pre_ffn_norm, w_gate, w_up, w_down, post_ffn_norm, w_ple_proj, ple_norm,
           w_ple_gate, b_ple_gate):
    depth = pre_mix_norm.shape[0]
    tables = _rope_tables(positions)
    h = x
    for i in range(depth):
        wuq_p = _pad_heads(w_uq[i], MLA_QK_DIM, MLA_QK_DIM)
        wuk_p = _pad_heads(w_ukv[i], MLA_NOPE_DIM + MLA_V_DIM, MLA_NOPE_DIM)
        wv = w_ukv[i].reshape(MLA_KV_LORA, MLA_HEADS, MLA_NOPE_DIM + MLA_V_DIM)[:, :, MLA_NOPE_DIM:]
        wvt = wv.reshape(MLA_KV_LORA, MLA_WIDTH).T.astype(BF16)
        ret_out, q, k, vt = _in_proj(h, pre_mix_norm[i][None], *_prep_w_in(w_in[i]), mla_q_norm[i][None],
                                     wuq_p, mla_kv_norm[i][None], wuk_p, wvt, tables, ret_gn_w[i][None])
        mla_out = _mla_attention(q, k, vt)
        h = _post(ret_out, mla_out, h, p[i], w_o[i].astype(BF16), post_mix_norm[i][None],
                  pre_ffn_norm[i][None], w_gate[i].astype(BF16), w_up[i].astype(BF16),
                  w_down[i].astype(BF16), post_ffn_norm[i][None], w_ple_proj[i].astype(BF16),
                  ple_norm[i][None], w_ple_gate[i].astype(BF16), b_ple_gate[i][None])
    return h
```

```python
import math
from functools import partial

import jax
import jax.numpy as jnp
from jax import lax
from jax.experimental import pallas as pl
from jax.experimental.pallas import tpu as pltpu

D_MODEL = 1024
PLE_DIM = 256
RET_HEADS = 4
RET_HEAD_DIM = 128
RET_WIDTH = RET_HEADS * RET_HEAD_DIM
MLA_HEADS = 8
MLA_NOPE_DIM = 64
MLA_ROPE_DIM = 32
MLA_QK_DIM = MLA_NOPE_DIM + MLA_ROPE_DIM
MLA_V_DIM = 64
MLA_WIDTH = MLA_HEADS * MLA_V_DIM
MLA_Q_LORA = 384
MLA_KV_LORA = 256
D_FF = 2816
ROPE_BASE = 10000.0
EPS = 1e-6
NEG_BIG = -1e30

LANES = 128
HEAD_SLAB = 128
IN_COLS_MAIN = 4 * RET_WIDTH + MLA_Q_LORA + MLA_KV_LORA
VMEM_LIMIT = 56 * 1024 * 1024

TM_PROJ = 512
RET_CHUNK = 256
TQ = 2048
TK = 512
ONES_ROWS = 16
TM_POST = 512
FF_CHUNK = 256

BF16 = jnp.bfloat16
F32 = jnp.float32


def _dot(a, b):
    return jnp.dot(a, b, preferred_element_type=F32)


def _dot_nt(a, b):
    return lax.dot_general(a, b, (((1,), (1,)), ((), ())), preferred_element_type=F32)


def _dot_tn(a, b):
    return lax.dot_general(a, b, (((0,), (0,)), ((), ())), preferred_element_type=F32)


def _rms(v, w):
    return v * lax.rsqrt(jnp.mean(v * v, axis=-1, keepdims=True) + EPS) * w


TM_ROPE = 1024


def _rope_tables_body(pos_ref, invr_ref, invm_ref, cr_ref, sr_ref, cm_ref, sm_ref, c16_ref, s16_ref):
    pos = pos_ref[0].astype(F32)
    ang_r = invr_ref[...] * pos
    c = jnp.cos(ang_r)
    s = jnp.sin(ang_r)
    ang_m = invm_ref[...] * pos
    c16 = jnp.cos(ang_m)
    s16 = jnp.sin(ang_m)
    tm = pos.shape[1]
    pad = HEAD_SLAB - MLA_QK_DIM
    cr_ref[0] = jnp.concatenate([c, c], axis=0).T
    sr_ref[0] = jnp.concatenate([-s, s], axis=0).T
    cm_ref[0] = jnp.concatenate([jnp.ones((MLA_NOPE_DIM, tm), F32), c16, c16, jnp.zeros((pad, tm), F32)], axis=0).T
    sm_ref[0] = jnp.concatenate([jnp.zeros((MLA_NOPE_DIM, tm), F32), -s16, s16, jnp.zeros((pad, tm), F32)], axis=0).T
    c16_ref[0] = c16
    s16_ref[0] = s16


def _rope_tables(positions):
    B, S = positions.shape
    tm = TM_ROPE
    half_r = RET_HEAD_DIM // 2
    inv_r = 1.0 / (ROPE_BASE ** (jnp.arange(half_r, dtype=F32) / half_r))
    half_m = MLA_ROPE_DIM // 2
    inv_m = 1.0 / (ROPE_BASE ** (jnp.arange(half_m, dtype=F32) / half_m))
    inv_r = jnp.broadcast_to(inv_r[:, None], (half_r, tm))
    inv_m = jnp.broadcast_to(inv_m[:, None], (half_m, tm))
    tab = pl.BlockSpec((1, tm, LANES), lambda b, i: (b, i, 0))
    out = jax.ShapeDtypeStruct((B, S, LANES), F32)
    tab_t = pl.BlockSpec((1, half_m, tm), lambda b, i: (b, 0, i))
    out_t = jax.ShapeDtypeStruct((B, half_m, S), F32)
    return pl.pallas_call(
        _rope_tables_body,
        grid=(B, S // tm),
        in_specs=[pl.BlockSpec((1, 1, tm), lambda b, i: (b, 0, i)),
                  pl.BlockSpec((half_r, tm), lambda b, i: (0, 0)),
                  pl.BlockSpec((half_m, tm), lambda b, i: (0, 0))],
        out_specs=[tab, tab, tab, tab, tab_t, tab_t],
        out_shape=[out, out, out, out, out_t, out_t],
        compiler_params=pltpu.CompilerParams(dimension_semantics=("arbitrary", "arbitrary"),
                                             vmem_limit_bytes=VMEM_LIMIT),
        name="rope_tables",
    )(positions.reshape(B, 1, S), inv_r, inv_m)


def _in_proj_body(x_ref, g_ref, win_ref, wkr_ref, qn_ref, wuq_ref, kvn_ref, wuk_ref, wvt_ref,
                  cr_ref, sr_ref, cm_ref, sm_ref, c16_ref, s16_ref, dmask_ref, zeta_ref, xi_ref, gnw_ref,
                  ret_ref, qt_ref, k_ref, vt_ref, state_ref, *, q_scale, g_chunk):
    @pl.when(pl.program_id(1) == 0)
    def _():
        state_ref[...] = jnp.zeros_like(state_ref)

    xn = _rms(x_ref[0], g_ref[...]).astype(BF16)
    cr = cr_ref[0]
    sr = sr_ref[0]

    def ret_rope(v):
        return v * cr + pltpu.roll(v, RET_HEAD_DIM // 2, 1) * sr

    pq = _dot(xn, win_ref[:, 0:RET_WIDTH])
    pk = _dot(xn, win_ref[:, RET_WIDTH:2 * RET_WIDTH])
    pv = _dot(xn, win_ref[:, 2 * RET_WIDTH:3 * RET_WIDTH]).astype(BF16)
    pg = _dot(xn, win_ref[:, 3 * RET_WIDTH:4 * RET_WIDTH])

    C = RET_CHUNK
    head_cols = [slice(h * RET_HEAD_DIM, (h + 1) * RET_HEAD_DIM) for h in range(RET_HEADS)]
    rq = [ret_rope(pq[:, cols]).astype(BF16) for cols in head_cols]
    rk = [(ret_rope(pk[:, cols]) * (RET_HEAD_DIM ** -0.5)).astype(BF16) for cols in head_cols]

    def ret_chunk(c):
        rows = slice(c * C, (c + 1) * C)
        for h, cols in enumerate(head_cols):
            q = rq[h][rows]
            k = rk[h][rows]
            v = pv[rows, cols]
            scores = _dot_nt(q, k) * dmask_ref[h]
            inner = _dot(scores.astype(BF16), v)
            r_prev = state_ref[h]
            cross = _dot(q, r_prev.astype(BF16)) * xi_ref[h]
            kz = (k.astype(F32) * zeta_ref[h]).astype(BF16)
            state_ref[h] = g_chunk[h] * r_prev + _dot_tn(kz, v)
            y = inner + cross
            mu = jnp.mean(y, axis=-1, keepdims=True)
            yc = y - mu
            var = jnp.mean(yc * yc, axis=-1, keepdims=True)
            yn = yc * lax.rsqrt(var + EPS) * gnw_ref[:, cols]
            gate = pg[rows, cols]
            ret_ref[0, rows, cols] = (gate * jax.nn.sigmoid(gate) * yn).astype(BF16)

    cm = cm_ref[0]
    sm = sm_ref[0]
    lane = lax.broadcasted_iota(jnp.int32, cm.shape, 1)
    upper = lane >= (MLA_NOPE_DIM + MLA_ROPE_DIM // 2)

    def mla_rope(v):
        swapped = jnp.where(upper, pltpu.roll(v, MLA_ROPE_DIM // 2, 1),
                            pltpu.roll(v, LANES - MLA_ROPE_DIM // 2, 1))
        return v * cm + swapped * sm

    for c in range(TM_PROJ // C):
        ret_chunk(c)
    o = 4 * RET_WIDTH
    cq = _dot(xn, win_ref[:, o:o + MLA_Q_LORA])
    cqn = _rms(cq, qn_ref[...]).astype(BF16)
    qt = _dot_nt(wuq_ref[...], cqn)
    c16 = c16_ref[0]
    s16 = s16_ref[0]
    half = MLA_ROPE_DIM // 2
    for h in range(MLA_HEADS):
        r0 = h * HEAD_SLAB
        x1 = qt[r0 + MLA_NOPE_DIM:r0 + MLA_NOPE_DIM + half]
        x2 = qt[r0 + MLA_NOPE_DIM + half:r0 + MLA_QK_DIM]
        slab = jnp.concatenate([qt[r0:r0 + MLA_NOPE_DIM], x1 * c16 - x2 * s16, x2 * c16 + x1 * s16,
                                qt[r0 + MLA_QK_DIM:r0 + HEAD_SLAB]], axis=0)
        qt_ref[0, 0, r0:r0 + HEAD_SLAB, :] = (slab * q_scale).astype(BF16)

    o += MLA_Q_LORA
    ckv = _dot(xn, win_ref[:, o:o + MLA_KV_LORA])
    ckvn = _rms(ckv, kvn_ref[...]).astype(BF16)
    kr = mla_rope(_dot(xn, wkr_ref[...]))
    kn = _dot(ckvn, wuk_ref[...])
    for h in range(MLA_HEADS):
        sl = slice(h * HEAD_SLAB, (h + 1) * HEAD_SLAB)
        k_ref[0, :, sl] = (kn[:, sl] + kr).astype(BF16)
    vt = _dot_nt(wvt_ref[...], ckvn).astype(BF16)
    for blk in range(TM_PROJ // TK):
        vt_ref[0, blk] = vt[:, blk * TK:(blk + 1) * TK]


def _in_proj(x, pre_mix_norm, w_in_p, w_kr, q_norm, wuq_p, kv_norm, wuk_p, wvt, tables, ret_gn_w):
    B, S, D = x.shape
    tm = TM_PROJ
    assert tm == TK, "attention reads one transposed query block per key-block-sized piece"
    cr, sr, cm, sm, c16, s16 = tables
    q_scale = (1.0 / math.sqrt(MLA_QK_DIM)) * math.log2(math.e)
    C = RET_CHUNK
    H = RET_HEADS
    log_g = jnp.log(1.0 - 2.0 ** (-5.0 - jnp.arange(H, dtype=F32)))
    j = jnp.arange(C, dtype=F32)
    diff = j[:, None] - j[None, :]
    dmask = jnp.where(diff[None] >= 0, jnp.exp(jnp.maximum(diff, 0.0)[None] * log_g[:, None, None]), 0.0)
    zeta = jnp.exp((C - 1 - j)[None, :] * log_g[:, None])
    xi = jnp.exp((j + 1)[None, :] * log_g[:, None])
    zeta_b = jnp.broadcast_to(zeta[:, :, None], (H, C, RET_HEAD_DIM))
    xi_b = jnp.broadcast_to(xi[:, :, None], (H, C, RET_HEAD_DIM))
    g_chunk = tuple(float((1.0 - 2.0 ** (-5.0 - h)) ** C) for h in range(H))
    const = lambda shape: pl.BlockSpec(shape, lambda b, i: (0,) * len(shape))
    tile = lambda w: pl.BlockSpec((1, tm, w), lambda b, i: (b, i, 0))
    bf = lambda w: jax.ShapeDtypeStruct((B, S, w), BF16)
    return pl.pallas_call(
        partial(_in_proj_body, q_scale=q_scale, g_chunk=g_chunk),
        grid=(B, S // tm),
        in_specs=[tile(D), const((1, D)), const((D, IN_COLS_MAIN)), const((D, LANES)),
                  const((1, MLA_Q_LORA)), const((MLA_HEADS * HEAD_SLAB, MLA_Q_LORA)),
                  const((1, MLA_KV_LORA)), const((MLA_KV_LORA, MLA_HEADS * HEAD_SLAB)),
                  const((MLA_WIDTH, MLA_KV_LORA)),
                  tile(LANES), tile(LANES), tile(LANES), tile(LANES),
                  pl.BlockSpec((1, MLA_ROPE_DIM // 2, tm), lambda b, i: (b, 0, i)),
                  pl.BlockSpec((1, MLA_ROPE_DIM // 2, tm), lambda b, i: (b, 0, i)),
                  const((H, C, C)), const((H, C, RET_HEAD_DIM)), const((H, C, RET_HEAD_DIM)),
                  const((1, RET_WIDTH))],
        out_specs=[tile(RET_WIDTH),
                   pl.BlockSpec((1, 1, MLA_HEADS * HEAD_SLAB, tm), lambda b, i: (b, i, 0, 0)),
                   tile(MLA_HEADS * HEAD_SLAB),
                   pl.BlockSpec((1, tm // TK, MLA_WIDTH, TK), lambda b, i: (b, i, 0, 0))],
        out_shape=[bf(RET_WIDTH),
                   jax.ShapeDtypeStruct((B, S // tm, MLA_HEADS * HEAD_SLAB, tm), BF16),
                   bf(MLA_HEADS * HEAD_SLAB),
                   jax.ShapeDtypeStruct((B, S // TK, MLA_WIDTH, TK), BF16)],
        scratch_shapes=[pltpu.VMEM((H, RET_HEAD_DIM, RET_HEAD_DIM), F32)],
        compiler_params=pltpu.CompilerParams(dimension_semantics=("arbitrary", "arbitrary"),
                                             vmem_limit_bytes=VMEM_LIMIT),
        name="in_proj",
    )(x, pre_mix_norm, w_in_p, w_kr, q_norm, wuq_p, kv_norm, wuk_p, wvt, cr, sr, cm, sm, c16, s16,
      dmask, zeta_b, xi_b, ret_gn_w)


HEADS_PER_STEP = 2
PIECES = TQ // TK


def _attn_body(qt_ref, k_ref, vt_ref, o_ref, *scratch):
    n = 2 * HEADS_PER_STEP
    s_ref = {(i // HEADS_PER_STEP, i % HEADS_PER_STEP): scratch[i] for i in range(n)}
    mb_ref = {(i // HEADS_PER_STEP, i % HEADS_PER_STEP): scratch[n + i] for i in range(n)}
    nq = qt_ref.shape[1] // PIECES
    ones = jnp.ones((ONES_ROWS, TK), BF16)
    chains = [(hh, c) for c in range(PIECES) for hh in range(HEADS_PER_STEP)]
    row = lax.broadcasted_iota(jnp.int32, (TK, TK), 0)
    col = lax.broadcasted_iota(jnp.int32, (TK, TK), 1)
    tri = row <= col

    def score_piece(qstart, j, slot, hh, c):
        kstart = pl.multiple_of(j * TK, TK)
        cols = slice(hh * HEAD_SLAB, (hh + 1) * HEAD_SLAB)
        lanes = slice(c * TK, (c + 1) * TK)
        st = _dot(k_ref[0, pl.ds(kstart, TK), cols], qt_ref[0, qstart + c, cols, :])
        s_ref[slot, hh][:, lanes] = st
        mb_ref[slot, hh][:, lanes] = jnp.max(st, axis=0, keepdims=True)

    def acc_piece(j, slot, hh, c, state, masked):
        m, acc = state
        lanes = slice(c * TK, (c + 1) * TK)
        st = s_ref[slot, hh][:, lanes]
        if masked:
            st = jnp.where(tri, st, NEG_BIG)
            blk_max = jnp.max(st, axis=0, keepdims=True)
        else:
            blk_max = mb_ref[slot, hh][:, lanes]
        m_new = jnp.maximum(m, blk_max)
        p = jnp.exp2(st - m_new).astype(BF16)
        vte = jnp.concatenate([vt_ref[0, j, hh * MLA_V_DIM:(hh + 1) * MLA_V_DIM, :], ones], axis=0)
        return m_new, acc * jnp.exp2(m - m_new) + _dot(vte, p)

    def stage(carry, j, slot, first, diag, nxt_qstart, nxt_j, nxt_first):
        carry = dict(carry)
        for hh, c in chains:
            if c >= nxt_first:
                score_piece(nxt_qstart, nxt_j, 1 - slot, hh, c)
            if c >= first:
                carry[hh, c] = acc_piece(j, slot, hh, c, carry[hh, c], diag and c == first)
        return carry

    def tile(qi, _):
        qstart = qi * PIECES

        def pair(t, flat):
            carry = dict(zip(chains, flat))
            carry = stage(carry, 2 * t, 0, 0, False, qstart, 2 * t + 1, 0)
            carry = stage(carry, 2 * t + 1, 1, 0, False, qstart, 2 * t + 2, 0)
            return tuple(carry[ch] for ch in chains)

        init = tuple((jnp.full((1, TK), NEG_BIG, F32), jnp.zeros((MLA_V_DIM + ONES_ROWS, TK), F32))
                     for _ in chains)
        d0 = PIECES * qi
        carry = dict(zip(chains, lax.fori_loop(0, d0 // 2, pair, init)))
        for d in range(PIECES):
            if d + 1 < PIECES:
                carry = stage(carry, d0 + d, d % 2, d, True, qstart, d0 + d + 1, d + 1)
            else:
                carry = stage(carry, d0 + d, d % 2, d, True, jnp.minimum(qi + 1, nq - 1) * PIECES, 0, 0)
        for hh, c in chains:
            acc = carry[hh, c][1]
            o_ref[0, qi, hh * MLA_V_DIM:(hh + 1) * MLA_V_DIM, c * TK:(c + 1) * TK] = (
                acc[:MLA_V_DIM] / acc[MLA_V_DIM:MLA_V_DIM + 1]).astype(BF16)
        return 0

    for hh, c in chains:
        score_piece(0, 0, 0, hh, c)
    lax.fori_loop(0, nq, tile, 0)


def _mla_attention(qt, k, vt):
    B, S, _ = k.shape
    nkb = S // TK
    nq = S // TQ
    hp = HEADS_PER_STEP
    return pl.pallas_call(
        _attn_body,
        grid=(B, MLA_HEADS // hp),
        in_specs=[pl.BlockSpec((1, nkb, hp * HEAD_SLAB, TK), lambda b, g: (b, 0, g, 0)),
                  pl.BlockSpec((1, S, hp * HEAD_SLAB), lambda b, g: (b, 0, g)),
                  pl.BlockSpec((1, nkb, hp * MLA_V_DIM, TK), lambda b, g: (b, 0, g, 0))],
        out_specs=pl.BlockSpec((1, nq, hp * MLA_V_DIM, TQ), lambda b, g: (b, 0, g, 0)),
        out_shape=jax.ShapeDtypeStruct((B, nq, MLA_WIDTH, TQ), BF16),
        scratch_shapes=[pltpu.VMEM((TK, TQ), F32)] * (2 * hp) + [pltpu.VMEM((1, TQ), F32)] * (2 * hp),
        compiler_params=pltpu.CompilerParams(dimension_semantics=("arbitrary", "arbitrary"),
                                             vmem_limit_bytes=VMEM_LIMIT),
        name="mla_attn",
    )(qt, k, vt)


def _post_body(ret_ref, mla_ref, x_ref, p_ref, wo_ref, pmn_ref, pfn_ref, wg_ref, wu_ref, wd_ref,
               pofn_ref, wpp_ref, plen_ref, wpg_ref, bpg_ref, o_ref, act_ref):
    mix = _dot(ret_ref[0], wo_ref[0:RET_WIDTH, :]) + _dot_tn(mla_ref[0, 0], wo_ref[RET_WIDTH:, :])
    h1 = x_ref[0] + _rms(mix, pmn_ref[...])
    hn = _rms(h1, pfn_ref[...]).astype(BF16)
    for c in range(D_FF // FF_CHUNK):
        cols = slice(c * FF_CHUNK, (c + 1) * FF_CHUNK)
        g = _dot(hn, wg_ref[:, cols])
        u = _dot(hn, wu_ref[:, cols])
        act_ref[:, cols] = (g * jax.nn.sigmoid(g) * u).astype(BF16)
    ff = _dot(act_ref[...], wd_ref[...])
    h2 = h1 + _rms(ff, pofn_ref[...])
    e = _rms(_dot(p_ref[0].astype(BF16), wpp_ref[...]), plen_ref[...])
    gate = jax.nn.sigmoid(_dot(h2.astype(BF16), wpg_ref[...]) + bpg_ref[...])
    o_ref[0] = h2 + e * gate


def _post(ret_out, mla_out, x, p, w_o, post_mix_norm, pre_ffn_norm, w_gate, w_up, w_down,
          post_ffn_norm, w_ple_proj, ple_norm, w_ple_gate, b_ple_gate):
    B, S, D = x.shape
    tm = TM_POST
    const = lambda shape: pl.BlockSpec(shape, lambda b, i: (0,) * len(shape),
                                       pipeline_mode=pl.Buffered(1))
    tile = lambda w: pl.BlockSpec((1, tm, w), lambda b, i: (b, i, 0))
    return pl.pallas_call(
        _post_body,
        grid=(B, S // tm),
        in_specs=[tile(RET_WIDTH),
                  pl.BlockSpec((1, 1, MLA_WIDTH, tm), lambda b, i: (b, i // (TQ // tm), 0, i % (TQ // tm))),
                  tile(D), tile(PLE_DIM),
                  const((D, D)), const((1, D)), const((1, D)),
                  const((D, D_FF)), const((D, D_FF)), const((D_FF, D)), const((1, D)),
                  const((PLE_DIM, D)), const((1, D)), const((D, D)), const((1, D))],
        out_specs=tile(D),
        out_shape=jax.ShapeDtypeStruct((B, S, D), F32),
        scratch_shapes=[pltpu.VMEM((tm, D_FF), BF16)],
        compiler_params=pltpu.CompilerParams(dimension_semantics=("arbitrary", "arbitrary"),
                                             vmem_limit_bytes=VMEM_LIMIT),
        name="post",
    )(ret_out, mla_out, x, p, w_o, post_mix_norm, pre_ffn_norm, w_gate, w_up, w_down,
      post_ffn_norm, w_ple_proj, ple_norm, w_ple_gate, b_ple_gate)


def _prep_w_in(w_in):
    kr = w_in[:, IN_COLS_MAIN:IN_COLS_MAIN + MLA_ROPE_DIM].astype(BF16)
    kr = jnp.pad(kr, ((0, 0), (MLA_NOPE_DIM, LANES - MLA_QK_DIM)))
    return w_in[:, :IN_COLS_MAIN].astype(BF16), kr


def _pad_heads(w, per_head, keep):
    K = w.shape[0]
    w = w.reshape(K, MLA_HEADS, per_head)[:, :, :keep]
    w = jnp.pad(w, ((0, 0), (0, 0), (0, HEAD_SLAB - keep)))
    return w.reshape(K, MLA_HEADS * HEAD_SLAB).astype(BF16)


def kernel(x, p, positions, pre_mix_norm, w_in, ret_gn_w, mla_q_norm, w_uq, mla_kv_norm, w_ukv, w_o,
           post_mix_norm, pre_ffn_norm, w_gate, w_up, w_down, post_ffn_norm, w_ple_proj, ple_norm,
           w_ple_gate, b_ple_gate):
    depth = pre_mix_norm.shape[0]
    tables = _rope_tables(positions)
    h = x
    for i in range(depth):
        wuq_p = _pad_heads(w_uq[i], MLA_QK_DIM, MLA_QK_DIM).T
        wuk_p = _pad_heads(w_ukv[i], MLA_NOPE_DIM + MLA_V_DIM, MLA_NOPE_DIM)
        wv = w_ukv[i].reshape(MLA_KV_LORA, MLA_HEADS, MLA_NOPE_DIM + MLA_V_DIM)[:, :, MLA_NOPE_DIM:]
        wvt = wv.reshape(MLA_KV_LORA, MLA_WIDTH).T.astype(BF16)
        ret_out, q, k, vt = _in_proj(h, pre_mix_norm[i][None], *_prep_w_in(w_in[i]), mla_q_norm[i][None],
                                     wuq_p, mla_kv_norm[i][None], wuk_p, wvt, tables, ret_gn_w[i][None])
        mla_out = _mla_attention(q, k, vt)
        h = _post(ret_out, mla_out, h, p[i], w_o[i].astype(BF16), post_mix_norm[i][None],
                  pre_ffn_norm[i][None], w_gate[i].astype(BF16), w_up[i].astype(BF16),
                  w_down[i].astype(BF16), post_ffn_norm[i][None], w_ple_proj[i].astype(BF16),
                  ple_norm[i][None], w_ple_gate[i].astype(BF16), b_ple_gate[i][None])
    return h
```

```python
import math
from functools import partial

import jax
import jax.numpy as jnp
from jax import lax
from jax.experimental import pallas as pl
from jax.experimental.pallas import tpu as pltpu

D_MODEL = 1024
PLE_DIM = 256
RET_HEADS = 4
RET_HEAD_DIM = 128
RET_WIDTH = RET_HEADS * RET_HEAD_DIM
MLA_HEADS = 8
MLA_NOPE_DIM = 64
MLA_ROPE_DIM = 32
MLA_QK_DIM = MLA_NOPE_DIM + MLA_ROPE_DIM
MLA_V_DIM = 64
MLA_WIDTH = MLA_HEADS * MLA_V_DIM
MLA_Q_LORA = 384
MLA_KV_LORA = 256
D_FF = 2816
ROPE_BASE = 10000.0
EPS = 1e-6
NEG_BIG = -1e30

LANES = 128
HEAD_SLAB = 128
IN_COLS_MAIN = 4 * RET_WIDTH + MLA_Q_LORA + MLA_KV_LORA
VMEM_LIMIT = 58 * 1024 * 1024

TM_PROJ = 512
RET_CHUNK = 256
TQ = 2048
TK = 512
ONES_ROWS = 16
TM_POST = 1024
FF_CHUNK = 256

BF16 = jnp.bfloat16
F32 = jnp.float32


def _dot(a, b):
    return jnp.dot(a, b, preferred_element_type=F32)


def _dot_nt(a, b):
    return lax.dot_general(a, b, (((1,), (1,)), ((), ())), preferred_element_type=F32)


def _dot_tn(a, b):
    return lax.dot_general(a, b, (((0,), (0,)), ((), ())), preferred_element_type=F32)


def _rms(v, w):
    return v * lax.rsqrt(jnp.mean(v * v, axis=-1, keepdims=True) + EPS) * w


TM_ROPE = 1024


def _rope_tables_body(pos_ref, invr_ref, invm_ref, cr_ref, sr_ref, cm_ref, sm_ref, c16_ref, s16_ref):
    pos = pos_ref[0].astype(F32)
    ang_r = invr_ref[...] * pos
    c = jnp.cos(ang_r)
    s = jnp.sin(ang_r)
    ang_m = invm_ref[...] * pos
    c16 = jnp.cos(ang_m)
    s16 = jnp.sin(ang_m)
    tm = pos.shape[1]
    pad = HEAD_SLAB - MLA_QK_DIM
    cr_ref[0] = jnp.concatenate([c, c], axis=0).T
    sr_ref[0] = jnp.concatenate([-s, s], axis=0).T
    cm_ref[0] = jnp.concatenate([jnp.ones((MLA_NOPE_DIM, tm), F32), c16, c16, jnp.zeros((pad, tm), F32)], axis=0).T
    sm_ref[0] = jnp.concatenate([jnp.zeros((MLA_NOPE_DIM, tm), F32), -s16, s16, jnp.zeros((pad, tm), F32)], axis=0).T
    c16_ref[0] = c16
    s16_ref[0] = s16


def _rope_tables(positions):
    B, S = positions.shape
    tm = TM_ROPE
    half_r = RET_HEAD_DIM // 2
    inv_r = 1.0 / (ROPE_BASE ** (jnp.arange(half_r, dtype=F32) / half_r))
    half_m = MLA_ROPE_DIM // 2
    inv_m = 1.0 / (ROPE_BASE ** (jnp.arange(half_m, dtype=F32) / half_m))
    inv_r = jnp.broadcast_to(inv_r[:, None], (half_r, tm))
    inv_m = jnp.broadcast_to(inv_m[:, None], (half_m, tm))
    tab = pl.BlockSpec((1, tm, LANES), lambda b, i: (b, i, 0))
    out = jax.ShapeDtypeStruct((B, S, LANES), F32)
    tab_t = pl.BlockSpec((1, half_m, tm), lambda b, i: (b, 0, i))
    out_t = jax.ShapeDtypeStruct((B, half_m, S), F32)
    return pl.pallas_call(
        _rope_tables_body,
        grid=(B, S // tm),
        in_specs=[pl.BlockSpec((1, 1, tm), lambda b, i: (b, 0, i)),
                  pl.BlockSpec((half_r, tm), lambda b, i: (0, 0)),
                  pl.BlockSpec((half_m, tm), lambda b, i: (0, 0))],
        out_specs=[tab, tab, tab, tab, tab_t, tab_t],
        out_shape=[out, out, out, out, out_t, out_t],
        compiler_params=pltpu.CompilerParams(dimension_semantics=("arbitrary", "arbitrary"),
                                             vmem_limit_bytes=VMEM_LIMIT),
        name="rope_tables",
    )(positions.reshape(B, 1, S), inv_r, inv_m)


def _in_proj_body(x_ref, g_ref, win_ref, wkr_ref, qn_ref, wuq_ref, kvn_ref, wuk_ref, wvt_ref,
                  cr_ref, sr_ref, cm_ref, sm_ref, c16_ref, s16_ref, dmask_ref, zeta_ref, xi_ref, gnw_ref,
                  ret_ref, qt_ref, k_ref, vt_ref, state_ref, *, q_scale, g_chunk):
    @pl.when(pl.program_id(1) == 0)
    def _():
        state_ref[...] = jnp.zeros_like(state_ref)

    xn = _rms(x_ref[0], g_ref[...]).astype(BF16)
    cr = cr_ref[0]
    sr = sr_ref[0]

    def ret_rope(v):
        return v * cr + pltpu.roll(v, RET_HEAD_DIM // 2, 1) * sr

    pq = _dot(xn, win_ref[:, 0:RET_WIDTH])
    pk = _dot(xn, win_ref[:, RET_WIDTH:2 * RET_WIDTH])
    pv = _dot(xn, win_ref[:, 2 * RET_WIDTH:3 * RET_WIDTH]).astype(BF16)
    pg = _dot(xn, win_ref[:, 3 * RET_WIDTH:4 * RET_WIDTH])

    C = RET_CHUNK
    head_cols = [slice(h * RET_HEAD_DIM, (h + 1) * RET_HEAD_DIM) for h in range(RET_HEADS)]
    rq = [ret_rope(pq[:, cols]).astype(BF16) for cols in head_cols]
    rk = [(ret_rope(pk[:, cols]) * (RET_HEAD_DIM ** -0.5)).astype(BF16) for cols in head_cols]

    def ret_chunk(c):
        rows = slice(c * C, (c + 1) * C)
        for h, cols in enumerate(head_cols):
            q = rq[h][rows]
            k = rk[h][rows]
            v = pv[rows, cols]
            scores = _dot_nt(q, k) * dmask_ref[h]
            inner = _dot(scores.astype(BF16), v)
            r_prev = state_ref[h]
            cross = _dot(q, r_prev.astype(BF16)) * xi_ref[h]
            kz = (k.astype(F32) * zeta_ref[h]).astype(BF16)
            state_ref[h] = g_chunk[h] * r_prev + _dot_tn(kz, v)
            y = inner + cross
            mu = jnp.mean(y, axis=-1, keepdims=True)
            yc = y - mu
            var = jnp.mean(yc * yc, axis=-1, keepdims=True)
            yn = yc * lax.rsqrt(var + EPS) * gnw_ref[:, cols]
            gate = pg[rows, cols]
            ret_ref[0, rows, cols] = (gate * jax.nn.sigmoid(gate) * yn).astype(BF16)

    cm = cm_ref[0]
    sm = sm_ref[0]
    lane = lax.broadcasted_iota(jnp.int32, cm.shape, 1)
    upper = lane >= (MLA_NOPE_DIM + MLA_ROPE_DIM // 2)

    def mla_rope(v):
        swapped = jnp.where(upper, pltpu.roll(v, MLA_ROPE_DIM // 2, 1),
                            pltpu.roll(v, LANES - MLA_ROPE_DIM // 2, 1))
        return v * cm + swapped * sm

    for c in range(TM_PROJ // C):
        ret_chunk(c)
    o = 4 * RET_WIDTH
    cq = _dot(xn, win_ref[:, o:o + MLA_Q_LORA])
    cqn = _rms(cq, qn_ref[...]).astype(BF16)
    qt = _dot_nt(wuq_ref[...], cqn)
    c16 = c16_ref[0]
    s16 = s16_ref[0]
    half = MLA_ROPE_DIM // 2
    for h in range(MLA_HEADS):
        r0 = h * HEAD_SLAB
        x1 = qt[r0 + MLA_NOPE_DIM:r0 + MLA_NOPE_DIM + half]
        x2 = qt[r0 + MLA_NOPE_DIM + half:r0 + MLA_QK_DIM]
        slab = jnp.concatenate([qt[r0:r0 + MLA_NOPE_DIM], x1 * c16 - x2 * s16, x2 * c16 + x1 * s16,
                                qt[r0 + MLA_QK_DIM:r0 + HEAD_SLAB]], axis=0)
        qt_ref[0, 0, r0:r0 + HEAD_SLAB, :] = (slab * q_scale).astype(BF16)

    o += MLA_Q_LORA
    ckv = _dot(xn, win_ref[:, o:o + MLA_KV_LORA])
    ckvn = _rms(ckv, kvn_ref[...]).astype(BF16)
    kr = mla_rope(_dot(xn, wkr_ref[...]))
    kn = _dot(ckvn, wuk_ref[...])
    for h in range(MLA_HEADS):
        sl = slice(h * HEAD_SLAB, (h + 1) * HEAD_SLAB)
        k_ref[0, :, sl] = (kn[:, sl] + kr).astype(BF16)
    vt = _dot_nt(wvt_ref[...], ckvn).astype(BF16)
    for blk in range(TM_PROJ // TK):
        vt_ref[0, blk] = vt[:, blk * TK:(blk + 1) * TK]


def _in_proj(x, pre_mix_norm, w_in_p, w_kr, q_norm, wuq_p, kv_norm, wuk_p, wvt, tables, ret_gn_w):
    B, S, D = x.shape
    tm = TM_PROJ
    assert tm == TK, "attention reads one transposed query block per key-block-sized piece"
    cr, sr, cm, sm, c16, s16 = tables
    q_scale = (1.0 / math.sqrt(MLA_QK_DIM)) * math.log2(math.e)
    C = RET_CHUNK
    H = RET_HEADS
    log_g = jnp.log(1.0 - 2.0 ** (-5.0 - jnp.arange(H, dtype=F32)))
    j = jnp.arange(C, dtype=F32)
    diff = j[:, None] - j[None, :]
    dmask = jnp.where(diff[None] >= 0, jnp.exp(jnp.maximum(diff, 0.0)[None] * log_g[:, None, None]), 0.0)
    zeta = jnp.exp((C - 1 - j)[None, :] * log_g[:, None])
    xi = jnp.exp((j + 1)[None, :] * log_g[:, None])
    zeta_b = jnp.broadcast_to(zeta[:, :, None], (H, C, RET_HEAD_DIM))
    xi_b = jnp.broadcast_to(xi[:, :, None], (H, C, RET_HEAD_DIM))
    g_chunk = tuple(float((1.0 - 2.0 ** (-5.0 - h)) ** C) for h in range(H))
    const = lambda shape: pl.BlockSpec(shape, lambda b, i: (0,) * len(shape))
    tile = lambda w: pl.BlockSpec((1, tm, w), lambda b, i: (b, i, 0))
    bf = lambda w: jax.ShapeDtypeStruct((B, S, w), BF16)
    return pl.pallas_call(
        partial(_in_proj_body, q_scale=q_scale, g_chunk=g_chunk),
        grid=(B, S // tm),
        in_specs=[tile(D), const((1, D)), const((D, IN_COLS_MAIN)), const((D, LANES)),
                  const((1, MLA_Q_LORA)), const((MLA_HEADS * HEAD_SLAB, MLA_Q_LORA)),
                  const((1, MLA_KV_LORA)), const((MLA_KV_LORA, MLA_HEADS * HEAD_SLAB)),
                  const((MLA_WIDTH, MLA_KV_LORA)),
                  tile(LANES), tile(LANES), tile(LANES), tile(LANES),
                  pl.BlockSpec((1, MLA_ROPE_DIM // 2, tm), lambda b, i: (b, 0, i)),
                  pl.BlockSpec((1, MLA_ROPE_DIM // 2, tm), lambda b, i: (b, 0, i)),
                  const((H, C, C)), const((H, C, RET_HEAD_DIM)), const((H, C, RET_HEAD_DIM)),
                  const((1, RET_WIDTH))],
        out_specs=[tile(RET_WIDTH),
                   pl.BlockSpec((1, 1, MLA_HEADS * HEAD_SLAB, tm), lambda b, i: (b, i, 0, 0)),
                   tile(MLA_HEADS * HEAD_SLAB),
                   pl.BlockSpec((1, tm // TK, MLA_WIDTH, TK), lambda b, i: (b, i, 0, 0))],
        out_shape=[bf(RET_WIDTH),
                   jax.ShapeDtypeStruct((B, S // tm, MLA_HEADS * HEAD_SLAB, tm), BF16),
                   bf(MLA_HEADS * HEAD_SLAB),
                   jax.ShapeDtypeStruct((B, S // TK, MLA_WIDTH, TK), BF16)],
        scratch_shapes=[pltpu.VMEM((H, RET_HEAD_DIM, RET_HEAD_DIM), F32)],
        compiler_params=pltpu.CompilerParams(dimension_semantics=("arbitrary", "arbitrary"),
                                             vmem_limit_bytes=VMEM_LIMIT),
        name="in_proj",
    )(x, pre_mix_norm, w_in_p, w_kr, q_norm, wuq_p, kv_norm, wuk_p, wvt, cr, sr, cm, sm, c16, s16,
      dmask, zeta_b, xi_b, ret_gn_w)


HEADS_PER_STEP = 2
PIECES = TQ // TK


def _attn_body(qt_ref, k_ref, vt_ref, o_ref, *scratch):
    n = 2 * HEADS_PER_STEP
    s_ref = {(i // HEADS_PER_STEP, i % HEADS_PER_STEP): scratch[i] for i in range(n)}
    mb_ref = {(i // HEADS_PER_STEP, i % HEADS_PER_STEP): scratch[n + i] for i in range(n)}
    nq = qt_ref.shape[1] // PIECES
    ones = jnp.ones((ONES_ROWS, TK), BF16)
    chains = [(hh, c) for c in range(PIECES) for hh in range(HEADS_PER_STEP)]
    row = lax.broadcasted_iota(jnp.int32, (TK, TK), 0)
    col = lax.broadcasted_iota(jnp.int32, (TK, TK), 1)
    tri = row <= col

    def score_piece(qstart, j, slot, hh, c):
        kstart = pl.multiple_of(j * TK, TK)
        cols = slice(hh * HEAD_SLAB, (hh + 1) * HEAD_SLAB)
        lanes = slice(c * TK, (c + 1) * TK)
        st = _dot(k_ref[0, pl.ds(kstart, TK), cols], qt_ref[0, qstart + c, cols, :])
        s_ref[slot, hh][:, lanes] = st
        mb_ref[slot, hh][:, lanes] = jnp.max(st, axis=0, keepdims=True)

    def acc_piece(j, slot, hh, c, state, masked):
        m, acc = state
        lanes = slice(c * TK, (c + 1) * TK)
        st = s_ref[slot, hh][:, lanes]
        if masked:
            st = jnp.where(tri, st, NEG_BIG)
            blk_max = jnp.max(st, axis=0, keepdims=True)
        else:
            blk_max = mb_ref[slot, hh][:, lanes]
        m_new = jnp.maximum(m, blk_max)
        p = jnp.exp2(st - m_new).astype(BF16)
        vte = jnp.concatenate([vt_ref[0, j, hh * MLA_V_DIM:(hh + 1) * MLA_V_DIM, :], ones], axis=0)
        return m_new, acc * jnp.exp2(m - m_new) + _dot(vte, p)

    def stage(carry, j, slot, first, diag, nxt_qstart, nxt_j, nxt_first):
        carry = dict(carry)
        for hh, c in chains:
            if c >= nxt_first:
                score_piece(nxt_qstart, nxt_j, 1 - slot, hh, c)
            if c >= first:
                carry[hh, c] = acc_piece(j, slot, hh, c, carry[hh, c], diag and c == first)
        return carry

    def tile(qi, _):
        qstart = qi * PIECES

        def pair(t, flat):
            carry = dict(zip(chains, flat))
            carry = stage(carry, 2 * t, 0, 0, False, qstart, 2 * t + 1, 0)
            carry = stage(carry, 2 * t + 1, 1, 0, False, qstart, 2 * t + 2, 0)
            return tuple(carry[ch] for ch in chains)

        init = tuple((jnp.full((1, TK), NEG_BIG, F32), jnp.zeros((MLA_V_DIM + ONES_ROWS, TK), F32))
                     for _ in chains)
        d0 = PIECES * qi
        carry = dict(zip(chains, lax.fori_loop(0, d0 // 2, pair, init)))
        for d in range(PIECES):
            if d + 1 < PIECES:
                carry = stage(carry, d0 + d, d % 2, d, True, qstart, d0 + d + 1, d + 1)
            else:
                carry = stage(carry, d0 + d, d % 2, d, True, jnp.minimum(qi + 1, nq - 1) * PIECES, 0, 0)
        for hh, c in chains:
            acc = carry[hh, c][1]
            o_ref[0, qi, hh * MLA_V_DIM:(hh + 1) * MLA_V_DIM, c * TK:(c + 1) * TK] = (
                acc[:MLA_V_DIM] / acc[MLA_V_DIM:MLA_V_DIM + 1]).astype(BF16)
        return 0

    for hh, c in chains:
        score_piece(0, 0, 0, hh, c)
    lax.fori_loop(0, nq, tile, 0)


def _mla_attention(qt, k, vt):
    B, S, _ = k.shape
    nkb = S // TK
    nq = S // TQ
    hp = HEADS_PER_STEP
    return pl.pallas_call(
        _attn_body,
        grid=(B, MLA_HEADS // hp),
        in_specs=[pl.BlockSpec((1, nkb, hp * HEAD_SLAB, TK), lambda b, g: (b, 0, g, 0)),
                  pl.BlockSpec((1, S, hp * HEAD_SLAB), lambda b, g: (b, 0, g)),
                  pl.BlockSpec((1, nkb, hp * MLA_V_DIM, TK), lambda b, g: (b, 0, g, 0))],
        out_specs=pl.BlockSpec((1, nq, hp * MLA_V_DIM, TQ), lambda b, g: (b, 0, g, 0)),
        out_shape=jax.ShapeDtypeStruct((B, nq, MLA_WIDTH, TQ), BF16),
        scratch_shapes=[pltpu.VMEM((TK, TQ), F32)] * (2 * hp) + [pltpu.VMEM((1, TQ), F32)] * (2 * hp),
        compiler_params=pltpu.CompilerParams(dimension_semantics=("arbitrary", "arbitrary"),
                                             vmem_limit_bytes=VMEM_LIMIT),
        name="mla_attn",
    )(qt, k, vt)


def _post_body(ret_ref, mla_ref, x_ref, p_ref, wo_ref, pmn_ref, pfn_ref, wg_ref, wu_ref, wd_ref,
               pofn_ref, wpp_ref, plen_ref, wpg_ref, bpg_ref, o_ref, act_ref):
    tm = x_ref.shape[1]
    halves = [slice(0, tm // 2), slice(tm // 2, tm)]
    mix = [_dot(ret_ref[0, r], wo_ref[0:RET_WIDTH, :]) + _dot_tn(mla_ref[0, 0][:, r], wo_ref[RET_WIDTH:, :])
           for r in halves]
    h1 = [x_ref[0, r] + _rms(m, pmn_ref[...]) for r, m in zip(halves, mix)]
    hn = [_rms(h, pfn_ref[...]).astype(BF16) for h in h1]
    for c in range(D_FF // FF_CHUNK):
        cols = slice(c * FF_CHUNK, (c + 1) * FF_CHUNK)
        for r, hnr in zip(halves, hn):
            g = _dot(hnr, wg_ref[:, cols])
            u = _dot(hnr, wu_ref[:, cols])
            act_ref[r, cols] = (g * jax.nn.sigmoid(g) * u).astype(BF16)
    ff = [_dot(act_ref[r, :], wd_ref[...]) for r in halves]
    h2 = [h + _rms(f, pofn_ref[...]) for h, f in zip(h1, ff)]
    e = [_rms(_dot(p_ref[0, r].astype(BF16), wpp_ref[...]), plen_ref[...]) for r in halves]
    gate = [jax.nn.sigmoid(_dot(h.astype(BF16), wpg_ref[...]) + bpg_ref[...]) for h in h2]
    for r, h, ee, gg in zip(halves, h2, e, gate):
        o_ref[0, r] = h + ee * gg


def _post(ret_out, mla_out, x, p, w_o, post_mix_norm, pre_ffn_norm, w_gate, w_up, w_down,
          post_ffn_norm, w_ple_proj, ple_norm, w_ple_gate, b_ple_gate):
    B, S, D = x.shape
    tm = TM_POST
    const = lambda shape: pl.BlockSpec(shape, lambda b, i: (0,) * len(shape),
                                       pipeline_mode=pl.Buffered(1))
    tile = lambda w: pl.BlockSpec((1, tm, w), lambda b, i: (b, i, 0))
    return pl.pallas_call(
        _post_body,
        grid=(B, S // tm),
        in_specs=[tile(RET_WIDTH),
                  pl.BlockSpec((1, 1, MLA_WIDTH, tm), lambda b, i: (b, i // (TQ // tm), 0, i % (TQ // tm))),
                  tile(D), tile(PLE_DIM),
                  const((D, D)), const((1, D)), const((1, D)),
                  const((D, D_FF)), const((D, D_FF)), const((D_FF, D)), const((1, D)),
                  const((PLE_DIM, D)), const((1, D)), const((D, D)), const((1, D))],
        out_specs=tile(D),
        out_shape=jax.ShapeDtypeStruct((B, S, D), F32),
        scratch_shapes=[pltpu.VMEM((tm, D_FF), BF16)],
        compiler_params=pltpu.CompilerParams(dimension_semantics=("arbitrary", "arbitrary"),
                                             vmem_limit_bytes=VMEM_LIMIT),
        name="post",
    )(ret_out, mla_out, x, p, w_o, post_mix_norm, pre_ffn_norm, w_gate, w_up, w_down,
      post_ffn_norm, w_ple_proj, ple_norm, w_ple_gate, b_ple_gate)


def _prep_w_in(w_in):
    kr = w_in[:, IN_COLS_MAIN:IN_COLS_MAIN + MLA_ROPE_DIM].astype(BF16)
    kr = jnp.pad(kr, ((0, 0), (MLA_NOPE_DIM, LANES - MLA_QK_DIM)))
    return w_in[:, :IN_COLS_MAIN].astype(BF16), kr


def _pad_heads(w, per_head, keep):
    K = w.shape[0]
    w = w.reshape(K, MLA_HEADS, per_head)[:, :, :keep]
    w = jnp.pad(w, ((0, 0), (0, 0), (0, HEAD_SLAB - keep)))
    return w.reshape(K, MLA_HEADS * HEAD_SLAB).astype(BF16)


def kernel(x, p, positions, pre_mix_norm, w_in, ret_gn_w, mla_q_norm, w_uq, mla_kv_norm, w_ukv, w_o,
           post_mix_norm, pre_ffn_norm, w_gate, w_up, w_down, post_ffn_norm, w_ple_proj, ple_norm,
           w_ple_gate, b_ple_gate):
    depth = pre_mix_norm.shape[0]
    tables = _rope_tables(positions)
    h = x
    for i in range(depth):
        wuq_p = _pad_heads(w_uq[i], MLA_QK_DIM, MLA_QK_DIM).T
        wuk_p = _pad_heads(w_ukv[i], MLA_NOPE_DIM + MLA_V_DIM, MLA_NOPE_DIM)
        wv = w_ukv[i].reshape(MLA_KV_LORA, MLA_HEADS, MLA_NOPE_DIM + MLA_V_DIM)[:, :, MLA_NOPE_DIM:]
        wvt = wv.reshape(MLA_KV_LORA, MLA_WIDTH).T.astype(BF16)
        ret_out, q, k, vt = _in_proj(h, pre_mix_norm[i][None], *_prep_w_in(w_in[i]), mla_q_norm[i][None],
                                     wuq_p, mla_kv_norm[i][None], wuk_p, wvt, tables, ret_gn_w[i][None])
        mla_out = _mla_attention(q, k, vt)
        h = _post(ret_out, mla_out, h, p[i], w_o[i].astype(BF16), post_mix_norm[i][None],
                  pre_ffn_norm[i][None], w_gate[i].astype(BF16), w_up[i].astype(BF16),
                  w_down[i].astype(BF16), post_ffn_norm[i][None], w_ple_proj[i].astype(BF16),
                  ple_norm[i][None], w_ple_gate[i].astype(BF16), b_ple_gate[i][None])
    return h
```

```python
import math
from functools import partial

import jax
import jax.numpy as jnp
from jax import lax
from jax.experimental import pallas as pl
from jax.experimental.pallas import tpu as pltpu

D_MODEL = 1024
PLE_DIM = 256
RET_HEADS = 4
RET_HEAD_DIM = 128
RET_WIDTH = RET_HEADS * RET_HEAD_DIM
MLA_HEADS = 8
MLA_NOPE_DIM = 64
MLA_ROPE_DIM = 32
MLA_QK_DIM = MLA_NOPE_DIM + MLA_ROPE_DIM
MLA_V_DIM = 64
MLA_WIDTH = MLA_HEADS * MLA_V_DIM
MLA_Q_LORA = 384
MLA_KV_LORA = 256
D_FF = 2816
ROPE_BASE = 10000.0
EPS = 1e-6
NEG_BIG = -1e30

LANES = 128
HEAD_SLAB = 128
IN_COLS_CQ_END = 4 * RET_WIDTH + MLA_Q_LORA
IN_COLS_MAIN = IN_COLS_CQ_END + MLA_KV_LORA
IN_COLS_ALL = IN_COLS_MAIN + MLA_ROPE_DIM
VMEM_LIMIT = 58 * 1024 * 1024

TM_PROJ = 512
RET_CHUNK = 256
TQ = 2048
TK = 512
ONES_ROWS = 16
TM_POST = 1024
FF_CHUNK = 256

BF16 = jnp.bfloat16
F32 = jnp.float32


def _dot(a, b):
    return jnp.dot(a, b, preferred_element_type=F32)


def _dot_nt(a, b):
    return lax.dot_general(a, b, (((1,), (1,)), ((), ())), preferred_element_type=F32)


def _dot_tn(a, b):
    return lax.dot_general(a, b, (((0,), (0,)), ((), ())), preferred_element_type=F32)


def _rms(v, w):
    return v * lax.rsqrt(jnp.mean(v * v, axis=-1, keepdims=True) + EPS) * w


TM_ROPE = 1024


def _rope_tables_body(pos_ref, invr_ref, invm_ref, cr_ref, sr_ref, cm_ref, sm_ref, c16_ref, s16_ref):
    pos = pos_ref[0].astype(F32)
    ang_r = invr_ref[...] * pos
    c = jnp.cos(ang_r)
    s = jnp.sin(ang_r)
    ang_m = invm_ref[...] * pos
    c16 = jnp.cos(ang_m)
    s16 = jnp.sin(ang_m)
    tm = pos.shape[1]
    pad = HEAD_SLAB - MLA_QK_DIM
    cr_ref[0] = jnp.concatenate([c, c], axis=0).T
    sr_ref[0] = jnp.concatenate([-s, s], axis=0).T
    cm_ref[0] = jnp.concatenate([jnp.ones((MLA_NOPE_DIM, tm), F32), c16, c16, jnp.zeros((pad, tm), F32)], axis=0).T
    sm_ref[0] = jnp.concatenate([jnp.zeros((MLA_NOPE_DIM, tm), F32), -s16, s16, jnp.zeros((pad, tm), F32)], axis=0).T
    c16_ref[0] = c16
    s16_ref[0] = s16


def _rope_tables(positions):
    B, S = positions.shape
    tm = TM_ROPE
    half_r = RET_HEAD_DIM // 2
    inv_r = 1.0 / (ROPE_BASE ** (jnp.arange(half_r, dtype=F32) / half_r))
    half_m = MLA_ROPE_DIM // 2
    inv_m = 1.0 / (ROPE_BASE ** (jnp.arange(half_m, dtype=F32) / half_m))
    inv_r = jnp.broadcast_to(inv_r[:, None], (half_r, tm))
    inv_m = jnp.broadcast_to(inv_m[:, None], (half_m, tm))
    tab = pl.BlockSpec((1, tm, LANES), lambda b, i: (b, i, 0))
    out = jax.ShapeDtypeStruct((B, S, LANES), F32)
    tab_t = pl.BlockSpec((1, half_m, tm), lambda b, i: (b, 0, i))
    out_t = jax.ShapeDtypeStruct((B, half_m, S), F32)
    return pl.pallas_call(
        _rope_tables_body,
        grid=(B, S // tm),
        in_specs=[pl.BlockSpec((1, 1, tm), lambda b, i: (b, 0, i)),
                  pl.BlockSpec((half_r, tm), lambda b, i: (0, 0)),
                  pl.BlockSpec((half_m, tm), lambda b, i: (0, 0))],
        out_specs=[tab, tab, tab, tab, tab_t, tab_t],
        out_shape=[out, out, out, out, out_t, out_t],
        compiler_params=pltpu.CompilerParams(dimension_semantics=("arbitrary", "arbitrary"),
                                             vmem_limit_bytes=VMEM_LIMIT),
        name="rope_tables",
    )(positions.reshape(B, 1, S), inv_r, inv_m)


def _in_proj_body(x_ref, g_ref, win32_ref, place_ref, qn_ref, wuq_ref, kvn_ref, wuk_ref, wvt_ref,
                  cr_ref, sr_ref, cm_ref, sm_ref, c16_ref, s16_ref, dmask_ref, zeta_ref, xi_ref, gnw_ref,
                  ret_ref, qt_ref, k_ref, vt_ref, state_ref, win_ref, *, q_scale, g_chunk):
    @pl.when((pl.program_id(0) == 0) & (pl.program_id(1) == 0))
    def _():
        a, b = IN_COLS_CQ_END, IN_COLS_MAIN
        win_ref[:, 0:a] = win32_ref[:, 0:a].astype(BF16)
        kr_w = win32_ref[:, b:b + MLA_ROPE_DIM].astype(BF16)
        win_ref[:, a:a + LANES] = _dot(kr_w, place_ref[...]).astype(BF16)
        win_ref[:, a + LANES:] = win32_ref[:, a:b].astype(BF16)

    @pl.when(pl.program_id(1) == 0)
    def _():
        state_ref[...] = jnp.zeros_like(state_ref)

    xn = _rms(x_ref[0], g_ref[...]).astype(BF16)
    cr = cr_ref[0]
    sr = sr_ref[0]

    def ret_rope(v):
        return v * cr + pltpu.roll(v, RET_HEAD_DIM // 2, 1) * sr

    pq = _dot(xn, win_ref[:, 0:RET_WIDTH])
    pk = _dot(xn, win_ref[:, RET_WIDTH:2 * RET_WIDTH])
    pv = _dot(xn, win_ref[:, 2 * RET_WIDTH:3 * RET_WIDTH]).astype(BF16)
    pg = _dot(xn, win_ref[:, 3 * RET_WIDTH:4 * RET_WIDTH])

    C = RET_CHUNK
    head_cols = [slice(h * RET_HEAD_DIM, (h + 1) * RET_HEAD_DIM) for h in range(RET_HEADS)]
    rq = [ret_rope(pq[:, cols]).astype(BF16) for cols in head_cols]
    rk = [(ret_rope(pk[:, cols]) * (RET_HEAD_DIM ** -0.5)).astype(BF16) for cols in head_cols]

    def ret_chunk(c):
        rows = slice(c * C, (c + 1) * C)
        for h, cols in enumerate(head_cols):
            q = rq[h][rows]
            k = rk[h][rows]
            v = pv[rows, cols]
            scores = _dot_nt(q, k) * dmask_ref[h]
            inner = _dot(scores.astype(BF16), v)
            r_prev = state_ref[h]
            cross = _dot(q, r_prev.astype(BF16)) * xi_ref[h]
            kz = (k.astype(F32) * zeta_ref[h]).astype(BF16)
            state_ref[h] = g_chunk[h] * r_prev + _dot_tn(kz, v)
            y = inner + cross
            mu = jnp.mean(y, axis=-1, keepdims=True)
            yc = y - mu
            var = jnp.mean(yc * yc, axis=-1, keepdims=True)
            yn = yc * lax.rsqrt(var + EPS) * gnw_ref[:, cols]
            gate = pg[rows, cols]
            ret_ref[0, rows, cols] = (gate * jax.nn.sigmoid(gate) * yn).astype(BF16)

    cm = cm_ref[0]
    sm = sm_ref[0]
    lane = lax.broadcasted_iota(jnp.int32, cm.shape, 1)
    upper = lane >= (MLA_NOPE_DIM + MLA_ROPE_DIM // 2)

    def mla_rope(v):
        swapped = jnp.where(upper, pltpu.roll(v, MLA_ROPE_DIM // 2, 1),
                            pltpu.roll(v, LANES - MLA_ROPE_DIM // 2, 1))
        return v * cm + swapped * sm

    for c in range(TM_PROJ // C):
        ret_chunk(c)
    o = 4 * RET_WIDTH
    cq = _dot(xn, win_ref[:, o:IN_COLS_CQ_END])
    cqn = _rms(cq, qn_ref[...]).astype(BF16)
    qt = _dot_nt(wuq_ref[...], cqn)
    c16 = c16_ref[0]
    s16 = s16_ref[0]
    half = MLA_ROPE_DIM // 2
    for h in range(MLA_HEADS):
        r0 = h * HEAD_SLAB
        x1 = qt[r0 + MLA_NOPE_DIM:r0 + MLA_NOPE_DIM + half]
        x2 = qt[r0 + MLA_NOPE_DIM + half:r0 + MLA_QK_DIM]
        slab = jnp.concatenate([qt[r0:r0 + MLA_NOPE_DIM], x1 * c16 - x2 * s16, x2 * c16 + x1 * s16,
                                qt[r0 + MLA_QK_DIM:r0 + HEAD_SLAB]], axis=0)
        qt_ref[0, 0, r0:r0 + HEAD_SLAB, :] = (slab * q_scale).astype(BF16)

    ckv = _dot(xn, win_ref[:, IN_COLS_CQ_END + LANES:])
    ckvn = _rms(ckv, kvn_ref[...]).astype(BF16)
    kr = mla_rope(_dot(xn, win_ref[:, IN_COLS_CQ_END:IN_COLS_CQ_END + LANES]))
    kn = _dot(ckvn, wuk_ref[...])
    for h in range(MLA_HEADS):
        sl = slice(h * HEAD_SLAB, (h + 1) * HEAD_SLAB)
        k_ref[0, :, sl] = (kn[:, sl] + kr).astype(BF16)
    vt = _dot_nt(wvt_ref[...], ckvn).astype(BF16)
    for blk in range(TM_PROJ // TK):
        vt_ref[0, blk] = vt[:, blk * TK:(blk + 1) * TK]


def _in_proj(x, pre_mix_norm, w_in, q_norm, wuq_p, kv_norm, wuk_p, wvt, tables, ret_gn_w):
    B, S, D = x.shape
    tm = TM_PROJ
    place = (jnp.arange(MLA_ROPE_DIM)[:, None] + MLA_NOPE_DIM == jnp.arange(LANES)[None, :]).astype(BF16)
    assert tm == TK, "attention reads one transposed query block per key-block-sized piece"
    cr, sr, cm, sm, c16, s16 = tables
    q_scale = (1.0 / math.sqrt(MLA_QK_DIM)) * math.log2(math.e)
    C = RET_CHUNK
    H = RET_HEADS
    log_g = jnp.log(1.0 - 2.0 ** (-5.0 - jnp.arange(H, dtype=F32)))
    j = jnp.arange(C, dtype=F32)
    diff = j[:, None] - j[None, :]
    dmask = jnp.where(diff[None] >= 0, jnp.exp(jnp.maximum(diff, 0.0)[None] * log_g[:, None, None]), 0.0)
    zeta = jnp.exp((C - 1 - j)[None, :] * log_g[:, None])
    xi = jnp.exp((j + 1)[None, :] * log_g[:, None])
    zeta_b = jnp.broadcast_to(zeta[:, :, None], (H, C, RET_HEAD_DIM))
    xi_b = jnp.broadcast_to(xi[:, :, None], (H, C, RET_HEAD_DIM))
    g_chunk = tuple(float((1.0 - 2.0 ** (-5.0 - h)) ** C) for h in range(H))
    const = lambda shape: pl.BlockSpec(shape, lambda b, i: (0,) * len(shape))
    tile = lambda w: pl.BlockSpec((1, tm, w), lambda b, i: (b, i, 0))
    bf = lambda w: jax.ShapeDtypeStruct((B, S, w), BF16)
    return pl.pallas_call(
        partial(_in_proj_body, q_scale=q_scale, g_chunk=g_chunk),
        grid=(B, S // tm),
        in_specs=[tile(D), const((1, D)),
                  pl.BlockSpec((D, IN_COLS_ALL), lambda b, i: (0, 0), pipeline_mode=pl.Buffered(1)),
                  const((MLA_ROPE_DIM, LANES)),
                  const((1, MLA_Q_LORA)), const((MLA_HEADS * HEAD_SLAB, MLA_Q_LORA)),
                  const((1, MLA_KV_LORA)), const((MLA_KV_LORA, MLA_HEADS * HEAD_SLAB)),
                  const((MLA_WIDTH, MLA_KV_LORA)),
                  tile(LANES), tile(LANES), tile(LANES), tile(LANES),
                  pl.BlockSpec((1, MLA_ROPE_DIM // 2, tm), lambda b, i: (b, 0, i)),
                  pl.BlockSpec((1, MLA_ROPE_DIM // 2, tm), lambda b, i: (b, 0, i)),
                  const((H, C, C)), const((H, C, RET_HEAD_DIM)), const((H, C, RET_HEAD_DIM)),
                  const((1, RET_WIDTH))],
        out_specs=[tile(RET_WIDTH),
                   pl.BlockSpec((1, 1, MLA_HEADS * HEAD_SLAB, tm), lambda b, i: (b, i, 0, 0)),
                   tile(MLA_HEADS * HEAD_SLAB),
                   pl.BlockSpec((1, tm // TK, MLA_WIDTH, TK), lambda b, i: (b, i, 0, 0))],
        out_shape=[bf(RET_WIDTH),
                   jax.ShapeDtypeStruct((B, S // tm, MLA_HEADS * HEAD_SLAB, tm), BF16),
                   bf(MLA_HEADS * HEAD_SLAB),
                   jax.ShapeDtypeStruct((B, S // TK, MLA_WIDTH, TK), BF16)],
        scratch_shapes=[pltpu.VMEM((H, RET_HEAD_DIM, RET_HEAD_DIM), F32),
                        pltpu.VMEM((D, IN_COLS_MAIN + LANES), BF16)],
        compiler_params=pltpu.CompilerParams(dimension_semantics=("arbitrary", "arbitrary"),
                                             vmem_limit_bytes=VMEM_LIMIT),
        name="in_proj",
    )(x, pre_mix_norm, w_in, place, q_norm, wuq_p, kv_norm, wuk_p, wvt, cr, sr, cm, sm, c16, s16,
      dmask, zeta_b, xi_b, ret_gn_w)


HEADS_PER_STEP = 2
PIECES = TQ // TK


def _attn_body(qt_ref, k_ref, vt_ref, o_ref, *scratch):
    n = 2 * HEADS_PER_STEP
    s_ref = {(i // HEADS_PER_STEP, i % HEADS_PER_STEP): scratch[i] for i in range(n)}
    mb_ref = {(i // HEADS_PER_STEP, i % HEADS_PER_STEP): scratch[n + i] for i in range(n)}
    nq = qt_ref.shape[1] // PIECES
    ones = jnp.ones((ONES_ROWS, TK), BF16)
    chains = [(hh, c) for c in range(PIECES) for hh in range(HEADS_PER_STEP)]
    row = lax.broadcasted_iota(jnp.int32, (TK, TK), 0)
    col = lax.broadcasted_iota(jnp.int32, (TK, TK), 1)
    tri = row <= col

    def score_piece(qstart, j, slot, hh, c):
        kstart = pl.multiple_of(j * TK, TK)
        cols = slice(hh * HEAD_SLAB, (hh + 1) * HEAD_SLAB)
        lanes = slice(c * TK, (c + 1) * TK)
        st = _dot(k_ref[0, pl.ds(kstart, TK), cols], qt_ref[0, qstart + c, cols, :])
        s_ref[slot, hh][:, lanes] = st
        mb_ref[slot, hh][:, lanes] = jnp.max(st, axis=0, keepdims=True)

    def acc_piece(j, slot, hh, c, state, masked):
        m, acc = state
        lanes = slice(c * TK, (c + 1) * TK)
        st = s_ref[slot, hh][:, lanes]
        if masked:
            st = jnp.where(tri, st, NEG_BIG)
            blk_max = jnp.max(st, axis=0, keepdims=True)
        else:
            blk_max = mb_ref[slot, hh][:, lanes]
        m_new = jnp.maximum(m, blk_max)
        p = jnp.exp2(st - m_new).astype(BF16)
        vte = jnp.concatenate([vt_ref[0, j, hh * MLA_V_DIM:(hh + 1) * MLA_V_DIM, :], ones], axis=0)
        return m_new, acc * jnp.exp2(m - m_new) + _dot(vte, p)

    def stage(carry, j, slot, first, diag, nxt_qstart, nxt_j, nxt_first):
        carry = dict(carry)
        for hh, c in chains:
            if c >= nxt_first:
                score_piece(nxt_qstart, nxt_j, 1 - slot, hh, c)
            if c >= first:
                carry[hh, c] = acc_piece(j, slot, hh, c, carry[hh, c], diag and c == first)
        return carry

    def tile(qi, _):
        qstart = qi * PIECES

        def pair(t, flat):
            carry = dict(zip(chains, flat))
            carry = stage(carry, 2 * t, 0, 0, False, qstart, 2 * t + 1, 0)
            carry = stage(carry, 2 * t + 1, 1, 0, False, qstart, 2 * t + 2, 0)
            return tuple(carry[ch] for ch in chains)

        init = tuple((jnp.full((1, TK), NEG_BIG, F32), jnp.zeros((MLA_V_DIM + ONES_ROWS, TK), F32))
                     for _ in chains)
        d0 = PIECES * qi
        carry = dict(zip(chains, lax.fori_loop(0, d0 // 2, pair, init)))
        for d in range(PIECES):
            if d + 1 < PIECES:
                carry = stage(carry, d0 + d, d % 2, d, True, qstart, d0 + d + 1, d + 1)
            else:
                carry = stage(carry, d0 + d, d % 2, d, True, jnp.minimum(qi + 1, nq - 1) * PIECES, 0, 0)
        for hh, c in chains:
            acc = carry[hh, c][1]
            o_ref[0, qi, hh * MLA_V_DIM:(hh + 1) * MLA_V_DIM, c * TK:(c + 1) * TK] = (
                acc[:MLA_V_DIM] / acc[MLA_V_DIM:MLA_V_DIM + 1]).astype(BF16)
        return 0

    for hh, c in chains:
        score_piece(0, 0, 0, hh, c)
    lax.fori_loop(0, nq, tile, 0)


def _mla_attention(qt, k, vt):
    B, S, _ = k.shape
    nkb = S // TK
    nq = S // TQ
    hp = HEADS_PER_STEP
    return pl.pallas_call(
        _attn_body,
        grid=(B, MLA_HEADS // hp),
        in_specs=[pl.BlockSpec((1, nkb, hp * HEAD_SLAB, TK), lambda b, g: (b, 0, g, 0)),
                  pl.BlockSpec((1, S, hp * HEAD_SLAB), lambda b, g: (b, 0, g)),
                  pl.BlockSpec((1, nkb, hp * MLA_V_DIM, TK), lambda b, g: (b, 0, g, 0))],
        out_specs=pl.BlockSpec((1, nq, hp * MLA_V_DIM, TQ), lambda b, g: (b, 0, g, 0)),
        out_shape=jax.ShapeDtypeStruct((B, nq, MLA_WIDTH, TQ), BF16),
        scratch_shapes=[pltpu.VMEM((TK, TQ), F32)] * (2 * hp) + [pltpu.VMEM((1, TQ), F32)] * (2 * hp),
        compiler_params=pltpu.CompilerParams(dimension_semantics=("arbitrary", "arbitrary"),
                                             vmem_limit_bytes=VMEM_LIMIT),
        name="mla_attn",
    )(qt, k, vt)


def _post_body(ret_ref, mla_ref, x_ref, p_ref, wo_ref, pmn_ref, pfn_ref, wg_ref, wu_ref, wd_ref,
               pofn_ref, wpp_ref, plen_ref, wpg_ref, bpg_ref, o_ref, act_ref):
    tm = x_ref.shape[1]
    halves = [slice(0, tm // 2), slice(tm // 2, tm)]
    mix = [_dot(ret_ref[0, r], wo_ref[0:RET_WIDTH, :]) + _dot_tn(mla_ref[0, 0][:, r], wo_ref[RET_WIDTH:, :])
           for r in halves]
    h1 = [x_ref[0, r] + _rms(m, pmn_ref[...]) for r, m in zip(halves, mix)]
    hn = [_rms(h, pfn_ref[...]).astype(BF16) for h in h1]
    for c in range(D_FF // FF_CHUNK):
        cols = slice(c * FF_CHUNK, (c + 1) * FF_CHUNK)
        for r, hnr in zip(halves, hn):
            g = _dot(hnr, wg_ref[:, cols])
            u = _dot(hnr, wu_ref[:, cols])
            act_ref[r, cols] = (g * jax.nn.sigmoid(g) * u).astype(BF16)
    ff = [_dot(act_ref[r, :], wd_ref[...]) for r in halves]
    h2 = [h + _rms(f, pofn_ref[...]) for h, f in zip(h1, ff)]
    e = [_rms(_dot(p_ref[0, r].astype(BF16), wpp_ref[...]), plen_ref[...]) for r in halves]
    gate = [jax.nn.sigmoid(_dot(h.astype(BF16), wpg_ref[...]) + bpg_ref[...]) for h in h2]
    for r, h, ee, gg in zip(halves, h2, e, gate):
        o_ref[0, r] = h + ee * gg


def _post(ret_out, mla_out, x, p, w_o, post_mix_norm, pre_ffn_norm, w_gate, w_up, w_down,
          post_ffn_norm, w_ple_proj, ple_norm, w_ple_gate, b_ple_gate):
    B, S, D = x.shape
    tm = TM_POST
    const = lambda shape: pl.BlockSpec(shape, lambda b, i: (0,) * len(shape),
                                       pipeline_mode=pl.Buffered(1))
    tile = lambda w: pl.BlockSpec((1, tm, w), lambda b, i: (b, i, 0))
    return pl.pallas_call(
        _post_body,
        grid=(B, S // tm),
        in_specs=[tile(RET_WIDTH),
                  pl.BlockSpec((1, 1, MLA_WIDTH, tm), lambda b, i: (b, i // (TQ // tm), 0, i % (TQ // tm))),
                  tile(D), tile(PLE_DIM),
                  const((D, D)), const((1, D)), const((1, D)),
                  const((D, D_FF)), const((D, D_FF)), const((D_FF, D)), const((1, D)),
                  const((PLE_DIM, D)), const((1, D)), const((D, D)), const((1, D))],
        out_specs=tile(D),
        out_shape=jax.ShapeDtypeStruct((B, S, D), F32),
        scratch_shapes=[pltpu.VMEM((tm, D_FF), BF16)],
        compiler_params=pltpu.CompilerParams(dimension_semantics=("arbitrary", "arbitrary"),
                                             vmem_limit_bytes=VMEM_LIMIT),
        name="post",
    )(ret_out, mla_out, x, p, w_o, post_mix_norm, pre_ffn_norm, w_gate, w_up, w_down,
      post_ffn_norm, w_ple_proj, ple_norm, w_ple_gate, b_ple_gate)


def _pad_heads(w, per_head, keep):
    K = w.shape[0]
    w = w.reshape(K, MLA_HEADS, per_head)[:, :, :keep]
    w = jnp.pad(w, ((0, 0), (0, 0), (0, HEAD_SLAB - keep)))
    return w.reshape(K, MLA_HEADS * HEAD_SLAB).astype(BF16)


def kernel(x, p, positions, pre_mix_norm, w_in, ret_gn_w, mla_q_norm, w_uq, mla_kv_norm, w_ukv, w_o,
           post_mix_norm, pre_ffn_norm, w_gate, w_up, w_down, post_ffn_norm, w_ple_proj, ple_norm,
           w_ple_gate, b_ple_gate):
    depth = pre_mix_norm.shape[0]
    tables = _rope_tables(positions)
    h = x
    for i in range(depth):
        wuq_p = _pad_heads(w_uq[i], MLA_QK_DIM, MLA_QK_DIM).T
        wuk_p = _pad_heads(w_ukv[i], MLA_NOPE_DIM + MLA_V_DIM, MLA_NOPE_DIM)
        wv = w_ukv[i].reshape(MLA_KV_LORA, MLA_HEADS, MLA_NOPE_DIM + MLA_V_DIM)[:, :, MLA_NOPE_DIM:]
        wvt = wv.reshape(MLA_KV_LORA, MLA_WIDTH).T.astype(BF16)
        ret_out, q, k, vt = _in_proj(h, pre_mix_norm[i][None], w_in[i], mla_q_norm[i][None],
                                     wuq_p, mla_kv_norm[i][None], wuk_p, wvt, tables, ret_gn_w[i][None])
        mla_out = _mla_attention(q, k, vt)
        h = _post(ret_out, mla_out, h, p[i], w_o[i].astype(BF16), post_mix_norm[i][None],
                  pre_ffn_norm[i][None], w_gate[i].astype(BF16), w_up[i].astype(BF16),
                  w_down[i].astype(BF16), post_ffn_norm[i][None], w_ple_proj[i].astype(BF16),
                  ple_norm[i][None], w_ple_gate[i].astype(BF16), b_ple_gate[i][None])
    return h
```

```python
import math
from functools import partial

import jax
import jax.numpy as jnp
from jax import lax
from jax.experimental import pallas as pl
from jax.experimental.pallas import tpu as pltpu

D_MODEL = 1024
PLE_DIM = 256
RET_HEADS = 4
RET_HEAD_DIM = 128
RET_WIDTH = RET_HEADS * RET_HEAD_DIM
MLA_HEADS = 8
MLA_NOPE_DIM = 64
MLA_ROPE_DIM = 32
MLA_QK_DIM = MLA_NOPE_DIM + MLA_ROPE_DIM
MLA_V_DIM = 64
MLA_WIDTH = MLA_HEADS * MLA_V_DIM
MLA_Q_LORA = 384
MLA_KV_LORA = 256
D_FF = 2816
ROPE_BASE = 10000.0
EPS = 1e-6
NEG_BIG = -1e30

LANES = 128
HEAD_SLAB = 128
IN_COLS_CQ_END = 4 * RET_WIDTH + MLA_Q_LORA
IN_COLS_MAIN = IN_COLS_CQ_END + MLA_KV_LORA
IN_COLS_ALL = IN_COLS_MAIN + MLA_ROPE_DIM
VMEM_LIMIT = 58 * 1024 * 1024

TM_PROJ = 512
RET_CHUNK = 256
TQ = 2048
TK = 512
ONES_ROWS = 16
TM_POST = 1024
FF_CHUNK = 256

BF16 = jnp.bfloat16
F32 = jnp.float32


def _dot(a, b):
    return jnp.dot(a, b, preferred_element_type=F32)


def _dot_nt(a, b):
    return lax.dot_general(a, b, (((1,), (1,)), ((), ())), preferred_element_type=F32)


def _dot_tn(a, b):
    return lax.dot_general(a, b, (((0,), (0,)), ((), ())), preferred_element_type=F32)


def _rms(v, w):
    return v * lax.rsqrt(jnp.mean(v * v, axis=-1, keepdims=True) + EPS) * w


TM_ROPE = 1024


def _rope_tables_body(pos_ref, invr_ref, invm_ref, *refs, n_cast):
    w_refs, refs = refs[:n_cast], refs[n_cast:]
    cr_ref, sr_ref, cm_ref, sm_ref, c16_ref, s16_ref = refs[:6]
    for w_ref, o_ref in zip(w_refs, refs[6:]):
        o_ref[...] = w_ref[...].astype(BF16)
    pos = pos_ref[0].astype(F32)
    ang_r = invr_ref[...] * pos
    c = jnp.cos(ang_r)
    s = jnp.sin(ang_r)
    ang_m = invm_ref[...] * pos
    c16 = jnp.cos(ang_m)
    s16 = jnp.sin(ang_m)
    tm = pos.shape[1]
    pad = HEAD_SLAB - MLA_QK_DIM
    cr_ref[0] = jnp.concatenate([c, c], axis=0).T
    sr_ref[0] = jnp.concatenate([-s, s], axis=0).T
    cm_ref[0] = jnp.concatenate([jnp.ones((MLA_NOPE_DIM, tm), F32), c16, c16, jnp.zeros((pad, tm), F32)], axis=0).T
    sm_ref[0] = jnp.concatenate([jnp.zeros((MLA_NOPE_DIM, tm), F32), -s16, s16, jnp.zeros((pad, tm), F32)], axis=0).T
    c16_ref[0] = c16
    s16_ref[0] = s16


def _rope_tables(positions, weights):
    B, S = positions.shape
    tm = TM_ROPE
    steps = B * (S // tm)
    for w in weights:
        assert w.shape[0] % (steps * 16) == 0, "row slab per grid step must be whole bf16 sublane tiles"
    slab = lambda w: pl.BlockSpec((w.shape[0] // steps, w.shape[1]), lambda b, i: (b * (S // tm) + i, 0))
    half_r = RET_HEAD_DIM // 2
    inv_r = 1.0 / (ROPE_BASE ** (jnp.arange(half_r, dtype=F32) / half_r))
    half_m = MLA_ROPE_DIM // 2
    inv_m = 1.0 / (ROPE_BASE ** (jnp.arange(half_m, dtype=F32) / half_m))
    inv_r = jnp.broadcast_to(inv_r[:, None], (half_r, tm))
    inv_m = jnp.broadcast_to(inv_m[:, None], (half_m, tm))
    tab = pl.BlockSpec((1, tm, LANES), lambda b, i: (b, i, 0))
    out = jax.ShapeDtypeStruct((B, S, LANES), F32)
    tab_t = pl.BlockSpec((1, half_m, tm), lambda b, i: (b, 0, i))
    out_t = jax.ShapeDtypeStruct((B, half_m, S), F32)
    res = pl.pallas_call(
        partial(_rope_tables_body, n_cast=len(weights)),
        grid=(B, S // tm),
        in_specs=[pl.BlockSpec((1, 1, tm), lambda b, i: (b, 0, i)),
                  pl.BlockSpec((half_r, tm), lambda b, i: (0, 0)),
                  pl.BlockSpec((half_m, tm), lambda b, i: (0, 0))] + [slab(w) for w in weights],
        out_specs=[tab, tab, tab, tab, tab_t, tab_t] + [slab(w) for w in weights],
        out_shape=[out, out, out, out, out_t, out_t] + [jax.ShapeDtypeStruct(w.shape, BF16) for w in weights],
        compiler_params=pltpu.CompilerParams(dimension_semantics=("arbitrary", "arbitrary"),
                                             vmem_limit_bytes=VMEM_LIMIT),
        name="rope_tables",
    )(positions.reshape(B, 1, S), inv_r, inv_m, *weights)
    return res[:6], res[6:]


def _in_proj_body(x_ref, g_ref, win32_ref, place_ref, qn_ref, wuq_ref, kvn_ref, wuk_ref, wvt_ref,
                  cr_ref, sr_ref, cm_ref, sm_ref, c16_ref, s16_ref, dmask_ref, zeta_ref, xi_ref, gnw_ref,
                  ret_ref, qt_ref, k_ref, vt_ref, state_ref, win_ref, *, q_scale, g_chunk):
    @pl.when((pl.program_id(0) == 0) & (pl.program_id(1) == 0))
    def _():
        a, b = IN_COLS_CQ_END, IN_COLS_MAIN
        win_ref[:, 0:a] = win32_ref[:, 0:a].astype(BF16)
        kr_w = win32_ref[:, b:b + MLA_ROPE_DIM].astype(BF16)
        win_ref[:, a:a + LANES] = _dot(kr_w, place_ref[...]).astype(BF16)
        win_ref[:, a + LANES:] = win32_ref[:, a:b].astype(BF16)

    @pl.when(pl.program_id(1) == 0)
    def _():
        state_ref[...] = jnp.zeros_like(state_ref)

    xn = _rms(x_ref[0], g_ref[...]).astype(BF16)
    cr = cr_ref[0]
    sr = sr_ref[0]

    def ret_rope(v):
        return v * cr + pltpu.roll(v, RET_HEAD_DIM // 2, 1) * sr

    pq = _dot(xn, win_ref[:, 0:RET_WIDTH])
    pk = _dot(xn, win_ref[:, RET_WIDTH:2 * RET_WIDTH])
    pv = _dot(xn, win_ref[:, 2 * RET_WIDTH:3 * RET_WIDTH]).astype(BF16)
    pg = _dot(xn, win_ref[:, 3 * RET_WIDTH:4 * RET_WIDTH])

    C = RET_CHUNK
    head_cols = [slice(h * RET_HEAD_DIM, (h + 1) * RET_HEAD_DIM) for h in range(RET_HEADS)]
    rq = [ret_rope(pq[:, cols]).astype(BF16) for cols in head_cols]
    rk = [(ret_rope(pk[:, cols]) * (RET_HEAD_DIM ** -0.5)).astype(BF16) for cols in head_cols]

    def ret_chunk(c):
        rows = slice(c * C, (c + 1) * C)
        for h, cols in enumerate(head_cols):
            q = rq[h][rows]
            k = rk[h][rows]
            v = pv[rows, cols]
            scores = _dot_nt(q, k) * dmask_ref[h]
            inner = _dot(scores.astype(BF16), v)
            r_prev = state_ref[h]
            cross = _dot(q, r_prev.astype(BF16)) * xi_ref[h]
            kz = (k.astype(F32) * zeta_ref[h]).astype(BF16)
            state_ref[h] = g_chunk[h] * r_prev + _dot_tn(kz, v)
            y = inner + cross
            mu = jnp.mean(y, axis=-1, keepdims=True)
            yc = y - mu
            var = jnp.mean(yc * yc, axis=-1, keepdims=True)
            yn = yc * lax.rsqrt(var + EPS) * gnw_ref[:, cols]
            gate = pg[rows, cols]
            ret_ref[0, rows, cols] = (gate * jax.nn.sigmoid(gate) * yn).astype(BF16)

    cm = cm_ref[0]
    sm = sm_ref[0]
    lane = lax.broadcasted_iota(jnp.int32, cm.shape, 1)
    upper = lane >= (MLA_NOPE_DIM + MLA_ROPE_DIM // 2)

    def mla_rope(v):
        swapped = jnp.where(upper, pltpu.roll(v, MLA_ROPE_DIM // 2, 1),
                            pltpu.roll(v, LANES - MLA_ROPE_DIM // 2, 1))
        return v * cm + swapped * sm

    for c in range(TM_PROJ // C):
        ret_chunk(c)
    o = 4 * RET_WIDTH
    cq = _dot(xn, win_ref[:, o:IN_COLS_CQ_END])
    cqn = _rms(cq, qn_ref[...]).astype(BF16)
    qt = _dot_nt(wuq_ref[...], cqn)
    c16 = c16_ref[0]
    s16 = s16_ref[0]
    half = MLA_ROPE_DIM // 2
    for h in range(MLA_HEADS):
        r0 = h * HEAD_SLAB
        x1 = qt[r0 + MLA_NOPE_DIM:r0 + MLA_NOPE_DIM + half]
        x2 = qt[r0 + MLA_NOPE_DIM + half:r0 + MLA_QK_DIM]
        slab = jnp.concatenate([qt[r0:r0 + MLA_NOPE_DIM], x1 * c16 - x2 * s16, x2 * c16 + x1 * s16,
                                qt[r0 + MLA_QK_DIM:r0 + HEAD_SLAB]], axis=0)
        qt_ref[0, 0, r0:r0 + HEAD_SLAB, :] = (slab * q_scale).astype(BF16)

    ckv = _dot(xn, win_ref[:, IN_COLS_CQ_END + LANES:])
    ckvn = _rms(ckv, kvn_ref[...]).astype(BF16)
    kr = mla_rope(_dot(xn, win_ref[:, IN_COLS_CQ_END:IN_COLS_CQ_END + LANES]))
    kn = _dot(ckvn, wuk_ref[...])
    for h in range(MLA_HEADS):
        sl = slice(h * HEAD_SLAB, (h + 1) * HEAD_SLAB)
        k_ref[0, :, sl] = (kn[:, sl] + kr).astype(BF16)
    vt = _dot_nt(wvt_ref[...], ckvn).astype(BF16)
    for blk in range(TM_PROJ // TK):
        vt_ref[0, blk] = vt[:, blk * TK:(blk + 1) * TK]


def _in_proj(x, pre_mix_norm, w_in, q_norm, wuq_p, kv_norm, wuk_p, wvt, tables, ret_gn_w):
    B, S, D = x.shape
    tm = TM_PROJ
    place = (jnp.arange(MLA_ROPE_DIM)[:, None] + MLA_NOPE_DIM == jnp.arange(LANES)[None, :]).astype(BF16)
    assert tm == TK, "attention reads one transposed query block per key-block-sized piece"
    cr, sr, cm, sm, c16, s16 = tables
    q_scale = (1.0 / math.sqrt(MLA_QK_DIM)) * math.log2(math.e)
    C = RET_CHUNK
    H = RET_HEADS
    log_g = jnp.log(1.0 - 2.0 ** (-5.0 - jnp.arange(H, dtype=F32)))
    j = jnp.arange(C, dtype=F32)
    diff = j[:, None] - j[None, :]
    dmask = jnp.where(diff[None] >= 0, jnp.exp(jnp.maximum(diff, 0.0)[None] * log_g[:, None, None]), 0.0)
    zeta = jnp.exp((C - 1 - j)[None, :] * log_g[:, None])
    xi = jnp.exp((j + 1)[None, :] * log_g[:, None])
    zeta_b = jnp.broadcast_to(zeta[:, :, None], (H, C, RET_HEAD_DIM))
    xi_b = jnp.broadcast_to(xi[:, :, None], (H, C, RET_HEAD_DIM))
    g_chunk = tuple(float((1.0 - 2.0 ** (-5.0 - h)) ** C) for h in range(H))
    const = lambda shape: pl.BlockSpec(shape, lambda b, i: (0,) * len(shape))
    tile = lambda w: pl.BlockSpec((1, tm, w), lambda b, i: (b, i, 0))
    bf = lambda w: jax.ShapeDtypeStruct((B, S, w), BF16)
    return pl.pallas_call(
        partial(_in_proj_body, q_scale=q_scale, g_chunk=g_chunk),
        grid=(B, S // tm),
        in_specs=[tile(D), const((1, D)),
                  pl.BlockSpec((D, IN_COLS_ALL), lambda b, i: (0, 0), pipeline_mode=pl.Buffered(1)),
                  const((MLA_ROPE_DIM, LANES)),
                  const((1, MLA_Q_LORA)), const((MLA_HEADS * HEAD_SLAB, MLA_Q_LORA)),
                  const((1, MLA_KV_LORA)), const((MLA_KV_LORA, MLA_HEADS * HEAD_SLAB)),
                  const((MLA_WIDTH, MLA_KV_LORA)),
                  tile(LANES), tile(LANES), tile(LANES), tile(LANES),
                  pl.BlockSpec((1, MLA_ROPE_DIM // 2, tm), lambda b, i: (b, 0, i)),
                  pl.BlockSpec((1, MLA_ROPE_DIM // 2, tm), lambda b, i: (b, 0, i)),
                  const((H, C, C)), const((H, C, RET_HEAD_DIM)), const((H, C, RET_HEAD_DIM)),
                  const((1, RET_WIDTH))],
        out_specs=[tile(RET_WIDTH),
                   pl.BlockSpec((1, 1, MLA_HEADS * HEAD_SLAB, tm), lambda b, i: (b, i, 0, 0)),
                   tile(MLA_HEADS * HEAD_SLAB),
                   pl.BlockSpec((1, tm // TK, MLA_WIDTH, TK), lambda b, i: (b, i, 0, 0))],
        out_shape=[bf(RET_WIDTH),
                   jax.ShapeDtypeStruct((B, S // tm, MLA_HEADS * HEAD_SLAB, tm), BF16),
                   bf(MLA_HEADS * HEAD_SLAB),
                   jax.ShapeDtypeStruct((B, S // TK, MLA_WIDTH, TK), BF16)],
        scratch_shapes=[pltpu.VMEM((H, RET_HEAD_DIM, RET_HEAD_DIM), F32),
                        pltpu.VMEM((D, IN_COLS_MAIN + LANES), BF16)],
        compiler_params=pltpu.CompilerParams(dimension_semantics=("arbitrary", "arbitrary"),
                                             vmem_limit_bytes=VMEM_LIMIT),
        name="in_proj",
    )(x, pre_mix_norm, w_in, place, q_norm, wuq_p, kv_norm, wuk_p, wvt, cr, sr, cm, sm, c16, s16,
      dmask, zeta_b, xi_b, ret_gn_w)


HEADS_PER_STEP = 2
PIECES = TQ // TK


def _attn_body(qt_ref, k_ref, vt_ref, o_ref, *scratch):
    n = 2 * HEADS_PER_STEP
    s_ref = {(i // HEADS_PER_STEP, i % HEADS_PER_STEP): scratch[i] for i in range(n)}
    mb_ref = {(i // HEADS_PER_STEP, i % HEADS_PER_STEP): scratch[n + i] for i in range(n)}
    nq = qt_ref.shape[1] // PIECES
    ones = jnp.ones((ONES_ROWS, TK), BF16)
    chains = [(hh, c) for c in range(PIECES) for hh in range(HEADS_PER_STEP)]
    row = lax.broadcasted_iota(jnp.int32, (TK, TK), 0)
    col = lax.broadcasted_iota(jnp.int32, (TK, TK), 1)
    tri = row <= col

    def score_piece(qstart, j, slot, hh, c):
        kstart = pl.multiple_of(j * TK, TK)
        cols = slice(hh * HEAD_SLAB, (hh + 1) * HEAD_SLAB)
        lanes = slice(c * TK, (c + 1) * TK)
        st = _dot(k_ref[0, pl.ds(kstart, TK), cols], qt_ref[0, qstart + c, cols, :])
        s_ref[slot, hh][:, lanes] = st
        mb_ref[slot, hh][:, lanes] = jnp.max(st, axis=0, keepdims=True)

    def acc_piece(j, slot, hh, c, state, masked):
        m, acc = state
        lanes = slice(c * TK, (c + 1) * TK)
        st = s_ref[slot, hh][:, lanes]
        if masked:
            st = jnp.where(tri, st, NEG_BIG)
            blk_max = jnp.max(st, axis=0, keepdims=True)
        else:
            blk_max = mb_ref[slot, hh][:, lanes]
        m_new = jnp.maximum(m, blk_max)
        p = jnp.exp2(st - m_new).astype(BF16)
        vte = jnp.concatenate([vt_ref[0, j, hh * MLA_V_DIM:(hh + 1) * MLA_V_DIM, :], ones], axis=0)
        return m_new, acc * jnp.exp2(m - m_new) + _dot(vte, p)

    def stage(carry, j, slot, first, diag, nxt_qstart, nxt_j, nxt_first):
        carry = dict(carry)
        for hh, c in chains:
            if c >= nxt_first:
                score_piece(nxt_qstart, nxt_j, 1 - slot, hh, c)
            if c >= first:
                carry[hh, c] = acc_piece(j, slot, hh, c, carry[hh, c], diag and c == first)
        return carry

    def tile(qi, _):
        qstart = qi * PIECES

        def pair(t, flat):
            carry = dict(zip(chains, flat))
            carry = stage(carry, 2 * t, 0, 0, False, qstart, 2 * t + 1, 0)
            carry = stage(carry, 2 * t + 1, 1, 0, False, qstart, 2 * t + 2, 0)
            return tuple(carry[ch] for ch in chains)

        init = tuple((jnp.full((1, TK), NEG_BIG, F32), jnp.zeros((MLA_V_DIM + ONES_ROWS, TK), F32))
                     for _ in chains)
        d0 = PIECES * qi
        carry = dict(zip(chains, lax.fori_loop(0, d0 // 2, pair, init)))
        for d in range(PIECES):
            if d + 1 < PIECES:
                carry = stage(carry, d0 + d, d % 2, d, True, qstart, d0 + d + 1, d + 1)
            else:
                carry = stage(carry, d0 + d, d % 2, d, True, jnp.minimum(qi + 1, nq - 1) * PIECES, 0, 0)
        for hh, c in chains:
            acc = carry[hh, c][1]
            o_ref[0, qi, hh * MLA_V_DIM:(hh + 1) * MLA_V_DIM, c * TK:(c + 1) * TK] = (
                acc[:MLA_V_DIM] / acc[MLA_V_DIM:MLA_V_DIM + 1]).astype(BF16)
        return 0

    for hh, c in chains:
        score_piece(0, 0, 0, hh, c)
    lax.fori_loop(0, nq, tile, 0)


def _mla_attention(qt, k, vt):
    B, S, _ = k.shape
    nkb = S // TK
    nq = S // TQ
    hp = HEADS_PER_STEP
    return pl.pallas_call(
        _attn_body,
        grid=(B, MLA_HEADS // hp),
        in_specs=[pl.BlockSpec((1, nkb, hp * HEAD_SLAB, TK), lambda b, g: (b, 0, g, 0)),
                  pl.BlockSpec((1, S, hp * HEAD_SLAB), lambda b, g: (b, 0, g)),
                  pl.BlockSpec((1, nkb, hp * MLA_V_DIM, TK), lambda b, g: (b, 0, g, 0))],
        out_specs=pl.BlockSpec((1, nq, hp * MLA_V_DIM, TQ), lambda b, g: (b, 0, g, 0)),
        out_shape=jax.ShapeDtypeStruct((B, nq, MLA_WIDTH, TQ), BF16),
        scratch_shapes=[pltpu.VMEM((TK, TQ), F32)] * (2 * hp) + [pltpu.VMEM((1, TQ), F32)] * (2 * hp),
        compiler_params=pltpu.CompilerParams(dimension_semantics=("arbitrary", "arbitrary"),
                                             vmem_limit_bytes=VMEM_LIMIT),
        name="mla_attn",
    )(qt, k, vt)


def _post_body(ret_ref, mla_ref, x_ref, p_ref, wo_ref, pmn_ref, pfn_ref, wg_ref, wu_ref, wd_ref,
               pofn_ref, wpp_ref, plen_ref, wpg_ref, bpg_ref, o_ref, act_ref):
    tm = x_ref.shape[1]
    halves = [slice(0, tm // 2), slice(tm // 2, tm)]
    mix = [_dot(ret_ref[0, r], wo_ref[0:RET_WIDTH, :]) + _dot_tn(mla_ref[0, 0][:, r], wo_ref[RET_WIDTH:, :])
           for r in halves]
    h1 = [x_ref[0, r] + _rms(m, pmn_ref[...]) for r, m in zip(halves, mix)]
    hn = [_rms(h, pfn_ref[...]).astype(BF16) for h in h1]
    for c in range(D_FF // FF_CHUNK):
        cols = slice(c * FF_CHUNK, (c + 1) * FF_CHUNK)
        for r, hnr in zip(halves, hn):
            g = _dot(hnr, wg_ref[:, cols])
            u = _dot(hnr, wu_ref[:, cols])
            act_ref[r, cols] = (g * jax.nn.sigmoid(g) * u).astype(BF16)
    ff = [_dot(act_ref[r, :], wd_ref[...]) for r in halves]
    h2 = [h + _rms(f, pofn_ref[...]) for h, f in zip(h1, ff)]
    e = [_rms(_dot(p_ref[0, r].astype(BF16), wpp_ref[...]), plen_ref[...]) for r in halves]
    gate = [jax.nn.sigmoid(_dot(h.astype(BF16), wpg_ref[...]) + bpg_ref[...]) for h in h2]
    for r, h, ee, gg in zip(halves, h2, e, gate):
        o_ref[0, r] = h + ee * gg


def _post(ret_out, mla_out, x, p, w_o, post_mix_norm, pre_ffn_norm, w_gate, w_up, w_down,
          post_ffn_norm, w_ple_proj, ple_norm, w_ple_gate, b_ple_gate):
    B, S, D = x.shape
    tm = TM_POST
    const = lambda shape: pl.BlockSpec(shape, lambda b, i: (0,) * len(shape),
                                       pipeline_mode=pl.Buffered(1))
    tile = lambda w: pl.BlockSpec((1, tm, w), lambda b, i: (b, i, 0))
    return pl.pallas_call(
        _post_body,
        grid=(B, S // tm),
        in_specs=[tile(RET_WIDTH),
                  pl.BlockSpec((1, 1, MLA_WIDTH, tm), lambda b, i: (b, i // (TQ // tm), 0, i % (TQ // tm))),
                  tile(D), tile(PLE_DIM),
                  const((D, D)), const((1, D)), const((1, D)),
                  const((D, D_FF)), const((D, D_FF)), const((D_FF, D)), const((1, D)),
                  const((PLE_DIM, D)), const((1, D)), const((D, D)), const((1, D))],
        out_specs=tile(D),
        out_shape=jax.ShapeDtypeStruct((B, S, D), F32),
        scratch_shapes=[pltpu.VMEM((tm, D_FF), BF16)],
        compiler_params=pltpu.CompilerParams(dimension_semantics=("arbitrary", "arbitrary"),
                                             vmem_limit_bytes=VMEM_LIMIT),
        name="post",
    )(ret_out, mla_out, x, p, w_o, post_mix_norm, pre_ffn_norm, w_gate, w_up, w_down,
      post_ffn_norm, w_ple_proj, ple_norm, w_ple_gate, b_ple_gate)


def _pad_heads(w, per_head, keep):
    K = w.shape[0]
    w = w.reshape(K, MLA_HEADS, per_head)[:, :, :keep]
    w = jnp.pad(w, ((0, 0), (0, 0), (0, HEAD_SLAB - keep)))
    return w.reshape(K, MLA_HEADS * HEAD_SLAB).astype(BF16)


def kernel(x, p, positions, pre_mix_norm, w_in, ret_gn_w, mla_q_norm, w_uq, mla_kv_norm, w_ukv, w_o,
           post_mix_norm, pre_ffn_norm, w_gate, w_up, w_down, post_ffn_norm, w_ple_proj, ple_norm,
           w_ple_gate, b_ple_gate):
    depth = pre_mix_norm.shape[0]
    h = x
    tables = None
    for i in range(depth):
        big = (w_o[i], w_gate[i], w_up[i], w_down[i], w_ple_proj[i], w_ple_gate[i])
        if tables is None:
            tables, big = _rope_tables(positions, big)
        else:
            big = tuple(w.astype(BF16) for w in big)
        wo_b, wg_b, wu_b, wd_b, wpp_b, wpg_b = big
        wuq_p = _pad_heads(w_uq[i], MLA_QK_DIM, MLA_QK_DIM).T
        wuk_p = _pad_heads(w_ukv[i], MLA_NOPE_DIM + MLA_V_DIM, MLA_NOPE_DIM)
        wv = w_ukv[i].reshape(MLA_KV_LORA, MLA_HEADS, MLA_NOPE_DIM + MLA_V_DIM)[:, :, MLA_NOPE_DIM:]
        wvt = wv.reshape(MLA_KV_LORA, MLA_WIDTH).T.astype(BF16)
        ret_out, q, k, vt = _in_proj(h, pre_mix_norm[i][None], w_in[i], mla_q_norm[i][None],
                                     wuq_p, mla_kv_norm[i][None], wuk_p, wvt, tables, ret_gn_w[i][None])
        mla_out = _mla_attention(q, k, vt)
        h = _post(ret_out, mla_out, h, p[i], wo_b, post_mix_norm[i][None], pre_ffn_norm[i][None],
                  wg_b, wu_b, wd_b, post_ffn_norm[i][None], wpp_b, ple_norm[i][None], wpg_b,
                  b_ple_gate[i][None])
    return h
```

```python
import math
from functools import partial

import jax
import jax.numpy as jnp
from jax import lax
from jax.experimental import pallas as pl
from jax.experimental.pallas import tpu as pltpu

D_MODEL = 1024
PLE_DIM = 256
RET_HEADS = 4
RET_HEAD_DIM = 128
RET_WIDTH = RET_HEADS * RET_HEAD_DIM
MLA_HEADS = 8
MLA_NOPE_DIM = 64
MLA_ROPE_DIM = 32
MLA_QK_DIM = MLA_NOPE_DIM + MLA_ROPE_DIM
MLA_V_DIM = 64
MLA_WIDTH = MLA_HEADS * MLA_V_DIM
MLA_Q_LORA = 384
MLA_KV_LORA = 256
D_FF = 2816
ROPE_BASE = 10000.0
EPS = 1e-6
NEG_BIG = -1e30

LANES = 128
HEAD_SLAB = 128
IN_COLS_CQ_END = 4 * RET_WIDTH + MLA_Q_LORA
IN_COLS_MAIN = IN_COLS_CQ_END + MLA_KV_LORA
IN_COLS_ALL = IN_COLS_MAIN + MLA_ROPE_DIM
VMEM_LIMIT = 58 * 1024 * 1024

TM_PROJ = 512
RET_CHUNK = 256
TQ = 2048
TK = 512
ONES_ROWS = 16
TM_POST = 1024
FF_CHUNK = 256

BF16 = jnp.bfloat16
F32 = jnp.float32


def _dot(a, b):
    return jnp.dot(a, b, preferred_element_type=F32)


def _dot_nt(a, b):
    return lax.dot_general(a, b, (((1,), (1,)), ((), ())), preferred_element_type=F32)


def _dot_tn(a, b):
    return lax.dot_general(a, b, (((0,), (0,)), ((), ())), preferred_element_type=F32)


def _rms(v, w):
    return v * lax.rsqrt(jnp.mean(v * v, axis=-1, keepdims=True) + EPS) * w


TM_ROPE = 1024


def _rope_tables_body(pos_ref, invr_ref, invm_ref, cr_ref, sr_ref, cm_ref, sm_ref, c16_ref, s16_ref):
    pos = pos_ref[0].astype(F32)
    ang_r = invr_ref[...] * pos
    c = jnp.cos(ang_r)
    s = jnp.sin(ang_r)
    ang_m = invm_ref[...] * pos
    c16 = jnp.cos(ang_m)
    s16 = jnp.sin(ang_m)
    tm = pos.shape[1]
    pad = HEAD_SLAB - MLA_QK_DIM
    cr_ref[0] = jnp.concatenate([c, c], axis=0).T
    sr_ref[0] = jnp.concatenate([-s, s], axis=0).T
    cm_ref[0] = jnp.concatenate([jnp.ones((MLA_NOPE_DIM, tm), F32), c16, c16, jnp.zeros((pad, tm), F32)], axis=0).T
    sm_ref[0] = jnp.concatenate([jnp.zeros((MLA_NOPE_DIM, tm), F32), -s16, s16, jnp.zeros((pad, tm), F32)], axis=0).T
    c16_ref[0] = c16
    s16_ref[0] = s16


def _rope_tables(positions):
    B, S = positions.shape
    tm = TM_ROPE
    half_r = RET_HEAD_DIM // 2
    inv_r = 1.0 / (ROPE_BASE ** (jnp.arange(half_r, dtype=F32) / half_r))
    half_m = MLA_ROPE_DIM // 2
    inv_m = 1.0 / (ROPE_BASE ** (jnp.arange(half_m, dtype=F32) / half_m))
    inv_r = jnp.broadcast_to(inv_r[:, None], (half_r, tm))
    inv_m = jnp.broadcast_to(inv_m[:, None], (half_m, tm))
    tab = pl.BlockSpec((1, tm, LANES), lambda b, i: (b, i, 0))
    out = jax.ShapeDtypeStruct((B, S, LANES), F32)
    tab_t = pl.BlockSpec((1, half_m, tm), lambda b, i: (b, 0, i))
    out_t = jax.ShapeDtypeStruct((B, half_m, S), F32)
    return pl.pallas_call(
        _rope_tables_body,
        grid=(B, S // tm),
        in_specs=[pl.BlockSpec((1, 1, tm), lambda b, i: (b, 0, i)),
                  pl.BlockSpec((half_r, tm), lambda b, i: (0, 0)),
                  pl.BlockSpec((half_m, tm), lambda b, i: (0, 0))],
        out_specs=[tab, tab, tab, tab, tab_t, tab_t],
        out_shape=[out, out, out, out, out_t, out_t],
        compiler_params=pltpu.CompilerParams(dimension_semantics=("arbitrary", "arbitrary"),
                                             vmem_limit_bytes=VMEM_LIMIT),
        name="rope_tables",
    )(positions.reshape(B, 1, S), inv_r, inv_m)


N_PROJ_INPUTS = 19
CAST_SLABS = 16


def _in_proj_body(*refs, q_scale, g_chunk, n_cast):
    (x_ref, g_ref, win32_ref, place_ref, qn_ref, wuq_ref, kvn_ref, wuk_ref, wvt_ref,
     cr_ref, sr_ref, cm_ref, sm_ref, c16_ref, s16_ref, dmask_ref, zeta_ref, xi_ref, gnw_ref) = refs[:N_PROJ_INPUTS]
    w32_refs = refs[N_PROJ_INPUTS:N_PROJ_INPUTS + n_cast]
    ret_ref, qt_ref, k_ref, vt_ref = refs[N_PROJ_INPUTS + n_cast:N_PROJ_INPUTS + n_cast + 4]
    w16_refs = refs[N_PROJ_INPUTS + n_cast + 4:N_PROJ_INPUTS + 2 * n_cast + 4]
    state_ref, win_ref = refs[N_PROJ_INPUTS + 2 * n_cast + 4:]

    for w32_ref, w16_ref in zip(w32_refs, w16_refs):
        w16_ref[...] = w32_ref[...].astype(BF16)

    @pl.when((pl.program_id(0) == 0) & (pl.program_id(1) == 0))
    def _():
        a, b = IN_COLS_CQ_END, IN_COLS_MAIN
        win_ref[:, 0:a] = win32_ref[:, 0:a].astype(BF16)
        kr_w = win32_ref[:, b:b + MLA_ROPE_DIM].astype(BF16)
        win_ref[:, a:a + LANES] = _dot(kr_w, place_ref[...]).astype(BF16)
        win_ref[:, a + LANES:] = win32_ref[:, a:b].astype(BF16)

    @pl.when(pl.program_id(1) == 0)
    def _():
        state_ref[...] = jnp.zeros_like(state_ref)

    xn = _rms(x_ref[0], g_ref[...]).astype(BF16)
    cr = cr_ref[0]
    sr = sr_ref[0]

    def ret_rope(v):
        return v * cr + pltpu.roll(v, RET_HEAD_DIM // 2, 1) * sr

    pq = _dot(xn, win_ref[:, 0:RET_WIDTH])
    pk = _dot(xn, win_ref[:, RET_WIDTH:2 * RET_WIDTH])
    pv = _dot(xn, win_ref[:, 2 * RET_WIDTH:3 * RET_WIDTH]).astype(BF16)
    pg = _dot(xn, win_ref[:, 3 * RET_WIDTH:4 * RET_WIDTH])

    C = RET_CHUNK
    head_cols = [slice(h * RET_HEAD_DIM, (h + 1) * RET_HEAD_DIM) for h in range(RET_HEADS)]
    rq = [ret_rope(pq[:, cols]).astype(BF16) for cols in head_cols]
    rk = [(ret_rope(pk[:, cols]) * (RET_HEAD_DIM ** -0.5)).astype(BF16) for cols in head_cols]

    def ret_chunk(c):
        rows = slice(c * C, (c + 1) * C)
        for h, cols in enumerate(head_cols):
            q = rq[h][rows]
            k = rk[h][rows]
            v = pv[rows, cols]
            scores = _dot_nt(q, k) * dmask_ref[h]
            inner = _dot(scores.astype(BF16), v)
            r_prev = state_ref[h]
            cross = _dot(q, r_prev.astype(BF16)) * xi_ref[h]
            kz = (k.astype(F32) * zeta_ref[h]).astype(BF16)
            state_ref[h] = g_chunk[h] * r_prev + _dot_tn(kz, v)
            y = inner + cross
            mu = jnp.mean(y, axis=-1, keepdims=True)
            yc = y - mu
            var = jnp.mean(yc * yc, axis=-1, keepdims=True)
            yn = yc * lax.rsqrt(var + EPS) * gnw_ref[:, cols]
            gate = pg[rows, cols]
            ret_ref[0, rows, cols] = (gate * jax.nn.sigmoid(gate) * yn).astype(BF16)

    cm = cm_ref[0]
    sm = sm_ref[0]
    lane = lax.broadcasted_iota(jnp.int32, cm.shape, 1)
    upper = lane >= (MLA_NOPE_DIM + MLA_ROPE_DIM // 2)

    def mla_rope(v):
        swapped = jnp.where(upper, pltpu.roll(v, MLA_ROPE_DIM // 2, 1),
                            pltpu.roll(v, LANES - MLA_ROPE_DIM // 2, 1))
        return v * cm + swapped * sm

    for c in range(TM_PROJ // C):
        ret_chunk(c)
    o = 4 * RET_WIDTH
    cq = _dot(xn, win_ref[:, o:IN_COLS_CQ_END])
    cqn = _rms(cq, qn_ref[...]).astype(BF16)
    qt = _dot_nt(wuq_ref[...], cqn)
    c16 = c16_ref[0]
    s16 = s16_ref[0]
    half = MLA_ROPE_DIM // 2
    for h in range(MLA_HEADS):
        r0 = h * HEAD_SLAB
        x1 = qt[r0 + MLA_NOPE_DIM:r0 + MLA_NOPE_DIM + half]
        x2 = qt[r0 + MLA_NOPE_DIM + half:r0 + MLA_QK_DIM]
        slab = jnp.concatenate([qt[r0:r0 + MLA_NOPE_DIM], x1 * c16 - x2 * s16, x2 * c16 + x1 * s16,
                                qt[r0 + MLA_QK_DIM:r0 + HEAD_SLAB]], axis=0)
        qt_ref[0, 0, r0:r0 + HEAD_SLAB, :] = (slab * q_scale).astype(BF16)

    ckv = _dot(xn, win_ref[:, IN_COLS_CQ_END + LANES:])
    ckvn = _rms(ckv, kvn_ref[...]).astype(BF16)
    kr = mla_rope(_dot(xn, win_ref[:, IN_COLS_CQ_END:IN_COLS_CQ_END + LANES]))
    kn = _dot(ckvn, wuk_ref[...])
    for h in range(MLA_HEADS):
        sl = slice(h * HEAD_SLAB, (h + 1) * HEAD_SLAB)
        k_ref[0, :, sl] = (kn[:, sl] + kr).astype(BF16)
    vt = _dot_nt(wvt_ref[...], ckvn).astype(BF16)
    for blk in range(TM_PROJ // TK):
        vt_ref[0, blk] = vt[:, blk * TK:(blk + 1) * TK]


def _in_proj(x, pre_mix_norm, w_in, q_norm, wuq_p, kv_norm, wuk_p, wvt, tables, ret_gn_w, cast_weights):
    B, S, D = x.shape
    tm = TM_PROJ
    nt = S // tm
    assert (B * nt) % CAST_SLABS == 0
    per = B * nt // CAST_SLABS
    for w in cast_weights:
        assert w.shape[0] % (CAST_SLABS * 16) == 0, "a row slab must be whole bf16 sublane tiles"
    slab = lambda w: pl.BlockSpec((w.shape[0] // CAST_SLABS, w.shape[1]), lambda b, i: ((b * nt + i) // per, 0))
    place = (jnp.arange(MLA_ROPE_DIM)[:, None] + MLA_NOPE_DIM == jnp.arange(LANES)[None, :]).astype(BF16)
    assert tm == TK, "attention reads one transposed query block per key-block-sized piece"
    cr, sr, cm, sm, c16, s16 = tables
    q_scale = (1.0 / math.sqrt(MLA_QK_DIM)) * math.log2(math.e)
    C = RET_CHUNK
    H = RET_HEADS
    log_g = jnp.log(1.0 - 2.0 ** (-5.0 - jnp.arange(H, dtype=F32)))
    j = jnp.arange(C, dtype=F32)
    diff = j[:, None] - j[None, :]
    dmask = jnp.where(diff[None] >= 0, jnp.exp(jnp.maximum(diff, 0.0)[None] * log_g[:, None, None]), 0.0)
    zeta = jnp.exp((C - 1 - j)[None, :] * log_g[:, None])
    xi = jnp.exp((j + 1)[None, :] * log_g[:, None])
    zeta_b = jnp.broadcast_to(zeta[:, :, None], (H, C, RET_HEAD_DIM))
    xi_b = jnp.broadcast_to(xi[:, :, None], (H, C, RET_HEAD_DIM))
    g_chunk = tuple(float((1.0 - 2.0 ** (-5.0 - h)) ** C) for h in range(H))
    const = lambda shape: pl.BlockSpec(shape, lambda b, i: (0,) * len(shape))
    tile = lambda w: pl.BlockSpec((1, tm, w), lambda b, i: (b, i, 0))
    bf = lambda w: jax.ShapeDtypeStruct((B, S, w), BF16)
    res = pl.pallas_call(
        partial(_in_proj_body, q_scale=q_scale, g_chunk=g_chunk, n_cast=len(cast_weights)),
        grid=(B, S // tm),
        in_specs=[tile(D), const((1, D)),
                  pl.BlockSpec((D, IN_COLS_ALL), lambda b, i: (0, 0), pipeline_mode=pl.Buffered(1)),
                  const((MLA_ROPE_DIM, LANES)),
                  const((1, MLA_Q_LORA)), const((MLA_HEADS * HEAD_SLAB, MLA_Q_LORA)),
                  const((1, MLA_KV_LORA)), const((MLA_KV_LORA, MLA_HEADS * HEAD_SLAB)),
                  const((MLA_WIDTH, MLA_KV_LORA)),
                  tile(LANES), tile(LANES), tile(LANES), tile(LANES),
                  pl.BlockSpec((1, MLA_ROPE_DIM // 2, tm), lambda b, i: (b, 0, i)),
                  pl.BlockSpec((1, MLA_ROPE_DIM // 2, tm), lambda b, i: (b, 0, i)),
                  const((H, C, C)), const((H, C, RET_HEAD_DIM)), const((H, C, RET_HEAD_DIM)),
                  const((1, RET_WIDTH))] + [slab(w) for w in cast_weights],
        out_specs=[tile(RET_WIDTH),
                   pl.BlockSpec((1, 1, MLA_HEADS * HEAD_SLAB, tm), lambda b, i: (b, i, 0, 0)),
                   tile(MLA_HEADS * HEAD_SLAB),
                   pl.BlockSpec((1, tm // TK, MLA_WIDTH, TK), lambda b, i: (b, i, 0, 0))]
        + [slab(w) for w in cast_weights],
        out_shape=[bf(RET_WIDTH),
                   jax.ShapeDtypeStruct((B, S // tm, MLA_HEADS * HEAD_SLAB, tm), BF16),
                   bf(MLA_HEADS * HEAD_SLAB),
                   jax.ShapeDtypeStruct((B, S // TK, MLA_WIDTH, TK), BF16)]
        + [jax.ShapeDtypeStruct(w.shape, BF16) for w in cast_weights],
        scratch_shapes=[pltpu.VMEM((H, RET_HEAD_DIM, RET_HEAD_DIM), F32),
                        pltpu.VMEM((D, IN_COLS_MAIN + LANES), BF16)],
        compiler_params=pltpu.CompilerParams(dimension_semantics=("arbitrary", "arbitrary"),
                                             vmem_limit_bytes=VMEM_LIMIT),
        name="in_proj",
    )(x, pre_mix_norm, w_in, place, q_norm, wuq_p, kv_norm, wuk_p, wvt, cr, sr, cm, sm, c16, s16,
      dmask, zeta_b, xi_b, ret_gn_w, *cast_weights)
    return res[:4], res[4:]


HEADS_PER_STEP = 2
PIECES = TQ // TK


def _attn_body(qt_ref, k_ref, vt_ref, o_ref, *scratch):
    n = 2 * HEADS_PER_STEP
    s_ref = {(i // HEADS_PER_STEP, i % HEADS_PER_STEP): scratch[i] for i in range(n)}
    mb_ref = {(i // HEADS_PER_STEP, i % HEADS_PER_STEP): scratch[n + i] for i in range(n)}
    nq = qt_ref.shape[1] // PIECES
    ones = jnp.ones((ONES_ROWS, TK), BF16)
    chains = [(hh, c) for c in range(PIECES) for hh in range(HEADS_PER_STEP)]
    row = lax.broadcasted_iota(jnp.int32, (TK, TK), 0)
    col = lax.broadcasted_iota(jnp.int32, (TK, TK), 1)
    tri = row <= col

    def score_piece(qstart, j, slot, hh, c):
        kstart = pl.multiple_of(j * TK, TK)
        cols = slice(hh * HEAD_SLAB, (hh + 1) * HEAD_SLAB)
        lanes = slice(c * TK, (c + 1) * TK)
        st = _dot(k_ref[0, pl.ds(kstart, TK), cols], qt_ref[0, qstart + c, cols, :])
        s_ref[slot, hh][:, lanes] = st
        mb_ref[slot, hh][:, lanes] = jnp.max(st, axis=0, keepdims=True)

    def acc_piece(j, slot, hh, c, state, masked):
        m, acc = state
        lanes = slice(c * TK, (c + 1) * TK)
        st = s_ref[slot, hh][:, lanes]
        if masked:
            st = jnp.where(tri, st, NEG_BIG)
            blk_max = jnp.max(st, axis=0, keepdims=True)
        else:
            blk_max = mb_ref[slot, hh][:, lanes]
        m_new = jnp.maximum(m, blk_max)
        p = jnp.exp2(st - m_new).astype(BF16)
        vte = jnp.concatenate([vt_ref[0, j, hh * MLA_V_DIM:(hh + 1) * MLA_V_DIM, :], ones], axis=0)
        return m_new, acc * jnp.exp2(m - m_new) + _dot(vte, p)

    def stage(carry, j, slot, first, diag, nxt_qstart, nxt_j, nxt_first):
        carry = dict(carry)
        for hh, c in chains:
            if c >= nxt_first:
                score_piece(nxt_qstart, nxt_j, 1 - slot, hh, c)
            if c >= first:
                carry[hh, c] = acc_piece(j, slot, hh, c, carry[hh, c], diag and c == first)
        return carry

    def tile(qi, _):
        qstart = qi * PIECES

        def pair(t, flat):
            carry = dict(zip(chains, flat))
            carry = stage(carry, 2 * t, 0, 0, False, qstart, 2 * t + 1, 0)
            carry = stage(carry, 2 * t + 1, 1, 0, False, qstart, 2 * t + 2, 0)
            return tuple(carry[ch] for ch in chains)

        init = tuple((jnp.full((1, TK), NEG_BIG, F32), jnp.zeros((MLA_V_DIM + ONES_ROWS, TK), F32))
                     for _ in chains)
        d0 = PIECES * qi
        carry = dict(zip(chains, lax.fori_loop(0, d0 // 2, pair, init)))
        for d in range(PIECES):
            if d + 1 < PIECES:
                carry = stage(carry, d0 + d, d % 2, d, True, qstart, d0 + d + 1, d + 1)
            else:
                carry = stage(carry, d0 + d, d % 2, d, True, jnp.minimum(qi + 1, nq - 1) * PIECES, 0, 0)
        for hh, c in chains:
            acc = carry[hh, c][1]
            o_ref[0, qi, hh * MLA_V_DIM:(hh + 1) * MLA_V_DIM, c * TK:(c + 1) * TK] = (
                acc[:MLA_V_DIM] / acc[MLA_V_DIM:MLA_V_DIM + 1]).astype(BF16)
        return 0

    for hh, c in chains:
        score_piece(0, 0, 0, hh, c)
    lax.fori_loop(0, nq, tile, 0)


def _mla_attention(qt, k, vt):
    B, S, _ = k.shape
    nkb = S // TK
    nq = S // TQ
    hp = HEADS_PER_STEP
    return pl.pallas_call(
        _attn_body,
        grid=(B, MLA_HEADS // hp),
        in_specs=[pl.BlockSpec((1, nkb, hp * HEAD_SLAB, TK), lambda b, g: (b, 0, g, 0)),
                  pl.BlockSpec((1, S, hp * HEAD_SLAB), lambda b, g: (b, 0, g)),
                  pl.BlockSpec((1, nkb, hp * MLA_V_DIM, TK), lambda b, g: (b, 0, g, 0))],
        out_specs=pl.BlockSpec((1, nq, hp * MLA_V_DIM, TQ), lambda b, g: (b, 0, g, 0)),
        out_shape=jax.ShapeDtypeStruct((B, nq, MLA_WIDTH, TQ), BF16),
        scratch_shapes=[pltpu.VMEM((TK, TQ), F32)] * (2 * hp) + [pltpu.VMEM((1, TQ), F32)] * (2 * hp),
        compiler_params=pltpu.CompilerParams(dimension_semantics=("arbitrary", "arbitrary"),
                                             vmem_limit_bytes=VMEM_LIMIT),
        name="mla_attn",
    )(qt, k, vt)


def _post_body(ret_ref, mla_ref, x_ref, p_ref, wo_ref, pmn_ref, pfn_ref, wg_ref, wu_ref, wd_ref,
               pofn_ref, wpp_ref, plen_ref, wpg_ref, bpg_ref, o_ref, act_ref):
    tm = x_ref.shape[1]
    halves = [slice(0, tm // 2), slice(tm // 2, tm)]
    mix = [_dot(ret_ref[0, r], wo_ref[0:RET_WIDTH, :]) + _dot_tn(mla_ref[0, 0][:, r], wo_ref[RET_WIDTH:, :])
           for r in halves]
    h1 = [x_ref[0, r] + _rms(m, pmn_ref[...]) for r, m in zip(halves, mix)]
    hn = [_rms(h, pfn_ref[...]).astype(BF16) for h in h1]
    for c in range(D_FF // FF_CHUNK):
        cols = slice(c * FF_CHUNK, (c + 1) * FF_CHUNK)
        for r, hnr in zip(halves, hn):
            g = _dot(hnr, wg_ref[:, cols])
            u = _dot(hnr, wu_ref[:, cols])
            act_ref[r, cols] = (g * jax.nn.sigmoid(g) * u).astype(BF16)
    ff = [_dot(act_ref[r, :], wd_ref[...]) for r in halves]
    h2 = [h + _rms(f, pofn_ref[...]) for h, f in zip(h1, ff)]
    e = [_rms(_dot(p_ref[0, r].astype(BF16), wpp_ref[...]), plen_ref[...]) for r in halves]
    gate = [jax.nn.sigmoid(_dot(h.astype(BF16), wpg_ref[...]) + bpg_ref[...]) for h in h2]
    for r, h, ee, gg in zip(halves, h2, e, gate):
        o_ref[0, r] = h + ee * gg


def _post(ret_out, mla_out, x, p, w_o, post_mix_norm, pre_ffn_norm, w_gate, w_up, w_down,
          post_ffn_norm, w_ple_proj, ple_norm, w_ple_gate, b_ple_gate):
    B, S, D = x.shape
    tm = TM_POST
    const = lambda shape: pl.BlockSpec(shape, lambda b, i: (0,) * len(shape),
                                       pipeline_mode=pl.Buffered(1))
    tile = lambda w: pl.BlockSpec((1, tm, w), lambda b, i: (b, i, 0))
    return pl.pallas_call(
        _post_body,
        grid=(B, S // tm),
        in_specs=[tile(RET_WIDTH),
                  pl.BlockSpec((1, 1, MLA_WIDTH, tm), lambda b, i: (b, i // (TQ // tm), 0, i % (TQ // tm))),
                  tile(D), tile(PLE_DIM),
                  const((D, D)), const((1, D)), const((1, D)),
                  const((D, D_FF)), const((D, D_FF)), const((D_FF, D)), const((1, D)),
                  const((PLE_DIM, D)), const((1, D)), const((D, D)), const((1, D))],
        out_specs=tile(D),
        out_shape=jax.ShapeDtypeStruct((B, S, D), F32),
        scratch_shapes=[pltpu.VMEM((tm, D_FF), BF16)],
        compiler_params=pltpu.CompilerParams(dimension_semantics=("arbitrary", "arbitrary"),
                                             vmem_limit_bytes=VMEM_LIMIT),
        name="post",
    )(ret_out, mla_out, x, p, w_o, post_mix_norm, pre_ffn_norm, w_gate, w_up, w_down,
      post_ffn_norm, w_ple_proj, ple_norm, w_ple_gate, b_ple_gate)


def _pad_heads(w, per_head, keep):
    K = w.shape[0]
    w = w.reshape(K, MLA_HEADS, per_head)[:, :, :keep]
    w = jnp.pad(w, ((0, 0), (0, 0), (0, HEAD_SLAB - keep)))
    return w.reshape(K, MLA_HEADS * HEAD_SLAB).astype(BF16)


def kernel(x, p, positions, pre_mix_norm, w_in, ret_gn_w, mla_q_norm, w_uq, mla_kv_norm, w_ukv, w_o,
           post_mix_norm, pre_ffn_norm, w_gate, w_up, w_down, post_ffn_norm, w_ple_proj, ple_norm,
           w_ple_gate, b_ple_gate):
    depth = pre_mix_norm.shape[0]
    h = x
    tables = _rope_tables(positions)
    for i in range(depth):
        big = (w_o[i], w_gate[i], w_up[i], w_down[i], w_ple_proj[i], w_ple_gate[i])
        wuq_p = _pad_heads(w_uq[i], MLA_QK_DIM, MLA_QK_DIM).T
        wuk_p = _pad_heads(w_ukv[i], MLA_NOPE_DIM + MLA_V_DIM, MLA_NOPE_DIM)
        wv = w_ukv[i].reshape(MLA_KV_LORA, MLA_HEADS, MLA_NOPE_DIM + MLA_V_DIM)[:, :, MLA_NOPE_DIM:]
        wvt = wv.reshape(MLA_KV_LORA, MLA_WIDTH).T.astype(BF16)
        (ret_out, q, k, vt), big = _in_proj(h, pre_mix_norm[i][None], w_in[i], mla_q_norm[i][None], wuq_p,
                                            mla_kv_norm[i][None], wuk_p, wvt, tables, ret_gn_w[i][None], big)
        wo_b, wg_b, wu_b, wd_b, wpp_b, wpg_b = big
        mla_out = _mla_attention(q, k, vt)
        h = _post(ret_out, mla_out, h, p[i], wo_b, post_mix_norm[i][None], pre_ffn_norm[i][None],
                  wg_b, wu_b, wd_b, post_ffn_norm[i][None], wpp_b, ple_norm[i][None], wpg_b,
                  b_ple_gate[i][None])
    return h
```

```python
import math
from functools import partial

import jax
import jax.numpy as jnp
from jax import lax
from jax.experimental import pallas as pl
from jax.experimental.pallas import tpu as pltpu

D_MODEL = 1024
PLE_DIM = 256
RET_HEADS = 4
RET_HEAD_DIM = 128
RET_WIDTH = RET_HEADS * RET_HEAD_DIM
MLA_HEADS = 8
MLA_NOPE_DIM = 64
MLA_ROPE_DIM = 32
MLA_QK_DIM = MLA_NOPE_DIM + MLA_ROPE_DIM
MLA_V_DIM = 64
MLA_WIDTH = MLA_HEADS * MLA_V_DIM
MLA_Q_LORA = 384
MLA_KV_LORA = 256
D_FF = 2816
ROPE_BASE = 10000.0
EPS = 1e-6
NEG_BIG = -1e30

LANES = 128
HEAD_SLAB = 128
IN_COLS_CQ_END = 4 * RET_WIDTH + MLA_Q_LORA
IN_COLS_MAIN = IN_COLS_CQ_END + MLA_KV_LORA
IN_COLS_ALL = IN_COLS_MAIN + MLA_ROPE_DIM
VMEM_LIMIT = 58 * 1024 * 1024

TM_PROJ = 512
RET_CHUNK = 256
TQ = 2048
TK = 512
ONES_ROWS = 16
TM_POST = 1024
FF_CHUNK = 256

BF16 = jnp.bfloat16
F32 = jnp.float32


def _dot(a, b):
    return jnp.dot(a, b, preferred_element_type=F32)


def _dot_nt(a, b):
    return lax.dot_general(a, b, (((1,), (1,)), ((), ())), preferred_element_type=F32)


def _dot_tn(a, b):
    return lax.dot_general(a, b, (((0,), (0,)), ((), ())), preferred_element_type=F32)


def _rms(v, w):
    return v * lax.rsqrt(jnp.mean(v * v, axis=-1, keepdims=True) + EPS) * w


TM_ROPE = 1024


def _rope_tables_body(pos_ref, invr_ref, invm_ref, cr_ref, sr_ref, cm_ref, sm_ref, c16_ref, s16_ref):
    pos = pos_ref[0].astype(F32)
    ang_r = invr_ref[...] * pos
    c = jnp.cos(ang_r)
    s = jnp.sin(ang_r)
    ang_m = invm_ref[...] * pos
    c16 = jnp.cos(ang_m)
    s16 = jnp.sin(ang_m)
    tm = pos.shape[1]
    pad = HEAD_SLAB - MLA_QK_DIM
    cr_ref[0] = jnp.concatenate([c, c], axis=0).T
    sr_ref[0] = jnp.concatenate([-s, s], axis=0).T
    cm_ref[0] = jnp.concatenate([jnp.ones((MLA_NOPE_DIM, tm), F32), c16, c16, jnp.zeros((pad, tm), F32)], axis=0).T
    sm_ref[0] = jnp.concatenate([jnp.zeros((MLA_NOPE_DIM, tm), F32), -s16, s16, jnp.zeros((pad, tm), F32)], axis=0).T
    c16_ref[0] = c16
    s16_ref[0] = s16


def _rope_tables(positions):
    B, S = positions.shape
    tm = TM_ROPE
    half_r = RET_HEAD_DIM // 2
    inv_r = 1.0 / (ROPE_BASE ** (jnp.arange(half_r, dtype=F32) / half_r))
    half_m = MLA_ROPE_DIM // 2
    inv_m = 1.0 / (ROPE_BASE ** (jnp.arange(half_m, dtype=F32) / half_m))
    inv_r = jnp.broadcast_to(inv_r[:, None], (half_r, tm))
    inv_m = jnp.broadcast_to(inv_m[:, None], (half_m, tm))
    tab = pl.BlockSpec((1, tm, LANES), lambda b, i: (b, i, 0))
    out = jax.ShapeDtypeStruct((B, S, LANES), F32)
    tab_t = pl.BlockSpec((1, half_m, tm), lambda b, i: (b, 0, i))
    out_t = jax.ShapeDtypeStruct((B, half_m, S), F32)
    return pl.pallas_call(
        _rope_tables_body,
        grid=(B, S // tm),
        in_specs=[pl.BlockSpec((1, 1, tm), lambda b, i: (b, 0, i)),
                  pl.BlockSpec((half_r, tm), lambda b, i: (0, 0)),
                  pl.BlockSpec((half_m, tm), lambda b, i: (0, 0))],
        out_specs=[tab, tab, tab, tab, tab_t, tab_t],
        out_shape=[out, out, out, out, out_t, out_t],
        compiler_params=pltpu.CompilerParams(dimension_semantics=("arbitrary", "arbitrary"),
                                             vmem_limit_bytes=VMEM_LIMIT),
        name="rope_tables",
    )(positions.reshape(B, 1, S), inv_r, inv_m)


N_PROJ_INPUTS = 19
CAST_SLABS = 16


def _in_proj_body(*refs, q_scale, g_chunk, n_cast):
    (x_ref, g_ref, win32_ref, place_ref, qn_ref, wuq_ref, kvn_ref, wuk_ref, wvt_ref,
     cr_ref, sr_ref, cm_ref, sm_ref, c16_ref, s16_ref, dmask_ref, zeta_ref, xi_ref, gnw_ref) = refs[:N_PROJ_INPUTS]
    w32_refs = refs[N_PROJ_INPUTS:N_PROJ_INPUTS + n_cast]
    ret_ref, qt_ref, k_ref, vt_ref = refs[N_PROJ_INPUTS + n_cast:N_PROJ_INPUTS + n_cast + 4]
    w16_refs = refs[N_PROJ_INPUTS + n_cast + 4:N_PROJ_INPUTS + 2 * n_cast + 4]
    state_ref, win_ref = refs[N_PROJ_INPUTS + 2 * n_cast + 4:]

    for w32_ref, w16_ref in zip(w32_refs, w16_refs):
        w16_ref[...] = w32_ref[...].astype(BF16)

    @pl.when((pl.program_id(0) == 0) & (pl.program_id(1) == 0))
    def _():
        a, b = IN_COLS_CQ_END, IN_COLS_MAIN
        win_ref[:, 0:a] = win32_ref[:, 0:a].astype(BF16)
        kr_w = win32_ref[:, b:b + MLA_ROPE_DIM].astype(BF16)
        win_ref[:, a:a + LANES] = _dot(kr_w, place_ref[...]).astype(BF16)
        win_ref[:, a + LANES:] = win32_ref[:, a:b].astype(BF16)

    @pl.when(pl.program_id(1) == 0)
    def _():
        state_ref[...] = jnp.zeros_like(state_ref)

    C = RET_CHUNK
    halves = [slice(i * C, (i + 1) * C) for i in range(x_ref.shape[1] // C)]
    ids = range(len(halves))
    xn = [_rms(x_ref[0, r], g_ref[...]).astype(BF16) for r in halves]
    cr = [cr_ref[0, r] for r in halves]
    sr = [sr_ref[0, r] for r in halves]

    def ret_rope(v, i):
        return v * cr[i] + pltpu.roll(v, RET_HEAD_DIM // 2, 1) * sr[i]

    pq = [_dot(x, win_ref[:, 0:RET_WIDTH]) for x in xn]
    pk = [_dot(x, win_ref[:, RET_WIDTH:2 * RET_WIDTH]) for x in xn]
    pv = [_dot(x, win_ref[:, 2 * RET_WIDTH:3 * RET_WIDTH]).astype(BF16) for x in xn]
    pg = [_dot(x, win_ref[:, 3 * RET_WIDTH:4 * RET_WIDTH]) for x in xn]

    head_cols = [slice(h * RET_HEAD_DIM, (h + 1) * RET_HEAD_DIM) for h in range(RET_HEADS)]
    rq = [[ret_rope(pq[i][:, cols], i).astype(BF16) for cols in head_cols] for i in ids]
    rk = [[(ret_rope(pk[i][:, cols], i) * (RET_HEAD_DIM ** -0.5)).astype(BF16) for cols in head_cols] for i in ids]
    for i in ids:
        for h, cols in enumerate(head_cols):
            q = rq[i][h]
            k = rk[i][h]
            v = pv[i][:, cols]
            scores = _dot_nt(q, k) * dmask_ref[h]
            inner = _dot(scores.astype(BF16), v)
            r_prev = state_ref[h]
            cross = _dot(q, r_prev.astype(BF16)) * xi_ref[h]
            kz = (k.astype(F32) * zeta_ref[h]).astype(BF16)
            state_ref[h] = g_chunk[h] * r_prev + _dot_tn(kz, v)
            y = inner + cross
            mu = jnp.mean(y, axis=-1, keepdims=True)
            yc = y - mu
            var = jnp.mean(yc * yc, axis=-1, keepdims=True)
            yn = yc * lax.rsqrt(var + EPS) * gnw_ref[:, cols]
            gate = pg[i][:, cols]
            ret_ref[0, halves[i], cols] = (gate * jax.nn.sigmoid(gate) * yn).astype(BF16)

    cm = [cm_ref[0, r] for r in halves]
    sm = [sm_ref[0, r] for r in halves]
    lane = lax.broadcasted_iota(jnp.int32, (C, LANES), 1)
    upper = lane >= (MLA_NOPE_DIM + MLA_ROPE_DIM // 2)

    def mla_rope(v, i):
        swapped = jnp.where(upper, pltpu.roll(v, MLA_ROPE_DIM // 2, 1),
                            pltpu.roll(v, LANES - MLA_ROPE_DIM // 2, 1))
        return v * cm[i] + swapped * sm[i]

    o = 4 * RET_WIDTH
    cq = [_dot(x, win_ref[:, o:IN_COLS_CQ_END]) for x in xn]
    cqn = [_rms(c, qn_ref[...]).astype(BF16) for c in cq]
    qt = [_dot_nt(wuq_ref[...], c) for c in cqn]
    c16 = [c16_ref[0, :, r] for r in halves]
    s16 = [s16_ref[0, :, r] for r in halves]
    half = MLA_ROPE_DIM // 2
    for h in range(MLA_HEADS):
        r0 = h * HEAD_SLAB
        for i in ids:
            x1 = qt[i][r0 + MLA_NOPE_DIM:r0 + MLA_NOPE_DIM + half]
            x2 = qt[i][r0 + MLA_NOPE_DIM + half:r0 + MLA_QK_DIM]
            slab = jnp.concatenate([qt[i][r0:r0 + MLA_NOPE_DIM], x1 * c16[i] - x2 * s16[i],
                                    x2 * c16[i] + x1 * s16[i], qt[i][r0 + MLA_QK_DIM:r0 + HEAD_SLAB]], axis=0)
            qt_ref[0, 0, r0:r0 + HEAD_SLAB, halves[i]] = (slab * q_scale).astype(BF16)

    ckv = [_dot(x, win_ref[:, IN_COLS_CQ_END + LANES:]) for x in xn]
    ckvn = [_rms(c, kvn_ref[...]).astype(BF16) for c in ckv]
    kr = [mla_rope(_dot(xn[i], win_ref[:, IN_COLS_CQ_END:IN_COLS_CQ_END + LANES]), i) for i in ids]
    kn = [_dot(c, wuk_ref[...]) for c in ckvn]
    for h in range(MLA_HEADS):
        sl = slice(h * HEAD_SLAB, (h + 1) * HEAD_SLAB)
        for i in ids:
            k_ref[0, halves[i], sl] = (kn[i][:, sl] + kr[i]).astype(BF16)
    for i in ids:
        vt_ref[0, 0, :, halves[i]] = _dot_nt(wvt_ref[...], ckvn[i]).astype(BF16)


def _in_proj(x, pre_mix_norm, w_in, q_norm, wuq_p, kv_norm, wuk_p, wvt, tables, ret_gn_w, cast_weights):
    B, S, D = x.shape
    tm = TM_PROJ
    nt = S // tm
    assert (B * nt) % CAST_SLABS == 0
    per = B * nt // CAST_SLABS
    for w in cast_weights:
        assert w.shape[0] % (CAST_SLABS * 16) == 0, "a row slab must be whole bf16 sublane tiles"
    slab = lambda w: pl.BlockSpec((w.shape[0] // CAST_SLABS, w.shape[1]), lambda b, i: ((b * nt + i) // per, 0))
    place = (jnp.arange(MLA_ROPE_DIM)[:, None] + MLA_NOPE_DIM == jnp.arange(LANES)[None, :]).astype(BF16)
    assert tm == TK, "attention reads one transposed query block per key-block-sized piece"
    cr, sr, cm, sm, c16, s16 = tables
    q_scale = (1.0 / math.sqrt(MLA_QK_DIM)) * math.log2(math.e)
    C = RET_CHUNK
    H = RET_HEADS
    log_g = jnp.log(1.0 - 2.0 ** (-5.0 - jnp.arange(H, dtype=F32)))
    j = jnp.arange(C, dtype=F32)
    diff = j[:, None] - j[None, :]
    dmask = jnp.where(diff[None] >= 0, jnp.exp(jnp.maximum(diff, 0.0)[None] * log_g[:, None, None]), 0.0)
    zeta = jnp.exp((C - 1 - j)[None, :] * log_g[:, None])
    xi = jnp.exp((j + 1)[None, :] * log_g[:, None])
    zeta_b = jnp.broadcast_to(zeta[:, :, None], (H, C, RET_HEAD_DIM))
    xi_b = jnp.broadcast_to(xi[:, :, None], (H, C, RET_HEAD_DIM))
    g_chunk = tuple(float((1.0 - 2.0 ** (-5.0 - h)) ** C) for h in range(H))
    const = lambda shape: pl.BlockSpec(shape, lambda b, i: (0,) * len(shape))
    tile = lambda w: pl.BlockSpec((1, tm, w), lambda b, i: (b, i, 0))
    bf = lambda w: jax.ShapeDtypeStruct((B, S, w), BF16)
    res = pl.pallas_call(
        partial(_in_proj_body, q_scale=q_scale, g_chunk=g_chunk, n_cast=len(cast_weights)),
        grid=(B, S // tm),
        in_specs=[tile(D), const((1, D)),
                  pl.BlockSpec((D, IN_COLS_ALL), lambda b, i: (0, 0), pipeline_mode=pl.Buffered(1)),
                  const((MLA_ROPE_DIM, LANES)),
                  const((1, MLA_Q_LORA)), const((MLA_HEADS * HEAD_SLAB, MLA_Q_LORA)),
                  const((1, MLA_KV_LORA)), const((MLA_KV_LORA, MLA_HEADS * HEAD_SLAB)),
                  const((MLA_WIDTH, MLA_KV_LORA)),
                  tile(LANES), tile(LANES), tile(LANES), tile(LANES),
                  pl.BlockSpec((1, MLA_ROPE_DIM // 2, tm), lambda b, i: (b, 0, i)),
                  pl.BlockSpec((1, MLA_ROPE_DIM // 2, tm), lambda b, i: (b, 0, i)),
                  const((H, C, C)), const((H, C, RET_HEAD_DIM)), const((H, C, RET_HEAD_DIM)),
                  const((1, RET_WIDTH))] + [slab(w) for w in cast_weights],
        out_specs=[tile(RET_WIDTH),
                   pl.BlockSpec((1, 1, MLA_HEADS * HEAD_SLAB, tm), lambda b, i: (b, i, 0, 0)),
                   tile(MLA_HEADS * HEAD_SLAB),
                   pl.BlockSpec((1, tm // TK, MLA_WIDTH, TK), lambda b, i: (b, i, 0, 0))]
        + [slab(w) for w in cast_weights],
        out_shape=[bf(RET_WIDTH),
                   jax.ShapeDtypeStruct((B, S // tm, MLA_HEADS * HEAD_SLAB, tm), BF16),
                   bf(MLA_HEADS * HEAD_SLAB),
                   jax.ShapeDtypeStruct((B, S // TK, MLA_WIDTH, TK), BF16)]
        + [jax.ShapeDtypeStruct(w.shape, BF16) for w in cast_weights],
        scratch_shapes=[pltpu.VMEM((H, RET_HEAD_DIM, RET_HEAD_DIM), F32),
                        pltpu.VMEM((D, IN_COLS_MAIN + LANES), BF16)],
        compiler_params=pltpu.CompilerParams(dimension_semantics=("arbitrary", "arbitrary"),
                                             vmem_limit_bytes=VMEM_LIMIT),
        name="in_proj",
    )(x, pre_mix_norm, w_in, place, q_norm, wuq_p, kv_norm, wuk_p, wvt, cr, sr, cm, sm, c16, s16,
      dmask, zeta_b, xi_b, ret_gn_w, *cast_weights)
    return res[:4], res[4:]


HEADS_PER_STEP = 2
PIECES = TQ // TK


def _attn_body(qt_ref, k_ref, vt_ref, o_ref, *scratch):
    n = 2 * HEADS_PER_STEP
    s_ref = {(i // HEADS_PER_STEP, i % HEADS_PER_STEP): scratch[i] for i in range(n)}
    mb_ref = {(i // HEADS_PER_STEP, i % HEADS_PER_STEP): scratch[n + i] for i in range(n)}
    nq = qt_ref.shape[1] // PIECES
    ones = jnp.ones((ONES_ROWS, TK), BF16)
    chains = [(hh, c) for c in range(PIECES) for hh in range(HEADS_PER_STEP)]
    row = lax.broadcasted_iota(jnp.int32, (TK, TK), 0)
    col = lax.broadcasted_iota(jnp.int32, (TK, TK), 1)
    tri = row <= col

    def score_piece(qstart, j, slot, hh, c):
        kstart = pl.multiple_of(j * TK, TK)
        cols = slice(hh * HEAD_SLAB, (hh + 1) * HEAD_SLAB)
        lanes = slice(c * TK, (c + 1) * TK)
        st = _dot(k_ref[0, pl.ds(kstart, TK), cols], qt_ref[0, qstart + c, cols, :])
        s_ref[slot, hh][:, lanes] = st
        mb_ref[slot, hh][:, lanes] = jnp.max(st, axis=0, keepdims=True)

    def acc_piece(j, slot, hh, c, state, masked):
        m, acc = state
        lanes = slice(c * TK, (c + 1) * TK)
        st = s_ref[slot, hh][:, lanes]
        if masked:
            st = jnp.where(tri, st, NEG_BIG)
            blk_max = jnp.max(st, axis=0, keepdims=True)
        else:
            blk_max = mb_ref[slot, hh][:, lanes]
        m_new = jnp.maximum(m, blk_max)
        p = jnp.exp2(st - m_new).astype(BF16)
        vte = jnp.concatenate([vt_ref[0, j, hh * MLA_V_DIM:(hh + 1) * MLA_V_DIM, :], ones], axis=0)
        return m_new, acc * jnp.exp2(m - m_new) + _dot(vte, p)

    def stage(carry, j, slot, first, diag, nxt_qstart, nxt_j, nxt_first):
        carry = dict(carry)
        for hh, c in chains:
            if c >= nxt_first:
                score_piece(nxt_qstart, nxt_j, 1 - slot, hh, c)
            if c >= first:
                carry[hh, c] = acc_piece(j, slot, hh, c, carry[hh, c], diag and c == first)
        return carry

    def tile(qi, _):
        qstart = qi * PIECES

        def pair(t, flat):
            carry = dict(zip(chains, flat))
            carry = stage(carry, 2 * t, 0, 0, False, qstart, 2 * t + 1, 0)
            carry = stage(carry, 2 * t + 1, 1, 0, False, qstart, 2 * t + 2, 0)
            return tuple(carry[ch] for ch in chains)

        init = tuple((jnp.full((1, TK), NEG_BIG, F32), jnp.zeros((MLA_V_DIM + ONES_ROWS, TK), F32))
                     for _ in chains)
        d0 = PIECES * qi
        carry = dict(zip(chains, lax.fori_loop(0, d0 // 2, pair, init)))
        for d in range(PIECES):
            if d + 1 < PIECES:
                carry = stage(carry, d0 + d, d % 2, d, True, qstart, d0 + d + 1, d + 1)
            else:
                carry = stage(carry, d0 + d, d % 2, d, True, jnp.minimum(qi + 1, nq - 1) * PIECES, 0, 0)
        for hh, c in chains:
            acc = carry[hh, c][1]
            o_ref[0, qi, hh * MLA_V_DIM:(hh + 1) * MLA_V_DIM, c * TK:(c + 1) * TK] = (
                acc[:MLA_V_DIM] / acc[MLA_V_DIM:MLA_V_DIM + 1]).astype(BF16)
        return 0

    for hh, c in chains:
        score_piece(0, 0, 0, hh, c)
    lax.fori_loop(0, nq, tile, 0)


def _mla_attention(qt, k, vt):
    B, S, _ = k.shape
    nkb = S // TK
    nq = S // TQ
    hp = HEADS_PER_STEP
    return pl.pallas_call(
        _attn_body,
        grid=(B, MLA_HEADS // hp),
        in_specs=[pl.BlockSpec((1, nkb, hp * HEAD_SLAB, TK), lambda b, g: (b, 0, g, 0)),
                  pl.BlockSpec((1, S, hp * HEAD_SLAB), lambda b, g: (b, 0, g)),
                  pl.BlockSpec((1, nkb, hp * MLA_V_DIM, TK), lambda b, g: (b, 0, g, 0))],
        out_specs=pl.BlockSpec((1, nq, hp * MLA_V_DIM, TQ), lambda b, g: (b, 0, g, 0)),
        out_shape=jax.ShapeDtypeStruct((B, nq, MLA_WIDTH, TQ), BF16),
        scratch_shapes=[pltpu.VMEM((TK, TQ), F32)] * (2 * hp) + [pltpu.VMEM((1, TQ), F32)] * (2 * hp),
        compiler_params=pltpu.CompilerParams(dimension_semantics=("arbitrary", "arbitrary"),
                                             vmem_limit_bytes=VMEM_LIMIT),
        name="mla_attn",
    )(qt, k, vt)


def _post_body(ret_ref, mla_ref, x_ref, p_ref, wo_ref, pmn_ref, pfn_ref, wg_ref, wu_ref, wd_ref,
               pofn_ref, wpp_ref, plen_ref, wpg_ref, bpg_ref, o_ref, act_ref):
    tm = x_ref.shape[1]
    halves = [slice(0, tm // 2), slice(tm // 2, tm)]
    mix = [_dot(ret_ref[0, r], wo_ref[0:RET_WIDTH, :]) + _dot_tn(mla_ref[0, 0][:, r], wo_ref[RET_WIDTH:, :])
           for r in halves]
    h1 = [x_ref[0, r] + _rms(m, pmn_ref[...]) for r, m in zip(halves, mix)]
    hn = [_rms(h, pfn_ref[...]).astype(BF16) for h in h1]
    for c in range(D_FF // FF_CHUNK):
        cols = slice(c * FF_CHUNK, (c + 1) * FF_CHUNK)
        for r, hnr in zip(halves, hn):
            g = _dot(hnr, wg_ref[:, cols])
            u = _dot(hnr, wu_ref[:, cols])
            act_ref[r, cols] = (g * jax.nn.sigmoid(g) * u).astype(BF16)
    ff = [_dot(act_ref[r, :], wd_ref[...]) for r in halves]
    h2 = [h + _rms(f, pofn_ref[...]) for h, f in zip(h1, ff)]
    e = [_rms(_dot(p_ref[0, r].astype(BF16), wpp_ref[...]), plen_ref[...]) for r in halves]
    gate = [jax.nn.sigmoid(_dot(h.astype(BF16), wpg_ref[...]) + bpg_ref[...]) for h in h2]
    for r, h, ee, gg in zip(halves, h2, e, gate):
        o_ref[0, r] = h + ee * gg


def _post(ret_out, mla_out, x, p, w_o, post_mix_norm, pre_ffn_norm, w_gate, w_up, w_down,
          post_ffn_norm, w_ple_proj, ple_norm, w_ple_gate, b_ple_gate):
    B, S, D = x.shape
    tm = TM_POST
    const = lambda shape: pl.BlockSpec(shape, lambda b, i: (0,) * len(shape),
                                       pipeline_mode=pl.Buffered(1))
    tile = lambda w: pl.BlockSpec((1, tm, w), lambda b, i: (b, i, 0))
    return pl.pallas_call(
        _post_body,
        grid=(B, S // tm),
        in_specs=[tile(RET_WIDTH),
                  pl.BlockSpec((1, 1, MLA_WIDTH, tm), lambda b, i: (b, i // (TQ // tm), 0, i % (TQ // tm))),
                  tile(D), tile(PLE_DIM),
                  const((D, D)), const((1, D)), const((1, D)),
                  const((D, D_FF)), const((D, D_FF)), const((D_FF, D)), const((1, D)),
                  const((PLE_DIM, D)), const((1, D)), const((D, D)), const((1, D))],
        out_specs=tile(D),
        out_shape=jax.ShapeDtypeStruct((B, S, D), F32),
        scratch_shapes=[pltpu.VMEM((tm, D_FF), BF16)],
        compiler_params=pltpu.CompilerParams(dimension_semantics=("arbitrary", "arbitrary"),
                                             vmem_limit_bytes=VMEM_LIMIT),
        name="post",
    )(ret_out, mla_out, x, p, w_o, post_mix_norm, pre_ffn_norm, w_gate, w_up, w_down,
      post_ffn_norm, w_ple_proj, ple_norm, w_ple_gate, b_ple_gate)


def _pad_heads(w, per_head, keep):
    K = w.shape[0]
    w = w.reshape(K, MLA_HEADS, per_head)[:, :, :keep]
    w = jnp.pad(w, ((0, 0), (0, 0), (0, HEAD_SLAB - keep)))
    return w.reshape(K, MLA_HEADS * HEAD_SLAB).astype(BF16)


def kernel(x, p, positions, pre_mix_norm, w_in, ret_gn_w, mla_q_norm, w_uq, mla_kv_norm, w_ukv, w_o,
           post_mix_norm, pre_ffn_norm, w_gate, w_up, w_down, post_ffn_norm, w_ple_proj, ple_norm,
           w_ple_gate, b_ple_gate):
    depth = pre_mix_norm.shape[0]
    h = x
    tables = _rope_tables(positions)
    for i in range(depth):
        big = (w_o[i], w_gate[i], w_up[i], w_down[i], w_ple_proj[i], w_ple_gate[i])
        wuq_p = _pad_heads(w_uq[i], MLA_QK_DIM, MLA_QK_DIM).T
        wuk_p = _pad_heads(w_ukv[i], MLA_NOPE_DIM + MLA_V_DIM, MLA_NOPE_DIM)
        wv = w_ukv[i].reshape(MLA_KV_LORA, MLA_HEADS, MLA_NOPE_DIM + MLA_V_DIM)[:, :, MLA_NOPE_DIM:]
        wvt = wv.reshape(MLA_KV_LORA, MLA_WIDTH).T.astype(BF16)
        (ret_out, q, k, vt), big = _in_proj(h, pre_mix_norm[i][None], w_in[i], mla_q_norm[i][None], wuq_p,
                                            mla_kv_norm[i][None], wuk_p, wvt, tables, ret_gn_w[i][None], big)
        wo_b, wg_b, wu_b, wd_b, wpp_b, wpg_b = big
        mla_out = _mla_attention(q, k, vt)
        h = _post(ret_out, mla_out, h, p[i], wo_b, post_mix_norm[i][None], pre_ffn_norm[i][None],
                  wg_b, wu_b, wd_b, post_ffn_norm[i][None], wpp_b, ple_norm[i][None], wpg_b,
                  b_ple_gate[i][None])
    return h
```

```python
import math
from functools import partial

import jax
import jax.numpy as jnp
from jax import lax
from jax.experimental import pallas as pl
from jax.experimental.pallas import tpu as pltpu

D_MODEL = 1024
PLE_DIM = 256
RET_HEADS = 4
RET_HEAD_DIM = 128
RET_WIDTH = RET_HEADS * RET_HEAD_DIM
MLA_HEADS = 8
MLA_NOPE_DIM = 64
MLA_ROPE_DIM = 32
MLA_QK_DIM = MLA_NOPE_DIM + MLA_ROPE_DIM
MLA_V_DIM = 64
MLA_WIDTH = MLA_HEADS * MLA_V_DIM
MLA_Q_LORA = 384
MLA_KV_LORA = 256
D_FF = 2816
ROPE_BASE = 10000.0
EPS = 1e-6
NEG_BIG = -1e30

LANES = 128
HEAD_SLAB = 128
IN_COLS_CQ_END = 4 * RET_WIDTH + MLA_Q_LORA
IN_COLS_MAIN = IN_COLS_CQ_END + MLA_KV_LORA
IN_COLS_ALL = IN_COLS_MAIN + MLA_ROPE_DIM
VMEM_LIMIT = 58 * 1024 * 1024

TM_PROJ = 512
RET_CHUNK = 256
TQ = 2048
TK = 512
ONES_ROWS = 16
TM_POST = 1024
FF_CHUNK = 256

BF16 = jnp.bfloat16
F32 = jnp.float32


def _dot(a, b):
    return jnp.dot(a, b, preferred_element_type=F32)


def _dot_nt(a, b):
    return lax.dot_general(a, b, (((1,), (1,)), ((), ())), preferred_element_type=F32)


def _dot_tn(a, b):
    return lax.dot_general(a, b, (((0,), (0,)), ((), ())), preferred_element_type=F32)


def _rms(v, w):
    return v * lax.rsqrt(jnp.mean(v * v, axis=-1, keepdims=True) + EPS) * w


TM_ROPE = 1024


def _rope_tables_body(pos_ref, invr_ref, invm_ref, cr_ref, sr_ref, cm_ref, sm_ref, c16_ref, s16_ref):
    pos = pos_ref[0].astype(F32)
    ang_r = invr_ref[...] * pos
    c = jnp.cos(ang_r)
    s = jnp.sin(ang_r)
    ang_m = invm_ref[...] * pos
    c16 = jnp.cos(ang_m)
    s16 = jnp.sin(ang_m)
    tm = pos.shape[1]
    pad = HEAD_SLAB - MLA_QK_DIM
    cr_ref[0] = jnp.concatenate([c, c], axis=0).T
    sr_ref[0] = jnp.concatenate([-s, s], axis=0).T
    cm_ref[0] = jnp.concatenate([jnp.ones((MLA_NOPE_DIM, tm), F32), c16, c16, jnp.zeros((pad, tm), F32)], axis=0).T
    sm_ref[0] = jnp.concatenate([jnp.zeros((MLA_NOPE_DIM, tm), F32), -s16, s16, jnp.zeros((pad, tm), F32)], axis=0).T
    c16_ref[0] = c16
    s16_ref[0] = s16


def _rope_tables(positions):
    B, S = positions.shape
    tm = TM_ROPE
    half_r = RET_HEAD_DIM // 2
    inv_r = 1.0 / (ROPE_BASE ** (jnp.arange(half_r, dtype=F32) / half_r))
    half_m = MLA_ROPE_DIM // 2
    inv_m = 1.0 / (ROPE_BASE ** (jnp.arange(half_m, dtype=F32) / half_m))
    inv_r = jnp.broadcast_to(inv_r[:, None], (half_r, tm))
    inv_m = jnp.broadcast_to(inv_m[:, None], (half_m, tm))
    tab = pl.BlockSpec((1, tm, LANES), lambda b, i: (b, i, 0))
    out = jax.ShapeDtypeStruct((B, S, LANES), F32)
    tab_t = pl.BlockSpec((1, half_m, tm), lambda b, i: (b, 0, i))
    out_t = jax.ShapeDtypeStruct((B, half_m, S), F32)
    return pl.pallas_call(
        _rope_tables_body,
        grid=(B, S // tm),
        in_specs=[pl.BlockSpec((1, 1, tm), lambda b, i: (b, 0, i)),
                  pl.BlockSpec((half_r, tm), lambda b, i: (0, 0)),
                  pl.BlockSpec((half_m, tm), lambda b, i: (0, 0))],
        out_specs=[tab, tab, tab, tab, tab_t, tab_t],
        out_shape=[out, out, out, out, out_t, out_t],
        compiler_params=pltpu.CompilerParams(dimension_semantics=("arbitrary", "arbitrary"),
                                             vmem_limit_bytes=VMEM_LIMIT),
        name="rope_tables",
    )(positions.reshape(B, 1, S), inv_r, inv_m)


N_PROJ_INPUTS = 16
CAST_SLABS = 16


def _in_proj_body(*refs, q_scale, g_chunk, n_cast):
    (x_ref, g_ref, win32_ref, place_ref, qn_ref, wuq_ref, kvn_ref, wuk_ref, wvt_ref,
     pos_ref, invr_ref, invm_ref, dmask_ref, zeta_ref, xi_ref, gnw_ref) = refs[:N_PROJ_INPUTS]
    w32_refs = refs[N_PROJ_INPUTS:N_PROJ_INPUTS + n_cast]
    ret_ref, qt_ref, k_ref, vt_ref = refs[N_PROJ_INPUTS + n_cast:N_PROJ_INPUTS + n_cast + 4]
    w16_refs = refs[N_PROJ_INPUTS + n_cast + 4:N_PROJ_INPUTS + 2 * n_cast + 4]
    state_ref, win_ref = refs[N_PROJ_INPUTS + 2 * n_cast + 4:]

    for w32_ref, w16_ref in zip(w32_refs, w16_refs):
        w16_ref[...] = w32_ref[...].astype(BF16)

    @pl.when((pl.program_id(0) == 0) & (pl.program_id(1) == 0))
    def _():
        a, b = IN_COLS_CQ_END, IN_COLS_MAIN
        win_ref[:, 0:a] = win32_ref[:, 0:a].astype(BF16)
        kr_w = win32_ref[:, b:b + MLA_ROPE_DIM].astype(BF16)
        win_ref[:, a:a + LANES] = _dot(kr_w, place_ref[...]).astype(BF16)
        win_ref[:, a + LANES:] = win32_ref[:, a:b].astype(BF16)

    @pl.when(pl.program_id(1) == 0)
    def _():
        state_ref[...] = jnp.zeros_like(state_ref)

    C = RET_CHUNK
    halves = [slice(i * C, (i + 1) * C) for i in range(x_ref.shape[1] // C)]
    ids = range(len(halves))
    xn = [_rms(x_ref[0, r], g_ref[...]).astype(BF16) for r in halves]
    cr, sr, cm, sm, c16, s16 = [], [], [], [], [], []
    pad = HEAD_SLAB - MLA_QK_DIM
    for r in halves:
        pos = pos_ref[0, :, r].astype(F32)
        ang_r = invr_ref[:, r] * pos
        c, s = jnp.cos(ang_r), jnp.sin(ang_r)
        ang_m = invm_ref[:, r] * pos
        cc, ss = jnp.cos(ang_m), jnp.sin(ang_m)
        cr.append(jnp.concatenate([c, c], axis=0).T)
        sr.append(jnp.concatenate([-s, s], axis=0).T)
        cm.append(jnp.concatenate([jnp.ones((MLA_NOPE_DIM, C), F32), cc, cc, jnp.zeros((pad, C), F32)], axis=0).T)
        sm.append(jnp.concatenate([jnp.zeros((MLA_NOPE_DIM, C), F32), -ss, ss, jnp.zeros((pad, C), F32)], axis=0).T)
        c16.append(cc)
        s16.append(ss)

    def ret_rope(v, i):
        return v * cr[i] + pltpu.roll(v, RET_HEAD_DIM // 2, 1) * sr[i]

    pq = [_dot(x, win_ref[:, 0:RET_WIDTH]) for x in xn]
    pk = [_dot(x, win_ref[:, RET_WIDTH:2 * RET_WIDTH]) for x in xn]
    pv = [_dot(x, win_ref[:, 2 * RET_WIDTH:3 * RET_WIDTH]).astype(BF16) for x in xn]
    pg = [_dot(x, win_ref[:, 3 * RET_WIDTH:4 * RET_WIDTH]) for x in xn]

    head_cols = [slice(h * RET_HEAD_DIM, (h + 1) * RET_HEAD_DIM) for h in range(RET_HEADS)]
    rq = [[ret_rope(pq[i][:, cols], i).astype(BF16) for cols in head_cols] for i in ids]
    rk = [[(ret_rope(pk[i][:, cols], i) * (RET_HEAD_DIM ** -0.5)).astype(BF16) for cols in head_cols] for i in ids]
    for i in ids:
        for h, cols in enumerate(head_cols):
            q = rq[i][h]
            k = rk[i][h]
            v = pv[i][:, cols]
            scores = _dot_nt(q, k) * dmask_ref[h]
            inner = _dot(scores.astype(BF16), v)
            r_prev = state_ref[h]
            cross = _dot(q, r_prev.astype(BF16)) * xi_ref[h]
            kz = (k.astype(F32) * zeta_ref[h]).astype(BF16)
            state_ref[h] = g_chunk[h] * r_prev + _dot_tn(kz, v)
            y = inner + cross
            mu = jnp.mean(y, axis=-1, keepdims=True)
            yc = y - mu
            var = jnp.mean(yc * yc, axis=-1, keepdims=True)
            yn = yc * lax.rsqrt(var + EPS) * gnw_ref[:, cols]
            gate = pg[i][:, cols]
            ret_ref[0, halves[i], cols] = (gate * jax.nn.sigmoid(gate) * yn).astype(BF16)

    lane = lax.broadcasted_iota(jnp.int32, (C, LANES), 1)
    upper = lane >= (MLA_NOPE_DIM + MLA_ROPE_DIM // 2)

    def mla_rope(v, i):
        swapped = jnp.where(upper, pltpu.roll(v, MLA_ROPE_DIM // 2, 1),
                            pltpu.roll(v, LANES - MLA_ROPE_DIM // 2, 1))
        return v * cm[i] + swapped * sm[i]

    o = 4 * RET_WIDTH
    cq = [_dot(x, win_ref[:, o:IN_COLS_CQ_END]) for x in xn]
    cqn = [_rms(c, qn_ref[...]).astype(BF16) for c in cq]
    qt = [_dot_nt(wuq_ref[...], c) for c in cqn]
    half = MLA_ROPE_DIM // 2
    for h in range(MLA_HEADS):
        r0 = h * HEAD_SLAB
        for i in ids:
            x1 = qt[i][r0 + MLA_NOPE_DIM:r0 + MLA_NOPE_DIM + half]
            x2 = qt[i][r0 + MLA_NOPE_DIM + half:r0 + MLA_QK_DIM]
            slab = jnp.concatenate([qt[i][r0:r0 + MLA_NOPE_DIM], x1 * c16[i] - x2 * s16[i],
                                    x2 * c16[i] + x1 * s16[i], qt[i][r0 + MLA_QK_DIM:r0 + HEAD_SLAB]], axis=0)
            qt_ref[0, 0, r0:r0 + HEAD_SLAB, halves[i]] = (slab * q_scale).astype(BF16)

    ckv = [_dot(x, win_ref[:, IN_COLS_CQ_END + LANES:]) for x in xn]
    ckvn = [_rms(c, kvn_ref[...]).astype(BF16) for c in ckv]
    kr = [mla_rope(_dot(xn[i], win_ref[:, IN_COLS_CQ_END:IN_COLS_CQ_END + LANES]), i) for i in ids]
    kn = [_dot(c, wuk_ref[...]) for c in ckvn]
    for h in range(MLA_HEADS):
        sl = slice(h * HEAD_SLAB, (h + 1) * HEAD_SLAB)
        for i in ids:
            k_ref[0, halves[i], sl] = (kn[i][:, sl] + kr[i]).astype(BF16)
    for i in ids:
        vt_ref[0, 0, :, halves[i]] = _dot_nt(wvt_ref[...], ckvn[i]).astype(BF16)


def _in_proj(x, pre_mix_norm, w_in, q_norm, wuq_p, kv_norm, wuk_p, wvt, positions, ret_gn_w, cast_weights):
    B, S, D = x.shape
    tm = TM_PROJ
    nt = S // tm
    assert (B * nt) % CAST_SLABS == 0
    per = B * nt // CAST_SLABS
    for w in cast_weights:
        assert w.shape[0] % (CAST_SLABS * 16) == 0, "a row slab must be whole bf16 sublane tiles"
    slab = lambda w: pl.BlockSpec((w.shape[0] // CAST_SLABS, w.shape[1]), lambda b, i: ((b * nt + i) // per, 0))
    place = (jnp.arange(MLA_ROPE_DIM)[:, None] + MLA_NOPE_DIM == jnp.arange(LANES)[None, :]).astype(BF16)
    assert tm == TK, "attention reads one transposed query block per key-block-sized piece"
    half_r = RET_HEAD_DIM // 2
    half_m = MLA_ROPE_DIM // 2
    inv_r = 1.0 / (ROPE_BASE ** (jnp.arange(half_r, dtype=F32) / half_r))
    inv_m = 1.0 / (ROPE_BASE ** (jnp.arange(half_m, dtype=F32) / half_m))
    inv_r = jnp.broadcast_to(inv_r[:, None], (half_r, tm))
    inv_m = jnp.broadcast_to(inv_m[:, None], (half_m, tm))
    q_scale = (1.0 / math.sqrt(MLA_QK_DIM)) * math.log2(math.e)
    C = RET_CHUNK
    H = RET_HEADS
    log_g = jnp.log(1.0 - 2.0 ** (-5.0 - jnp.arange(H, dtype=F32)))
    j = jnp.arange(C, dtype=F32)
    diff = j[:, None] - j[None, :]
    dmask = jnp.where(diff[None] >= 0, jnp.exp(jnp.maximum(diff, 0.0)[None] * log_g[:, None, None]), 0.0)
    zeta = jnp.exp((C - 1 - j)[None, :] * log_g[:, None])
    xi = jnp.exp((j + 1)[None, :] * log_g[:, None])
    zeta_b = jnp.broadcast_to(zeta[:, :, None], (H, C, RET_HEAD_DIM))
    xi_b = jnp.broadcast_to(xi[:, :, None], (H, C, RET_HEAD_DIM))
    g_chunk = tuple(float((1.0 - 2.0 ** (-5.0 - h)) ** C) for h in range(H))
    const = lambda shape: pl.BlockSpec(shape, lambda b, i: (0,) * len(shape))
    tile = lambda w: pl.BlockSpec((1, tm, w), lambda b, i: (b, i, 0))
    bf = lambda w: jax.ShapeDtypeStruct((B, S, w), BF16)
    res = pl.pallas_call(
        partial(_in_proj_body, q_scale=q_scale, g_chunk=g_chunk, n_cast=len(cast_weights)),
        grid=(B, S // tm),
        in_specs=[tile(D), const((1, D)),
                  pl.BlockSpec((D, IN_COLS_ALL), lambda b, i: (0, 0), pipeline_mode=pl.Buffered(1)),
                  const((MLA_ROPE_DIM, LANES)),
                  const((1, MLA_Q_LORA)), const((MLA_HEADS * HEAD_SLAB, MLA_Q_LORA)),
                  const((1, MLA_KV_LORA)), const((MLA_KV_LORA, MLA_HEADS * HEAD_SLAB)),
                  const((MLA_WIDTH, MLA_KV_LORA)),
                  pl.BlockSpec((1, 1, tm), lambda b, i: (b, 0, i)), const((half_r, tm)), const((half_m, tm)),
                  const((H, C, C)), const((H, C, RET_HEAD_DIM)), const((H, C, RET_HEAD_DIM)),
                  const((1, RET_WIDTH))] + [slab(w) for w in cast_weights],
        out_specs=[tile(RET_WIDTH),
                   pl.BlockSpec((1, 1, MLA_HEADS * HEAD_SLAB, tm), lambda b, i: (b, i, 0, 0)),
                   tile(MLA_HEADS * HEAD_SLAB),
                   pl.BlockSpec((1, tm // TK, MLA_WIDTH, TK), lambda b, i: (b, i, 0, 0))]
        + [slab(w) for w in cast_weights],
        out_shape=[bf(RET_WIDTH),
                   jax.ShapeDtypeStruct((B, S // tm, MLA_HEADS * HEAD_SLAB, tm), BF16),
                   bf(MLA_HEADS * HEAD_SLAB),
                   jax.ShapeDtypeStruct((B, S // TK, MLA_WIDTH, TK), BF16)]
        + [jax.ShapeDtypeStruct(w.shape, BF16) for w in cast_weights],
        scratch_shapes=[pltpu.VMEM((H, RET_HEAD_DIM, RET_HEAD_DIM), F32),
                        pltpu.VMEM((D, IN_COLS_MAIN + LANES), BF16)],
        compiler_params=pltpu.CompilerParams(dimension_semantics=("arbitrary", "arbitrary"),
                                             vmem_limit_bytes=VMEM_LIMIT),
        name="in_proj",
    )(x, pre_mix_norm, w_in, place, q_norm, wuq_p, kv_norm, wuk_p, wvt,
      positions.reshape(B, 1, S), inv_r, inv_m,
      dmask, zeta_b, xi_b, ret_gn_w, *cast_weights)
    return res[:4], res[4:]


HEADS_PER_STEP = 2
PIECES = TQ // TK


def _attn_body(qt_ref, k_ref, vt_ref, o_ref, *scratch):
    n = 2 * HEADS_PER_STEP
    s_ref = {(i // HEADS_PER_STEP, i % HEADS_PER_STEP): scratch[i] for i in range(n)}
    mb_ref = {(i // HEADS_PER_STEP, i % HEADS_PER_STEP): scratch[n + i] for i in range(n)}
    nq = qt_ref.shape[1] // PIECES
    ones = jnp.ones((ONES_ROWS, TK), BF16)
    chains = [(hh, c) for c in range(PIECES) for hh in range(HEADS_PER_STEP)]
    row = lax.broadcasted_iota(jnp.int32, (TK, TK), 0)
    col = lax.broadcasted_iota(jnp.int32, (TK, TK), 1)
    tri = row <= col

    def score_piece(qstart, j, slot, hh, c):
        kstart = pl.multiple_of(j * TK, TK)
        cols = slice(hh * HEAD_SLAB, (hh + 1) * HEAD_SLAB)
        lanes = slice(c * TK, (c + 1) * TK)
        st = _dot(k_ref[0, pl.ds(kstart, TK), cols], qt_ref[0, qstart + c, cols, :])
        s_ref[slot, hh][:, lanes] = st
        mb_ref[slot, hh][:, lanes] = jnp.max(st, axis=0, keepdims=True)

    def acc_piece(j, slot, hh, c, state, masked):
        m, acc = state
        lanes = slice(c * TK, (c + 1) * TK)
        st = s_ref[slot, hh][:, lanes]
        if masked:
            st = jnp.where(tri, st, NEG_BIG)
            blk_max = jnp.max(st, axis=0, keepdims=True)
        else:
            blk_max = mb_ref[slot, hh][:, lanes]
        m_new = jnp.maximum(m, blk_max)
        p = jnp.exp2(st - m_new).astype(BF16)
        vte = jnp.concatenate([vt_ref[0, j, hh * MLA_V_DIM:(hh + 1) * MLA_V_DIM, :], ones], axis=0)
        return m_new, acc * jnp.exp2(m - m_new) + _dot(vte, p)

    def stage(carry, j, slot, first, diag, nxt_qstart, nxt_j, nxt_first):
        carry = dict(carry)
        for hh, c in chains:
            if c >= nxt_first:
                score_piece(nxt_qstart, nxt_j, 1 - slot, hh, c)
            if c >= first:
                carry[hh, c] = acc_piece(j, slot, hh, c, carry[hh, c], diag and c == first)
        return carry

    def tile(qi, _):
        qstart = qi * PIECES

        def pair(t, flat):
            carry = dict(zip(chains, flat))
            carry = stage(carry, 2 * t, 0, 0, False, qstart, 2 * t + 1, 0)
            carry = stage(carry, 2 * t + 1, 1, 0, False, qstart, 2 * t + 2, 0)
            return tuple(carry[ch] for ch in chains)

        init = tuple((jnp.full((1, TK), NEG_BIG, F32), jnp.zeros((MLA_V_DIM + ONES_ROWS, TK), F32))
                     for _ in chains)
        d0 = PIECES * qi
        carry = dict(zip(chains, lax.fori_loop(0, d0 // 2, pair, init)))
        for d in range(PIECES):
            if d + 1 < PIECES:
                carry = stage(carry, d0 + d, d % 2, d, True, qstart, d0 + d + 1, d + 1)
            else:
                carry = stage(carry, d0 + d, d % 2, d, True, jnp.minimum(qi + 1, nq - 1) * PIECES, 0, 0)
        for hh, c in chains:
            acc = carry[hh, c][1]
            o_ref[0, qi, hh * MLA_V_DIM:(hh + 1) * MLA_V_DIM, c * TK:(c + 1) * TK] = (
                acc[:MLA_V_DIM] / acc[MLA_V_DIM:MLA_V_DIM + 1]).astype(BF16)
        return 0

    for hh, c in chains:
        score_piece(0, 0, 0, hh, c)
    lax.fori_loop(0, nq, tile, 0)


def _mla_attention(qt, k, vt):
    B, S, _ = k.shape
    nkb = S // TK
    nq = S // TQ
    hp = HEADS_PER_STEP
    return pl.pallas_call(
        _attn_body,
        grid=(B, MLA_HEADS // hp),
        in_specs=[pl.BlockSpec((1, nkb, hp * HEAD_SLAB, TK), lambda b, g: (b, 0, g, 0)),
                  pl.BlockSpec((1, S, hp * HEAD_SLAB), lambda b, g: (b, 0, g)),
                  pl.BlockSpec((1, nkb, hp * MLA_V_DIM, TK), lambda b, g: (b, 0, g, 0))],
        out_specs=pl.BlockSpec((1, nq, hp * MLA_V_DIM, TQ), lambda b, g: (b, 0, g, 0)),
        out_shape=jax.ShapeDtypeStruct((B, nq, MLA_WIDTH, TQ), BF16),
        scratch_shapes=[pltpu.VMEM((TK, TQ), F32)] * (2 * hp) + [pltpu.VMEM((1, TQ), F32)] * (2 * hp),
        compiler_params=pltpu.CompilerParams(dimension_semantics=("arbitrary", "arbitrary"),
                                             vmem_limit_bytes=VMEM_LIMIT),
        name="mla_attn",
    )(qt, k, vt)


def _post_body(ret_ref, mla_ref, x_ref, p_ref, wo_ref, pmn_ref, pfn_ref, wg_ref, wu_ref, wd_ref,
               pofn_ref, wpp_ref, plen_ref, wpg_ref, bpg_ref, o_ref, act_ref):
    tm = x_ref.shape[1]
    halves = [slice(0, tm // 2), slice(tm // 2, tm)]
    mix = [_dot(ret_ref[0, r], wo_ref[0:RET_WIDTH, :]) + _dot_tn(mla_ref[0, 0][:, r], wo_ref[RET_WIDTH:, :])
           for r in halves]
    h1 = [x_ref[0, r] + _rms(m, pmn_ref[...]) for r, m in zip(halves, mix)]
    hn = [_rms(h, pfn_ref[...]).astype(BF16) for h in h1]
    for c in range(D_FF // FF_CHUNK):
        cols = slice(c * FF_CHUNK, (c + 1) * FF_CHUNK)
        for r, hnr in zip(halves, hn):
            g = _dot(hnr, wg_ref[:, cols])
            u = _dot(hnr, wu_ref[:, cols])
            act_ref[r, cols] = (g * jax.nn.sigmoid(g) * u).astype(BF16)
    ff = [_dot(act_ref[r, :], wd_ref[...]) for r in halves]
    h2 = [h + _rms(f, pofn_ref[...]) for h, f in zip(h1, ff)]
    e = [_rms(_dot(p_ref[0, r].astype(BF16), wpp_ref[...]), plen_ref[...]) for r in halves]
    gate = [jax.nn.sigmoid(_dot(h.astype(BF16), wpg_ref[...]) + bpg_ref[...]) for h in h2]
    for r, h, ee, gg in zip(halves, h2, e, gate):
        o_ref[0, r] = h + ee * gg


def _post(ret_out, mla_out, x, p, w_o, post_mix_norm, pre_ffn_norm, w_gate, w_up, w_down,
          post_ffn_norm, w_ple_proj, ple_norm, w_ple_gate, b_ple_gate):
    B, S, D = x.shape
    tm = TM_POST
    const = lambda shape: pl.BlockSpec(shape, lambda b, i: (0,) * len(shape),
                                       pipeline_mode=pl.Buffered(1))
    tile = lambda w: pl.BlockSpec((1, tm, w), lambda b, i: (b, i, 0))
    return pl.pallas_call(
        _post_body,
        grid=(B, S // tm),
        in_specs=[tile(RET_WIDTH),
                  pl.BlockSpec((1, 1, MLA_WIDTH, tm), lambda b, i: (b, i // (TQ // tm), 0, i % (TQ // tm))),
                  tile(D), tile(PLE_DIM),
                  const((D, D)), const((1, D)), const((1, D)),
                  const((D, D_FF)), const((D, D_FF)), const((D_FF, D)), const((1, D)),
                  const((PLE_DIM, D)), const((1, D)), const((D, D)), const((1, D))],
        out_specs=tile(D),
        out_shape=jax.ShapeDtypeStruct((B, S, D), F32),
        scratch_shapes=[pltpu.VMEM((tm, D_FF), BF16)],
        compiler_params=pltpu.CompilerParams(dimension_semantics=("arbitrary", "arbitrary"),
                                             vmem_limit_bytes=VMEM_LIMIT),
        name="post",
    )(ret_out, mla_out, x, p, w_o, post_mix_norm, pre_ffn_norm, w_gate, w_up, w_down,
      post_ffn_norm, w_ple_proj, ple_norm, w_ple_gate, b_ple_gate)


def _pad_heads(w, per_head, keep):
    K = w.shape[0]
    w = w.reshape(K, MLA_HEADS, per_head)[:, :, :keep]
    w = jnp.pad(w, ((0, 0), (0, 0), (0, HEAD_SLAB - keep)))
    return w.reshape(K, MLA_HEADS * HEAD_SLAB).astype(BF16)


def kernel(x, p, positions, pre_mix_norm, w_in, ret_gn_w, mla_q_norm, w_uq, mla_kv_norm, w_ukv, w_o,
           post_mix_norm, pre_ffn_norm, w_gate, w_up, w_down, post_ffn_norm, w_ple_proj, ple_norm,
           w_ple_gate, b_ple_gate):
    depth = pre_mix_norm.shape[0]
    h = x
    for i in range(depth):
        big = (w_o[i], w_gate[i], w_up[i], w_down[i], w_ple_proj[i], w_ple_gate[i])
        wuq_p = _pad_heads(w_uq[i], MLA_QK_DIM, MLA_QK_DIM).T
        wuk_p = _pad_heads(w_ukv[i], MLA_NOPE_DIM + MLA_V_DIM, MLA_NOPE_DIM)
        wv = w_ukv[i].reshape(MLA_KV_LORA, MLA_HEADS, MLA_NOPE_DIM + MLA_V_DIM)[:, :, MLA_NOPE_DIM:]
        wvt = wv.reshape(MLA_KV_LORA, MLA_WIDTH).T.astype(BF16)
        (ret_out, q, k, vt), big = _in_proj(h, pre_mix_norm[i][None], w_in[i], mla_q_norm[i][None], wuq_p,
                                            mla_kv_norm[i][None], wuk_p, wvt, positions, ret_gn_w[i][None], big)
        wo_b, wg_b, wu_b, wd_b, wpp_b, wpg_b = big
        mla_out = _mla_attention(q, k, vt)
        h = _post(ret_out, mla_out, h, p[i], wo_b, post_mix_norm[i][None], pre_ffn_norm[i][None],
                  wg_b, wu_b, wd_b, post_ffn_norm[i][None], wpp_b, ple_norm[i][None], wpg_b,
                  b_ple_gate[i][None])
    return h
```

```python
import math
from functools import partial

import jax
import jax.numpy as jnp
from jax import lax
from jax.experimental import pallas as pl
from jax.experimental.pallas import tpu as pltpu

D_MODEL = 1024
PLE_DIM = 256
RET_HEADS = 4
RET_HEAD_DIM = 128
RET_WIDTH = RET_HEADS * RET_HEAD_DIM
MLA_HEADS = 8
MLA_NOPE_DIM = 64
MLA_ROPE_DIM = 32
MLA_QK_DIM = MLA_NOPE_DIM + MLA_ROPE_DIM
MLA_V_DIM = 64
MLA_WIDTH = MLA_HEADS * MLA_V_DIM
MLA_Q_LORA = 384
MLA_KV_LORA = 256
D_FF = 2816
ROPE_BASE = 10000.0
EPS = 1e-6
NEG_BIG = -1e30

LANES = 128
HEAD_SLAB = 128
IN_COLS_CQ_END = 4 * RET_WIDTH + MLA_Q_LORA
IN_COLS_MAIN = IN_COLS_CQ_END + MLA_KV_LORA
IN_COLS_ALL = IN_COLS_MAIN + MLA_ROPE_DIM
VMEM_LIMIT = 58 * 1024 * 1024

TM_PROJ = 512
RET_CHUNK = 256
TQ = 2048
TK = 512
ONES_ROWS = 16
TM_POST = 1024
FF_CHUNK = 256

BF16 = jnp.bfloat16
F32 = jnp.float32


def _dot(a, b):
    return jnp.dot(a, b, preferred_element_type=F32)


def _dot_nt(a, b):
    return lax.dot_general(a, b, (((1,), (1,)), ((), ())), preferred_element_type=F32)


def _dot_tn(a, b):
    return lax.dot_general(a, b, (((0,), (0,)), ((), ())), preferred_element_type=F32)


def _rms(v, w):
    return v * lax.rsqrt(jnp.mean(v * v, axis=-1, keepdims=True) + EPS) * w


N_PROJ_INPUTS = 16
CAST_SLABS = 16


def _in_proj_body(*refs, q_scale, g_chunk, n_cast):
    (x_ref, g_ref, win32_ref, place_ref, qn_ref, wuq_ref, kvn_ref, wuk_ref, wvt_ref,
     pos_ref, invr_ref, invm_ref, dmask_ref, zeta_ref, xi_ref, gnw_ref) = refs[:N_PROJ_INPUTS]
    w32_refs = refs[N_PROJ_INPUTS:N_PROJ_INPUTS + n_cast]
    ret_ref, qt_ref, k_ref, vt_ref = refs[N_PROJ_INPUTS + n_cast:N_PROJ_INPUTS + n_cast + 4]
    w16_refs = refs[N_PROJ_INPUTS + n_cast + 4:N_PROJ_INPUTS + 2 * n_cast + 4]
    state_ref, win_ref = refs[N_PROJ_INPUTS + 2 * n_cast + 4:]

    for w32_ref, w16_ref in zip(w32_refs, w16_refs):
        w16_ref[...] = w32_ref[...].astype(BF16)

    @pl.when((pl.program_id(0) == 0) & (pl.program_id(1) == 0))
    def _():
        a, b = IN_COLS_CQ_END, IN_COLS_MAIN
        win_ref[:, 0:a] = win32_ref[:, 0:a].astype(BF16)
        kr_w = win32_ref[:, b:b + MLA_ROPE_DIM].astype(BF16)
        win_ref[:, a:a + LANES] = _dot(kr_w, place_ref[...]).astype(BF16)
        win_ref[:, a + LANES:] = win32_ref[:, a:b].astype(BF16)

    @pl.when(pl.program_id(1) == 0)
    def _():
        state_ref[...] = jnp.zeros_like(state_ref)

    C = RET_CHUNK
    halves = [slice(i * C, (i + 1) * C) for i in range(x_ref.shape[1] // C)]
    ids = range(len(halves))
    xn = [_rms(x_ref[0, r], g_ref[...]).astype(BF16) for r in halves]
    cr, sr, cm, sm, c16, s16 = [], [], [], [], [], []
    pad = HEAD_SLAB - MLA_QK_DIM
    for r in halves:
        pos = pos_ref[0, :, r].astype(F32)
        ang_r = invr_ref[:, r] * pos
        c, s = jnp.cos(ang_r), jnp.sin(ang_r)
        ang_m = invm_ref[:, r] * pos
        cc, ss = jnp.cos(ang_m), jnp.sin(ang_m)
        cr.append(jnp.concatenate([c, c], axis=0).T)
        sr.append(jnp.concatenate([-s, s], axis=0).T)
        cm.append(jnp.concatenate([jnp.ones((MLA_NOPE_DIM, C), F32), cc, cc, jnp.zeros((pad, C), F32)], axis=0).T)
        sm.append(jnp.concatenate([jnp.zeros((MLA_NOPE_DIM, C), F32), -ss, ss, jnp.zeros((pad, C), F32)], axis=0).T)
        c16.append(cc)
        s16.append(ss)

    def ret_rope(v, i):
        return v * cr[i] + pltpu.roll(v, RET_HEAD_DIM // 2, 1) * sr[i]

    pq = [_dot(x, win_ref[:, 0:RET_WIDTH]) for x in xn]
    pk = [_dot(x, win_ref[:, RET_WIDTH:2 * RET_WIDTH]) for x in xn]
    pv = [_dot(x, win_ref[:, 2 * RET_WIDTH:3 * RET_WIDTH]).astype(BF16) for x in xn]
    pg = [_dot(x, win_ref[:, 3 * RET_WIDTH:4 * RET_WIDTH]) for x in xn]

    head_cols = [slice(h * RET_HEAD_DIM, (h + 1) * RET_HEAD_DIM) for h in range(RET_HEADS)]
    rq = [[ret_rope(pq[i][:, cols], i).astype(BF16) for cols in head_cols] for i in ids]
    rk = [[(ret_rope(pk[i][:, cols], i) * (RET_HEAD_DIM ** -0.5)).astype(BF16) for cols in head_cols] for i in ids]
    for i in ids:
        for h, cols in enumerate(head_cols):
            q = rq[i][h]
            k = rk[i][h]
            v = pv[i][:, cols]
            scores = _dot_nt(q, k) * dmask_ref[h]
            inner = _dot(scores.astype(BF16), v)
            r_prev = state_ref[h]
            cross = _dot(q, r_prev.astype(BF16)) * xi_ref[h]
            kz = (k.astype(F32) * zeta_ref[h]).astype(BF16)
            state_ref[h] = g_chunk[h] * r_prev + _dot_tn(kz, v)
            y = inner + cross
            mu = jnp.mean(y, axis=-1, keepdims=True)
            yc = y - mu
            var = jnp.mean(yc * yc, axis=-1, keepdims=True)
            yn = yc * lax.rsqrt(var + EPS) * gnw_ref[:, cols]
            gate = pg[i][:, cols]
            ret_ref[0, halves[i], cols] = (gate * jax.nn.sigmoid(gate) * yn).astype(BF16)

    lane = lax.broadcasted_iota(jnp.int32, (C, LANES), 1)
    upper = lane >= (MLA_NOPE_DIM + MLA_ROPE_DIM // 2)

    def mla_rope(v, i):
        swapped = jnp.where(upper, pltpu.roll(v, MLA_ROPE_DIM // 2, 1),
                            pltpu.roll(v, LANES - MLA_ROPE_DIM // 2, 1))
        return v * cm[i] + swapped * sm[i]

    o = 4 * RET_WIDTH
    cq = [_dot(x, win_ref[:, o:IN_COLS_CQ_END]) for x in xn]
    cqn = [_rms(c, qn_ref[...]).astype(BF16) for c in cq]
    qt = [_dot_nt(wuq_ref[...], c) for c in cqn]
    half = MLA_ROPE_DIM // 2
    for h in range(MLA_HEADS):
        r0 = h * HEAD_SLAB
        for i in ids:
            x1 = qt[i][r0 + MLA_NOPE_DIM:r0 + MLA_NOPE_DIM + half]
            x2 = qt[i][r0 + MLA_NOPE_DIM + half:r0 + MLA_QK_DIM]
            slab = jnp.concatenate([qt[i][r0:r0 + MLA_NOPE_DIM], x1 * c16[i] - x2 * s16[i],
                                    x2 * c16[i] + x1 * s16[i], qt[i][r0 + MLA_QK_DIM:r0 + HEAD_SLAB]], axis=0)
            qt_ref[0, 0, r0:r0 + HEAD_SLAB, halves[i]] = (slab * q_scale).astype(BF16)

    ckv = [_dot(x, win_ref[:, IN_COLS_CQ_END + LANES:]) for x in xn]
    ckvn = [_rms(c, kvn_ref[...]).astype(BF16) for c in ckv]
    kr = [mla_rope(_dot(xn[i], win_ref[:, IN_COLS_CQ_END:IN_COLS_CQ_END + LANES]), i) for i in ids]
    kn = [_dot(c, wuk_ref[...]) for c in ckvn]
    for h in range(MLA_HEADS):
        sl = slice(h * HEAD_SLAB, (h + 1) * HEAD_SLAB)
        for i in ids:
            k_ref[0, halves[i], sl] = (kn[i][:, sl] + kr[i]).astype(BF16)
    for i in ids:
        vt_ref[0, 0, :, halves[i]] = _dot_nt(wvt_ref[...], ckvn[i]).astype(BF16)


def _in_proj(x, pre_mix_norm, w_in, q_norm, wuq_p, kv_norm, wuk_p, wvt, positions, ret_gn_w, cast_weights):
    B, S, D = x.shape
    tm = TM_PROJ
    nt = S // tm
    assert (B * nt) % CAST_SLABS == 0
    per = B * nt // CAST_SLABS
    for w in cast_weights:
        assert w.shape[0] % (CAST_SLABS * 16) == 0, "a row slab must be whole bf16 sublane tiles"
    slab = lambda w: pl.BlockSpec((w.shape[0] // CAST_SLABS, w.shape[1]), lambda b, i: ((b * nt + i) // per, 0))
    place = (jnp.arange(MLA_ROPE_DIM)[:, None] + MLA_NOPE_DIM == jnp.arange(LANES)[None, :]).astype(BF16)
    assert tm == TK, "attention reads one transposed query block per key-block-sized piece"
    half_r = RET_HEAD_DIM // 2
    half_m = MLA_ROPE_DIM // 2
    inv_r = 1.0 / (ROPE_BASE ** (jnp.arange(half_r, dtype=F32) / half_r))
    inv_m = 1.0 / (ROPE_BASE ** (jnp.arange(half_m, dtype=F32) / half_m))
    inv_r = jnp.broadcast_to(inv_r[:, None], (half_r, tm))
    inv_m = jnp.broadcast_to(inv_m[:, None], (half_m, tm))
    q_scale = (1.0 / math.sqrt(MLA_QK_DIM)) * math.log2(math.e)
    C = RET_CHUNK
    H = RET_HEADS
    log_g = jnp.log(1.0 - 2.0 ** (-5.0 - jnp.arange(H, dtype=F32)))
    j = jnp.arange(C, dtype=F32)
    diff = j[:, None] - j[None, :]
    dmask = jnp.where(diff[None] >= 0, jnp.exp(jnp.maximum(diff, 0.0)[None] * log_g[:, None, None]), 0.0)
    zeta = jnp.exp((C - 1 - j)[None, :] * log_g[:, None])
    xi = jnp.exp((j + 1)[None, :] * log_g[:, None])
    zeta_b = jnp.broadcast_to(zeta[:, :, None], (H, C, RET_HEAD_DIM))
    xi_b = jnp.broadcast_to(xi[:, :, None], (H, C, RET_HEAD_DIM))
    g_chunk = tuple(float((1.0 - 2.0 ** (-5.0 - h)) ** C) for h in range(H))
    const = lambda shape: pl.BlockSpec(shape, lambda b, i: (0,) * len(shape))
    tile = lambda w: pl.BlockSpec((1, tm, w), lambda b, i: (b, i, 0))
    bf = lambda w: jax.ShapeDtypeStruct((B, S, w), BF16)
    res = pl.pallas_call(
        partial(_in_proj_body, q_scale=q_scale, g_chunk=g_chunk, n_cast=len(cast_weights)),
        grid=(B, S // tm),
        in_specs=[tile(D), const((1, D)),
                  pl.BlockSpec((D, IN_COLS_ALL), lambda b, i: (0, 0), pipeline_mode=pl.Buffered(1)),
                  const((MLA_ROPE_DIM, LANES)),
                  const((1, MLA_Q_LORA)), const((MLA_HEADS * HEAD_SLAB, MLA_Q_LORA)),
                  const((1, MLA_KV_LORA)), const((MLA_KV_LORA, MLA_HEADS * HEAD_SLAB)),
                  const((MLA_WIDTH, MLA_KV_LORA)),
                  pl.BlockSpec((1, 1, tm), lambda b, i: (b, 0, i)), const((half_r, tm)), const((half_m, tm)),
                  const((H, C, C)), const((H, C, RET_HEAD_DIM)), const((H, C, RET_HEAD_DIM)),
                  const((1, RET_WIDTH))] + [slab(w) for w in cast_weights],
        out_specs=[tile(RET_WIDTH),
                   pl.BlockSpec((1, 1, MLA_HEADS * HEAD_SLAB, tm), lambda b, i: (b, i, 0, 0)),
                   tile(MLA_HEADS * HEAD_SLAB),
                   pl.BlockSpec((1, tm // TK, MLA_WIDTH, TK), lambda b, i: (b, i, 0, 0))]
        + [slab(w) for w in cast_weights],
        out_shape=[bf(RET_WIDTH),
                   jax.ShapeDtypeStruct((B, S // tm, MLA_HEADS * HEAD_SLAB, tm), BF16),
                   bf(MLA_HEADS * HEAD_SLAB),
                   jax.ShapeDtypeStruct((B, S // TK, MLA_WIDTH, TK), BF16)]
        + [jax.ShapeDtypeStruct(w.shape, BF16) for w in cast_weights],
        scratch_shapes=[pltpu.VMEM((H, RET_HEAD_DIM, RET_HEAD_DIM), F32),
                        pltpu.VMEM((D, IN_COLS_MAIN + LANES), BF16)],
        compiler_params=pltpu.CompilerParams(dimension_semantics=("arbitrary", "arbitrary"),
                                             vmem_limit_bytes=VMEM_LIMIT),
        name="in_proj",
    )(x, pre_mix_norm, w_in, place, q_norm, wuq_p, kv_norm, wuk_p, wvt,
      positions.reshape(B, 1, S), inv_r, inv_m,
      dmask, zeta_b, xi_b, ret_gn_w, *cast_weights)
    return res[:4], res[4:]


HEADS_PER_STEP = 2
PIECES = TQ // TK


def _attn_body(qt_ref, k_ref, vt_ref, o_ref, *scratch):
    n = 2 * HEADS_PER_STEP
    s_ref = {(i // HEADS_PER_STEP, i % HEADS_PER_STEP): scratch[i] for i in range(n)}
    mb_ref = {(i // HEADS_PER_STEP, i % HEADS_PER_STEP): scratch[n + i] for i in range(n)}
    nq = qt_ref.shape[1] // PIECES
    ones = jnp.ones((ONES_ROWS, TK), BF16)
    chains = [(hh, c) for c in range(PIECES) for hh in range(HEADS_PER_STEP)]
    row = lax.broadcasted_iota(jnp.int32, (TK, TK), 0)
    col = lax.broadcasted_iota(jnp.int32, (TK, TK), 1)
    tri = row <= col

    def score_piece(qstart, j, slot, hh, c):
        kstart = pl.multiple_of(j * TK, TK)
        cols = slice(hh * HEAD_SLAB, (hh + 1) * HEAD_SLAB)
        lanes = slice(c * TK, (c + 1) * TK)
        st = _dot(k_ref[0, pl.ds(kstart, TK), cols], qt_ref[0, qstart + c, cols, :])
        s_ref[slot, hh][:, lanes] = st
        mb_ref[slot, hh][:, lanes] = jnp.max(st, axis=0, keepdims=True)

    def acc_piece(j, slot, hh, c, state, masked):
        m, acc = state
        lanes = slice(c * TK, (c + 1) * TK)
        st = s_ref[slot, hh][:, lanes]
        if masked:
            st = jnp.where(tri, st, NEG_BIG)
            blk_max = jnp.max(st, axis=0, keepdims=True)
        else:
            blk_max = mb_ref[slot, hh][:, lanes]
        m_new = jnp.maximum(m, blk_max)
        p = jnp.exp2(st - m_new).astype(BF16)
        vte = jnp.concatenate([vt_ref[0, j, hh * MLA_V_DIM:(hh + 1) * MLA_V_DIM, :], ones], axis=0)
        return m_new, acc * jnp.exp2(m - m_new) + _dot(vte, p)

    def stage(carry, j, slot, first, diag, nxt_qstart, nxt_j, nxt_first):
        carry = dict(carry)
        for hh, c in chains:
            if c >= nxt_first:
                score_piece(nxt_qstart, nxt_j, 1 - slot, hh, c)
            if c >= first:
                carry[hh, c] = acc_piece(j, slot, hh, c, carry[hh, c], diag and c == first)
        return carry

    def tile(qi, _):
        qstart = qi * PIECES

        def pair(t, flat):
            carry = dict(zip(chains, flat))
            carry = stage(carry, 2 * t, 0, 0, False, qstart, 2 * t + 1, 0)
            carry = stage(carry, 2 * t + 1, 1, 0, False, qstart, 2 * t + 2, 0)
            return tuple(carry[ch] for ch in chains)

        init = tuple((jnp.full((1, TK), NEG_BIG, F32), jnp.zeros((MLA_V_DIM + ONES_ROWS, TK), F32))
                     for _ in chains)
        d0 = PIECES * qi
        carry = dict(zip(chains, lax.fori_loop(0, d0 // 2, pair, init)))
        for d in range(PIECES):
            if d + 1 < PIECES:
                carry = stage(carry, d0 + d, d % 2, d, True, qstart, d0 + d + 1, d + 1)
            else:
                carry = stage(carry, d0 + d, d % 2, d, True, jnp.minimum(qi + 1, nq - 1) * PIECES, 0, 0)
        for hh, c in chains:
            acc = carry[hh, c][1]
            o_ref[0, qi, hh * MLA_V_DIM:(hh + 1) * MLA_V_DIM, c * TK:(c + 1) * TK] = (
                acc[:MLA_V_DIM] / acc[MLA_V_DIM:MLA_V_DIM + 1]).astype(BF16)
        return 0

    for hh, c in chains:
        score_piece(0, 0, 0, hh, c)
    lax.fori_loop(0, nq, tile, 0)


def _mla_attention(qt, k, vt):
    B, S, _ = k.shape
    nkb = S // TK
    nq = S // TQ
    hp = HEADS_PER_STEP
    return pl.pallas_call(
        _attn_body,
        grid=(B, MLA_HEADS // hp),
        in_specs=[pl.BlockSpec((1, nkb, hp * HEAD_SLAB, TK), lambda b, g: (b, 0, g, 0)),
                  pl.BlockSpec((1, S, hp * HEAD_SLAB), lambda b, g: (b, 0, g)),
                  pl.BlockSpec((1, nkb, hp * MLA_V_DIM, TK), lambda b, g: (b, 0, g, 0))],
        out_specs=pl.BlockSpec((1, nq, hp * MLA_V_DIM, TQ), lambda b, g: (b, 0, g, 0)),
        out_shape=jax.ShapeDtypeStruct((B, nq, MLA_WIDTH, TQ), BF16),
        scratch_shapes=[pltpu.VMEM((TK, TQ), F32)] * (2 * hp) + [pltpu.VMEM((1, TQ), F32)] * (2 * hp),
        compiler_params=pltpu.CompilerParams(dimension_semantics=("arbitrary", "arbitrary"),
                                             vmem_limit_bytes=VMEM_LIMIT),
        name="mla_attn",
    )(qt, k, vt)


def _post_body(ret_ref, mla_ref, x_ref, p_ref, wo_ref, pmn_ref, pfn_ref, wg_ref, wu_ref, wd_ref,
               pofn_ref, wpp_ref, plen_ref, wpg_ref, bpg_ref, o_ref, act_ref):
    tm = x_ref.shape[1]
    halves = [slice(0, tm // 2), slice(tm // 2, tm)]
    mix = [_dot(ret_ref[0, r], wo_ref[0:RET_WIDTH, :]) + _dot_tn(mla_ref[0, 0][:, r], wo_ref[RET_WIDTH:, :])
           for r in halves]
    h1 = [x_ref[0, r] + _rms(m, pmn_ref[...]) for r, m in zip(halves, mix)]
    hn = [_rms(h, pfn_ref[...]).astype(BF16) for h in h1]
    for c in range(D_FF // FF_CHUNK):
        cols = slice(c * FF_CHUNK, (c + 1) * FF_CHUNK)
        for r, hnr in zip(halves, hn):
            g = _dot(hnr, wg_ref[:, cols])
            u = _dot(hnr, wu_ref[:, cols])
            act_ref[r, cols] = (g * jax.nn.sigmoid(g) * u).astype(BF16)
    ff = [_dot(act_ref[r, :], wd_ref[...]) for r in halves]
    h2 = [h + _rms(f, pofn_ref[...]) for h, f in zip(h1, ff)]
    e = [_rms(_dot(p_ref[0, r].astype(BF16), wpp_ref[...]), plen_ref[...]) for r in halves]
    gate = [jax.nn.sigmoid(_dot(h.astype(BF16), wpg_ref[...]) + bpg_ref[...]) for h in h2]
    for r, h, ee, gg in zip(halves, h2, e, gate):
        o_ref[0, r] = h + ee * gg


def _post(ret_out, mla_out, x, p, w_o, post_mix_norm, pre_ffn_norm, w_gate, w_up, w_down,
          post_ffn_norm, w_ple_proj, ple_norm, w_ple_gate, b_ple_gate):
    B, S, D = x.shape
    tm = TM_POST
    const = lambda shape: pl.BlockSpec(shape, lambda b, i: (0,) * len(shape),
                                       pipeline_mode=pl.Buffered(1))
    tile = lambda w: pl.BlockSpec((1, tm, w), lambda b, i: (b, i, 0))
    return pl.pallas_call(
        _post_body,
        grid=(B, S // tm),
        in_specs=[tile(RET_WIDTH),
                  pl.BlockSpec((1, 1, MLA_WIDTH, tm), lambda b, i: (b, i // (TQ // tm), 0, i % (TQ // tm))),
                  tile(D), tile(PLE_DIM),
                  const((D, D)), const((1, D)), const((1, D)),
                  const((D, D_FF)), const((D, D_FF)), const((D_FF, D)), const((1, D)),
                  const((PLE_DIM, D)), const((1, D)), const((D, D)), const((1, D))],
        out_specs=tile(D),
        out_shape=jax.ShapeDtypeStruct((B, S, D), F32),
        scratch_shapes=[pltpu.VMEM((tm, D_FF), BF16)],
        compiler_params=pltpu.CompilerParams(dimension_semantics=("arbitrary", "arbitrary"),
                                             vmem_limit_bytes=VMEM_LIMIT),
        name="post",
    )(ret_out, mla_out, x, p, w_o, post_mix_norm, pre_ffn_norm, w_gate, w_up, w_down,
      post_ffn_norm, w_ple_proj, ple_norm, w_ple_gate, b_ple_gate)


def _pad_heads(w, per_head, keep):
    K = w.shape[0]
    w = w.reshape(K, MLA_HEADS, per_head)[:, :, :keep]
    w = jnp.pad(w, ((0, 0), (0, 0), (0, HEAD_SLAB - keep)))
    return w.reshape(K, MLA_HEADS * HEAD_SLAB).astype(BF16)


def kernel(x, p, positions, pre_mix_norm, w_in, ret_gn_w, mla_q_norm, w_uq, mla_kv_norm, w_ukv, w_o,
           post_mix_norm, pre_ffn_norm, w_gate, w_up, w_down, post_ffn_norm, w_ple_proj, ple_norm,
           w_ple_gate, b_ple_gate):
    depth = pre_mix_norm.shape[0]
    h = x
    for i in range(depth):
        big = (w_o[i], w_gate[i], w_up[i], w_down[i], w_ple_proj[i], w_ple_gate[i])
        wuq_p = _pad_heads(w_uq[i], MLA_QK_DIM, MLA_QK_DIM).T
        wuk_p = _pad_heads(w_ukv[i], MLA_NOPE_DIM + MLA_V_DIM, MLA_NOPE_DIM)
        wv = w_ukv[i].reshape(MLA_KV_LORA, MLA_HEADS, MLA_NOPE_DIM + MLA_V_DIM)[:, :, MLA_NOPE_DIM:]
        wvt = wv.reshape(MLA_KV_LORA, MLA_WIDTH).T.astype(BF16)
        (ret_out, q, k, vt), big = _in_proj(h, pre_mix_norm[i][None], w_in[i], mla_q_norm[i][None], wuq_p,
                                            mla_kv_norm[i][None], wuk_p, wvt, positions, ret_gn_w[i][None], big)
        wo_b, wg_b, wu_b, wd_b, wpp_b, wpg_b = big
        mla_out = _mla_attention(q, k, vt)
        h = _post(ret_out, mla_out, h, p[i], wo_b, post_mix_norm[i][None], pre_ffn_norm[i][None],
                  wg_b, wu_b, wd_b, post_ffn_norm[i][None], wpp_b, ple_norm[i][None], wpg_b,
                  b_ple_gate[i][None])
    return h
```

```python
import math
from functools import partial

import jax
import jax.numpy as jnp
from jax import lax
from jax.experimental import pallas as pl
from jax.experimental.pallas import tpu as pltpu

D_MODEL = 1024
PLE_DIM = 256
RET_HEADS = 4
RET_HEAD_DIM = 128
RET_WIDTH = RET_HEADS * RET_HEAD_DIM
MLA_HEADS = 8
MLA_NOPE_DIM = 64
MLA_ROPE_DIM = 32
MLA_QK_DIM = MLA_NOPE_DIM + MLA_ROPE_DIM
MLA_V_DIM = 64
MLA_WIDTH = MLA_HEADS * MLA_V_DIM
MLA_Q_LORA = 384
MLA_KV_LORA = 256
D_FF = 2816
ROPE_BASE = 10000.0
EPS = 1e-6
NEG_BIG = -1e30

LANES = 128
HEAD_SLAB = 128
IN_COLS_CQ_END = 4 * RET_WIDTH + MLA_Q_LORA
IN_COLS_MAIN = IN_COLS_CQ_END + MLA_KV_LORA
IN_COLS_ALL = IN_COLS_MAIN + MLA_ROPE_DIM
VMEM_LIMIT = 58 * 1024 * 1024

TM_PROJ = 512
RET_CHUNK = 256
TQ = 2048
TK = 512
ONES_ROWS = 16
TM_POST = 1024
FF_CHUNK = 256

BF16 = jnp.bfloat16
F32 = jnp.float32


def _dot(a, b):
    return jnp.dot(a, b, preferred_element_type=F32)


def _dot_nt(a, b):
    return lax.dot_general(a, b, (((1,), (1,)), ((), ())), preferred_element_type=F32)


def _dot_tn(a, b):
    return lax.dot_general(a, b, (((0,), (0,)), ((), ())), preferred_element_type=F32)


def _rms(v, w):
    return v * lax.rsqrt(jnp.mean(v * v, axis=-1, keepdims=True) + EPS) * w


N_PROJ_INPUTS = 16
CAST_SLABS = 16


def _in_proj_body(*refs, q_scale, g_chunk, n_cast):
    (x_ref, g_ref, win32_ref, place_ref, qn_ref, wuq_ref, kvn_ref, wuk_ref, wvt_ref,
     pos_ref, invr_ref, invm_ref, dmask_ref, zeta_ref, xi_ref, gnw_ref) = refs[:N_PROJ_INPUTS]
    w32_refs = refs[N_PROJ_INPUTS:N_PROJ_INPUTS + n_cast]
    ret_ref, qt_ref, k_ref, vt_ref = refs[N_PROJ_INPUTS + n_cast:N_PROJ_INPUTS + n_cast + 4]
    w16_refs = refs[N_PROJ_INPUTS + n_cast + 4:N_PROJ_INPUTS + 2 * n_cast + 4]
    state_ref, win_ref = refs[N_PROJ_INPUTS + 2 * n_cast + 4:]

    for w32_ref, w16_ref in zip(w32_refs, w16_refs):
        w16_ref[...] = w32_ref[...].astype(BF16)

    @pl.when((pl.program_id(0) == 0) & (pl.program_id(1) == 0))
    def _():
        a, b = IN_COLS_CQ_END, IN_COLS_MAIN
        for c0 in range(0, b, LANES):
            dst = c0 if c0 < a else c0 + LANES
            win_ref[:, dst:dst + LANES] = win32_ref[c0:c0 + LANES, :].T.astype(BF16)
        tail = win32_ref[IN_COLS_ALL - LANES:IN_COLS_ALL, :].T.astype(BF16)
        win_ref[:, a:a + LANES] = _dot(tail, place_ref[...]).astype(BF16)

    @pl.when(pl.program_id(1) == 0)
    def _():
        state_ref[...] = jnp.zeros_like(state_ref)

    C = RET_CHUNK
    halves = [slice(i * C, (i + 1) * C) for i in range(x_ref.shape[1] // C)]
    ids = range(len(halves))
    xn = [_rms(x_ref[0, r], g_ref[...]).astype(BF16) for r in halves]
    cr, sr, cm, sm, c16, s16 = [], [], [], [], [], []
    pad = HEAD_SLAB - MLA_QK_DIM
    for r in halves:
        pos = pos_ref[0, :, r].astype(F32)
        ang_r = invr_ref[:, r] * pos
        c, s = jnp.cos(ang_r), jnp.sin(ang_r)
        ang_m = invm_ref[:, r] * pos
        cc, ss = jnp.cos(ang_m), jnp.sin(ang_m)
        cr.append(jnp.concatenate([c, c], axis=0).T)
        sr.append(jnp.concatenate([-s, s], axis=0).T)
        cm.append(jnp.concatenate([jnp.ones((MLA_NOPE_DIM, C), F32), cc, cc, jnp.zeros((pad, C), F32)], axis=0).T)
        sm.append(jnp.concatenate([jnp.zeros((MLA_NOPE_DIM, C), F32), -ss, ss, jnp.zeros((pad, C), F32)], axis=0).T)
        c16.append(cc)
        s16.append(ss)

    def ret_rope(v, i):
        return v * cr[i] + pltpu.roll(v, RET_HEAD_DIM // 2, 1) * sr[i]

    pq = [_dot(x, win_ref[:, 0:RET_WIDTH]) for x in xn]
    pk = [_dot(x, win_ref[:, RET_WIDTH:2 * RET_WIDTH]) for x in xn]
    pv = [_dot(x, win_ref[:, 2 * RET_WIDTH:3 * RET_WIDTH]).astype(BF16) for x in xn]
    pg = [_dot(x, win_ref[:, 3 * RET_WIDTH:4 * RET_WIDTH]) for x in xn]

    head_cols = [slice(h * RET_HEAD_DIM, (h + 1) * RET_HEAD_DIM) for h in range(RET_HEADS)]
    rq = [[ret_rope(pq[i][:, cols], i).astype(BF16) for cols in head_cols] for i in ids]
    rk = [[(ret_rope(pk[i][:, cols], i) * (RET_HEAD_DIM ** -0.5)).astype(BF16) for cols in head_cols] for i in ids]
    for i in ids:
        for h, cols in enumerate(head_cols):
            q = rq[i][h]
            k = rk[i][h]
            v = pv[i][:, cols]
            scores = _dot_nt(q, k) * dmask_ref[h]
            inner = _dot(scores.astype(BF16), v)
            r_prev = state_ref[h]
            cross = _dot(q, r_prev.astype(BF16)) * xi_ref[h]
            kz = (k.astype(F32) * zeta_ref[h]).astype(BF16)
            state_ref[h] = g_chunk[h] * r_prev + _dot_tn(kz, v)
            y = inner + cross
            mu = jnp.mean(y, axis=-1, keepdims=True)
            yc = y - mu
            var = jnp.mean(yc * yc, axis=-1, keepdims=True)
            yn = yc * lax.rsqrt(var + EPS) * gnw_ref[:, cols]
            gate = pg[i][:, cols]
            ret_ref[0, halves[i], cols] = (gate * jax.nn.sigmoid(gate) * yn).astype(BF16)

    lane = lax.broadcasted_iota(jnp.int32, (C, LANES), 1)
    upper = lane >= (MLA_NOPE_DIM + MLA_ROPE_DIM // 2)

    def mla_rope(v, i):
        swapped = jnp.where(upper, pltpu.roll(v, MLA_ROPE_DIM // 2, 1),
                            pltpu.roll(v, LANES - MLA_ROPE_DIM // 2, 1))
        return v * cm[i] + swapped * sm[i]

    o = 4 * RET_WIDTH
    cq = [_dot(x, win_ref[:, o:IN_COLS_CQ_END]) for x in xn]
    cqn = [_rms(c, qn_ref[...]).astype(BF16) for c in cq]
    qt = [_dot_nt(wuq_ref[...], c) for c in cqn]
    half = MLA_ROPE_DIM // 2
    for h in range(MLA_HEADS):
        r0 = h * HEAD_SLAB
        for i in ids:
            x1 = qt[i][r0 + MLA_NOPE_DIM:r0 + MLA_NOPE_DIM + half]
            x2 = qt[i][r0 + MLA_NOPE_DIM + half:r0 + MLA_QK_DIM]
            slab = jnp.concatenate([qt[i][r0:r0 + MLA_NOPE_DIM], x1 * c16[i] - x2 * s16[i],
                                    x2 * c16[i] + x1 * s16[i], qt[i][r0 + MLA_QK_DIM:r0 + HEAD_SLAB]], axis=0)
            qt_ref[0, 0, r0:r0 + HEAD_SLAB, halves[i]] = (slab * q_scale).astype(BF16)

    ckv = [_dot(x, win_ref[:, IN_COLS_CQ_END + LANES:]) for x in xn]
    ckvn = [_rms(c, kvn_ref[...]).astype(BF16) for c in ckv]
    kr = [mla_rope(_dot(xn[i], win_ref[:, IN_COLS_CQ_END:IN_COLS_CQ_END + LANES]), i) for i in ids]
    kn = [_dot(c, wuk_ref[...]) for c in ckvn]
    for h in range(MLA_HEADS):
        sl = slice(h * HEAD_SLAB, (h + 1) * HEAD_SLAB)
        for i in ids:
            k_ref[0, halves[i], sl] = (kn[i][:, sl] + kr[i]).astype(BF16)
    for i in ids:
        vt_ref[0, 0, :, halves[i]] = _dot_nt(wvt_ref[...], ckvn[i]).astype(BF16)


def _in_proj(x, pre_mix_norm, w_in, q_norm, wuq_p, kv_norm, wuk_p, wvt, positions, ret_gn_w, cast_weights):
    B, S, D = x.shape
    tm = TM_PROJ
    nt = S // tm
    assert (B * nt) % CAST_SLABS == 0
    per = B * nt // CAST_SLABS
    for w in cast_weights:
        assert w.shape[0] % (CAST_SLABS * 16) == 0, "a row slab must be whole bf16 sublane tiles"
    slab = lambda w: pl.BlockSpec((w.shape[0] // CAST_SLABS, w.shape[1]), lambda b, i: ((b * nt + i) // per, 0))
    place = (jnp.arange(LANES)[:, None] - (LANES - MLA_ROPE_DIM) + MLA_NOPE_DIM == jnp.arange(LANES)[None, :])
    place = (place & (jnp.arange(LANES)[:, None] >= LANES - MLA_ROPE_DIM)).astype(BF16)
    assert tm == TK, "attention reads one transposed query block per key-block-sized piece"
    half_r = RET_HEAD_DIM // 2
    half_m = MLA_ROPE_DIM // 2
    inv_r = 1.0 / (ROPE_BASE ** (jnp.arange(half_r, dtype=F32) / half_r))
    inv_m = 1.0 / (ROPE_BASE ** (jnp.arange(half_m, dtype=F32) / half_m))
    inv_r = jnp.broadcast_to(inv_r[:, None], (half_r, tm))
    inv_m = jnp.broadcast_to(inv_m[:, None], (half_m, tm))
    q_scale = (1.0 / math.sqrt(MLA_QK_DIM)) * math.log2(math.e)
    C = RET_CHUNK
    H = RET_HEADS
    log_g = jnp.log(1.0 - 2.0 ** (-5.0 - jnp.arange(H, dtype=F32)))
    j = jnp.arange(C, dtype=F32)
    diff = j[:, None] - j[None, :]
    dmask = jnp.where(diff[None] >= 0, jnp.exp(jnp.maximum(diff, 0.0)[None] * log_g[:, None, None]), 0.0)
    zeta = jnp.exp((C - 1 - j)[None, :] * log_g[:, None])
    xi = jnp.exp((j + 1)[None, :] * log_g[:, None])
    zeta_b = jnp.broadcast_to(zeta[:, :, None], (H, C, RET_HEAD_DIM))
    xi_b = jnp.broadcast_to(xi[:, :, None], (H, C, RET_HEAD_DIM))
    g_chunk = tuple(float((1.0 - 2.0 ** (-5.0 - h)) ** C) for h in range(H))
    const = lambda shape: pl.BlockSpec(shape, lambda b, i: (0,) * len(shape))
    tile = lambda w: pl.BlockSpec((1, tm, w), lambda b, i: (b, i, 0))
    bf = lambda w: jax.ShapeDtypeStruct((B, S, w), BF16)
    res = pl.pallas_call(
        partial(_in_proj_body, q_scale=q_scale, g_chunk=g_chunk, n_cast=len(cast_weights)),
        grid=(B, S // tm),
        in_specs=[tile(D), const((1, D)),
                  pl.BlockSpec((IN_COLS_ALL, D), lambda b, i: (0, 0), pipeline_mode=pl.Buffered(1)),
                  const((LANES, LANES)),
                  const((1, MLA_Q_LORA)), const((MLA_HEADS * HEAD_SLAB, MLA_Q_LORA)),
                  const((1, MLA_KV_LORA)), const((MLA_KV_LORA, MLA_HEADS * HEAD_SLAB)),
                  const((MLA_WIDTH, MLA_KV_LORA)),
                  pl.BlockSpec((1, 1, tm), lambda b, i: (b, 0, i)), const((half_r, tm)), const((half_m, tm)),
                  const((H, C, C)), const((H, C, RET_HEAD_DIM)), const((H, C, RET_HEAD_DIM)),
                  const((1, RET_WIDTH))] + [slab(w) for w in cast_weights],
        out_specs=[tile(RET_WIDTH),
                   pl.BlockSpec((1, 1, MLA_HEADS * HEAD_SLAB, tm), lambda b, i: (b, i, 0, 0)),
                   tile(MLA_HEADS * HEAD_SLAB),
                   pl.BlockSpec((1, tm // TK, MLA_WIDTH, TK), lambda b, i: (b, i, 0, 0))]
        + [slab(w) for w in cast_weights],
        out_shape=[bf(RET_WIDTH),
                   jax.ShapeDtypeStruct((B, S // tm, MLA_HEADS * HEAD_SLAB, tm), BF16),
                   bf(MLA_HEADS * HEAD_SLAB),
                   jax.ShapeDtypeStruct((B, S // TK, MLA_WIDTH, TK), BF16)]
        + [jax.ShapeDtypeStruct(w.shape, BF16) for w in cast_weights],
        scratch_shapes=[pltpu.VMEM((H, RET_HEAD_DIM, RET_HEAD_DIM), F32),
                        pltpu.VMEM((D, IN_COLS_MAIN + LANES), BF16)],
        compiler_params=pltpu.CompilerParams(dimension_semantics=("arbitrary", "arbitrary"),
                                             vmem_limit_bytes=VMEM_LIMIT),
        name="in_proj",
    )(x, pre_mix_norm, w_in, place, q_norm, wuq_p, kv_norm, wuk_p, wvt,
      positions.reshape(B, 1, S), inv_r, inv_m,
      dmask, zeta_b, xi_b, ret_gn_w, *cast_weights)
    return res[:4], res[4:]


HEADS_PER_STEP = 2
PIECES = TQ // TK


def _attn_body(qt_ref, k_ref, vt_ref, o_ref, *scratch):
    n = 2 * HEADS_PER_STEP
    s_ref = {(i // HEADS_PER_STEP, i % HEADS_PER_STEP): scratch[i] for i in range(n)}
    mb_ref = {(i // HEADS_PER_STEP, i % HEADS_PER_STEP): scratch[n + i] for i in range(n)}
    nq = qt_ref.shape[1] // PIECES
    ones = jnp.ones((ONES_ROWS, TK), BF16)
    chains = [(hh, c) for c in range(PIECES) for hh in range(HEADS_PER_STEP)]
    row = lax.broadcasted_iota(jnp.int32, (TK, TK), 0)
    col = lax.broadcasted_iota(jnp.int32, (TK, TK), 1)
    tri = row <= col

    def score_piece(qstart, j, slot, hh, c):
        kstart = pl.multiple_of(j * TK, TK)
        cols = slice(hh * HEAD_SLAB, (hh + 1) * HEAD_SLAB)
        lanes = slice(c * TK, (c + 1) * TK)
        st = _dot(k_ref[0, pl.ds(kstart, TK), cols], qt_ref[0, qstart + c, cols, :])
        s_ref[slot, hh][:, lanes] = st
        mb_ref[slot, hh][:, lanes] = jnp.max(st, axis=0, keepdims=True)

    def acc_piece(j, slot, hh, c, state, masked):
        m, acc = state
        lanes = slice(c * TK, (c + 1) * TK)
        st = s_ref[slot, hh][:, lanes]
        if masked:
            st = jnp.where(tri, st, NEG_BIG)
            blk_max = jnp.max(st, axis=0, keepdims=True)
        else:
            blk_max = mb_ref[slot, hh][:, lanes]
        m_new = jnp.maximum(m, blk_max)
        p = jnp.exp2(st - m_new).astype(BF16)
        vte = jnp.concatenate([vt_ref[0, j, hh * MLA_V_DIM:(hh + 1) * MLA_V_DIM, :], ones], axis=0)
        return m_new, acc * jnp.exp2(m - m_new) + _dot(vte, p)

    def stage(carry, j, slot, first, diag, nxt_qstart, nxt_j, nxt_first):
        carry = dict(carry)
        for hh, c in chains:
            if c >= nxt_first:
                score_piece(nxt_qstart, nxt_j, 1 - slot, hh, c)
            if c >= first:
                carry[hh, c] = acc_piece(j, slot, hh, c, carry[hh, c], diag and c == first)
        return carry

    def tile(qi, _):
        qstart = qi * PIECES

        def pair(t, flat):
            carry = dict(zip(chains, flat))
            carry = stage(carry, 2 * t, 0, 0, False, qstart, 2 * t + 1, 0)
            carry = stage(carry, 2 * t + 1, 1, 0, False, qstart, 2 * t + 2, 0)
            return tuple(carry[ch] for ch in chains)

        init = tuple((jnp.full((1, TK), NEG_BIG, F32), jnp.zeros((MLA_V_DIM + ONES_ROWS, TK), F32))
                     for _ in chains)
        d0 = PIECES * qi
        carry = dict(zip(chains, lax.fori_loop(0, d0 // 2, pair, init)))
        for d in range(PIECES):
            if d + 1 < PIECES:
                carry = stage(carry, d0 + d, d % 2, d, True, qstart, d0 + d + 1, d + 1)
            else:
                carry = stage(carry, d0 + d, d % 2, d, True, jnp.minimum(qi + 1, nq - 1) * PIECES, 0, 0)
        for hh, c in chains:
            acc = carry[hh, c][1]
            o_ref[0, qi, hh * MLA_V_DIM:(hh + 1) * MLA_V_DIM, c * TK:(c + 1) * TK] = (
                acc[:MLA_V_DIM] / acc[MLA_V_DIM:MLA_V_DIM + 1]).astype(BF16)
        return 0

    for hh, c in chains:
        score_piece(0, 0, 0, hh, c)
    lax.fori_loop(0, nq, tile, 0)


def _mla_attention(qt, k, vt):
    B, S, _ = k.shape
    nkb = S // TK
    nq = S // TQ
    hp = HEADS_PER_STEP
    return pl.pallas_call(
        _attn_body,
        grid=(B, MLA_HEADS // hp),
        in_specs=[pl.BlockSpec((1, nkb, hp * HEAD_SLAB, TK), lambda b, g: (b, 0, g, 0)),
                  pl.BlockSpec((1, S, hp * HEAD_SLAB), lambda b, g: (b, 0, g)),
                  pl.BlockSpec((1, nkb, hp * MLA_V_DIM, TK), lambda b, g: (b, 0, g, 0))],
        out_specs=pl.BlockSpec((1, nq, hp * MLA_V_DIM, TQ), lambda b, g: (b, 0, g, 0)),
        out_shape=jax.ShapeDtypeStruct((B, nq, MLA_WIDTH, TQ), BF16),
        scratch_shapes=[pltpu.VMEM((TK, TQ), F32)] * (2 * hp) + [pltpu.VMEM((1, TQ), F32)] * (2 * hp),
        compiler_params=pltpu.CompilerParams(dimension_semantics=("arbitrary", "arbitrary"),
                                             vmem_limit_bytes=VMEM_LIMIT),
        name="mla_attn",
    )(qt, k, vt)


def _post_body(ret_ref, mla_ref, x_ref, p_ref, wo_ref, pmn_ref, pfn_ref, wg_ref, wu_ref, wd_ref,
               pofn_ref, wpp_ref, plen_ref, wpg_ref, bpg_ref, o_ref, act_ref):
    tm = x_ref.shape[1]
    halves = [slice(0, tm // 2), slice(tm // 2, tm)]
    mix = [_dot(ret_ref[0, r], wo_ref[0:RET_WIDTH, :]) + _dot_tn(mla_ref[0, 0][:, r], wo_ref[RET_WIDTH:, :])
           for r in halves]
    h1 = [x_ref[0, r] + _rms(m, pmn_ref[...]) for r, m in zip(halves, mix)]
    hn = [_rms(h, pfn_ref[...]).astype(BF16) for h in h1]
    for c in range(D_FF // FF_CHUNK):
        cols = slice(c * FF_CHUNK, (c + 1) * FF_CHUNK)
        for r, hnr in zip(halves, hn):
            g = _dot(hnr, wg_ref[:, cols])
            u = _dot(hnr, wu_ref[:, cols])
            act_ref[r, cols] = (g * jax.nn.sigmoid(g) * u).astype(BF16)
    ff = [_dot(act_ref[r, :], wd_ref[...]) for r in halves]
    h2 = [h + _rms(f, pofn_ref[...]) for h, f in zip(h1, ff)]
    e = [_rms(_dot(p_ref[0, r].astype(BF16), wpp_ref[...]), plen_ref[...]) for r in halves]
    gate = [jax.nn.sigmoid(_dot(h.astype(BF16), wpg_ref[...]) + bpg_ref[...]) for h in h2]
    for r, h, ee, gg in zip(halves, h2, e, gate):
        o_ref[0, r] = h + ee * gg


def _post(ret_out, mla_out, x, p, w_o, post_mix_norm, pre_ffn_norm, w_gate, w_up, w_down,
          post_ffn_norm, w_ple_proj, ple_norm, w_ple_gate, b_ple_gate):
    B, S, D = x.shape
    tm = TM_POST
    const = lambda shape: pl.BlockSpec(shape, lambda b, i: (0,) * len(shape),
                                       pipeline_mode=pl.Buffered(1))
    tile = lambda w: pl.BlockSpec((1, tm, w), lambda b, i: (b, i, 0))
    return pl.pallas_call(
        _post_body,
        grid=(B, S // tm),
        in_specs=[tile(RET_WIDTH),
                  pl.BlockSpec((1, 1, MLA_WIDTH, tm), lambda b, i: (b, i // (TQ // tm), 0, i % (TQ // tm))),
                  tile(D), tile(PLE_DIM),
                  const((D, D)), const((1, D)), const((1, D)),
                  const((D, D_FF)), const((D, D_FF)), const((D_FF, D)), const((1, D)),
                  const((PLE_DIM, D)), const((1, D)), const((D, D)), const((1, D))],
        out_specs=tile(D),
        out_shape=jax.ShapeDtypeStruct((B, S, D), F32),
        scratch_shapes=[pltpu.VMEM((tm, D_FF), BF16)],
        compiler_params=pltpu.CompilerParams(dimension_semantics=("arbitrary", "arbitrary"),
                                             vmem_limit_bytes=VMEM_LIMIT),
        name="post",
    )(ret_out, mla_out, x, p, w_o, post_mix_norm, pre_ffn_norm, w_gate, w_up, w_down,
      post_ffn_norm, w_ple_proj, ple_norm, w_ple_gate, b_ple_gate)


def _pad_heads(w, per_head, keep):
    K = w.shape[0]
    w = w.reshape(K, MLA_HEADS, per_head)[:, :, :keep]
    w = jnp.pad(w, ((0, 0), (0, 0), (0, HEAD_SLAB - keep)))
    return w.reshape(K, MLA_HEADS * HEAD_SLAB).astype(BF16)


def kernel(x, p, positions, pre_mix_norm, w_in, ret_gn_w, mla_q_norm, w_uq, mla_kv_norm, w_ukv, w_o,
           post_mix_norm, pre_ffn_norm, w_gate, w_up, w_down, post_ffn_norm, w_ple_proj, ple_norm,
           w_ple_gate, b_ple_gate):
    depth = pre_mix_norm.shape[0]
    h = x
    for i in range(depth):
        big = (w_o[i], w_gate[i], w_up[i], w_down[i], w_ple_proj[i], w_ple_gate[i])
        wuq_p = _pad_heads(w_uq[i], MLA_QK_DIM, MLA_QK_DIM).T
        wuk_p = _pad_heads(w_ukv[i], MLA_NOPE_DIM + MLA_V_DIM, MLA_NOPE_DIM)
        wv = w_ukv[i].reshape(MLA_KV_LORA, MLA_HEADS, MLA_NOPE_DIM + MLA_V_DIM)[:, :, MLA_NOPE_DIM:]
        wvt = wv.reshape(MLA_KV_LORA, MLA_WIDTH).T.astype(BF16)
        (ret_out, q, k, vt), big = _in_proj(h, pre_mix_norm[i][None], w_in[i].T, mla_q_norm[i][None], wuq_p,
                                            mla_kv_norm[i][None], wuk_p, wvt, positions, ret_gn_w[i][None], big)
        wo_b, wg_b, wu_b, wd_b, wpp_b, wpg_b = big
        mla_out = _mla_attention(q, k, vt)
        h = _post(ret_out, mla_out, h, p[i], wo_b, post_mix_norm[i][None], pre_ffn_norm[i][None],
                  wg_b, wu_b, wd_b, post_ffn_norm[i][None], wpp_b, ple_norm[i][None], wpg_b,
                  b_ple_gate[i][None])
    return h
```

```python
import math
from functools import partial

import numpy as np
import jax
import jax.numpy as jnp
from jax import lax
from jax.experimental import pallas as pl
from jax.experimental.pallas import tpu as pltpu

D_MODEL = 1024
PLE_DIM = 256
RET_HEADS = 4
RET_HEAD_DIM = 128
RET_WIDTH = RET_HEADS * RET_HEAD_DIM
MLA_HEADS = 8
MLA_NOPE_DIM = 64
MLA_ROPE_DIM = 32
MLA_QK_DIM = MLA_NOPE_DIM + MLA_ROPE_DIM
MLA_V_DIM = 64
MLA_WIDTH = MLA_HEADS * MLA_V_DIM
MLA_Q_LORA = 384
MLA_KV_LORA = 256
D_FF = 2816
ROPE_BASE = 10000.0
EPS = 1e-6
NEG_BIG = -1e30

LANES = 128
HEAD_SLAB = 128
IN_COLS_CQ_END = 4 * RET_WIDTH + MLA_Q_LORA
IN_COLS_MAIN = IN_COLS_CQ_END + MLA_KV_LORA
IN_COLS_ALL = IN_COLS_MAIN + MLA_ROPE_DIM
VMEM_LIMIT = 58 * 1024 * 1024

TM_PROJ = 512
RET_CHUNK = 256
TQ = 2048
TK = 512
ONES_ROWS = 16
TM_POST = 1024
FF_CHUNK = 256

BF16 = jnp.bfloat16
F32 = jnp.float32


def _dot(a, b):
    return jnp.dot(a, b, preferred_element_type=F32)


def _dot_nt(a, b):
    return lax.dot_general(a, b, (((1,), (1,)), ((), ())), preferred_element_type=F32)


def _dot_tn(a, b):
    return lax.dot_general(a, b, (((0,), (0,)), ((), ())), preferred_element_type=F32)


def _rms(v, w):
    return v * lax.rsqrt(jnp.mean(v * v, axis=-1, keepdims=True) + EPS) * w


N_PROJ_INPUTS = 16
CAST_SLABS = 16


def _in_proj_body(*refs, q_scale, g_chunk, n_cast):
    (x_ref, g_ref, win32_ref, place_ref, qn_ref, wuq_ref, kvn_ref, wuk_ref, wvt_ref,
     pos_ref, invr_ref, invm_ref, dmask_ref, zeta_ref, xi_ref, gnw_ref) = refs[:N_PROJ_INPUTS]
    w32_refs = refs[N_PROJ_INPUTS:N_PROJ_INPUTS + n_cast]
    ret_ref, qt_ref, k_ref, vt_ref = refs[N_PROJ_INPUTS + n_cast:N_PROJ_INPUTS + n_cast + 4]
    w16_refs = refs[N_PROJ_INPUTS + n_cast + 4:N_PROJ_INPUTS + 2 * n_cast + 4]
    state_ref, win_ref = refs[N_PROJ_INPUTS + 2 * n_cast + 4:]

    for w32_ref, w16_ref in zip(w32_refs, w16_refs):
        w16_ref[...] = w32_ref[...].astype(BF16)

    @pl.when((pl.program_id(0) == 0) & (pl.program_id(1) == 0))
    def _():
        a, b = IN_COLS_CQ_END, IN_COLS_MAIN
        for c0 in range(0, b, LANES):
            dst = c0 if c0 < a else c0 + LANES
            win_ref[:, dst:dst + LANES] = win32_ref[c0:c0 + LANES, :].T.astype(BF16)
        tail = win32_ref[IN_COLS_ALL - LANES:IN_COLS_ALL, :].T.astype(BF16)
        win_ref[:, a:a + LANES] = _dot(tail, place_ref[...]).astype(BF16)

    @pl.when(pl.program_id(1) == 0)
    def _():
        state_ref[...] = jnp.zeros_like(state_ref)

    C = RET_CHUNK
    halves = [slice(i * C, (i + 1) * C) for i in range(x_ref.shape[1] // C)]
    ids = range(len(halves))
    xn = [_rms(x_ref[0, r], g_ref[...]).astype(BF16) for r in halves]
    cr, sr, cm, sm, c16, s16 = [], [], [], [], [], []
    pad = HEAD_SLAB - MLA_QK_DIM
    for r in halves:
        pos = pos_ref[0, :, r].astype(F32)
        ang_r = invr_ref[:, r] * pos
        c, s = jnp.cos(ang_r), jnp.sin(ang_r)
        ang_m = invm_ref[:, r] * pos
        cc, ss = jnp.cos(ang_m), jnp.sin(ang_m)
        cr.append(jnp.concatenate([c, c], axis=0).T)
        sr.append(jnp.concatenate([-s, s], axis=0).T)
        cm.append(jnp.concatenate([jnp.ones((MLA_NOPE_DIM, C), F32), cc, cc, jnp.zeros((pad, C), F32)], axis=0).T)
        sm.append(jnp.concatenate([jnp.zeros((MLA_NOPE_DIM, C), F32), -ss, ss, jnp.zeros((pad, C), F32)], axis=0).T)
        c16.append(cc)
        s16.append(ss)

    def ret_rope(v, i):
        return v * cr[i] + pltpu.roll(v, RET_HEAD_DIM // 2, 1) * sr[i]

    pq = [_dot(x, win_ref[:, 0:RET_WIDTH]) for x in xn]
    pk = [_dot(x, win_ref[:, RET_WIDTH:2 * RET_WIDTH]) for x in xn]
    pv = [_dot(x, win_ref[:, 2 * RET_WIDTH:3 * RET_WIDTH]).astype(BF16) for x in xn]
    pg = [_dot(x, win_ref[:, 3 * RET_WIDTH:4 * RET_WIDTH]) for x in xn]

    head_cols = [slice(h * RET_HEAD_DIM, (h + 1) * RET_HEAD_DIM) for h in range(RET_HEADS)]
    rq = [[ret_rope(pq[i][:, cols], i).astype(BF16) for cols in head_cols] for i in ids]
    rk = [[(ret_rope(pk[i][:, cols], i) * (RET_HEAD_DIM ** -0.5)).astype(BF16) for cols in head_cols] for i in ids]
    for i in ids:
        for h, cols in enumerate(head_cols):
            q = rq[i][h]
            k = rk[i][h]
            v = pv[i][:, cols]
            scores = _dot_nt(q, k) * dmask_ref[h]
            inner = _dot(scores.astype(BF16), v)
            r_prev = state_ref[h]
            cross = _dot(q, r_prev.astype(BF16)) * xi_ref[h]
            kz = (k.astype(F32) * zeta_ref[h]).astype(BF16)
            state_ref[h] = g_chunk[h] * r_prev + _dot_tn(kz, v)
            y = inner + cross
            mu = jnp.mean(y, axis=-1, keepdims=True)
            yc = y - mu
            var = jnp.mean(yc * yc, axis=-1, keepdims=True)
            yn = yc * lax.rsqrt(var + EPS) * gnw_ref[:, cols]
            gate = pg[i][:, cols]
            ret_ref[0, halves[i], cols] = (gate * jax.nn.sigmoid(gate) * yn).astype(BF16)

    lane = lax.broadcasted_iota(jnp.int32, (C, LANES), 1)
    upper = lane >= (MLA_NOPE_DIM + MLA_ROPE_DIM // 2)

    def mla_rope(v, i):
        swapped = jnp.where(upper, pltpu.roll(v, MLA_ROPE_DIM // 2, 1),
                            pltpu.roll(v, LANES - MLA_ROPE_DIM // 2, 1))
        return v * cm[i] + swapped * sm[i]

    o = 4 * RET_WIDTH
    cq = [_dot(x, win_ref[:, o:IN_COLS_CQ_END]) for x in xn]
    cqn = [_rms(c, qn_ref[...]).astype(BF16) for c in cq]
    qt = [_dot_nt(wuq_ref[...], c) for c in cqn]
    half = MLA_ROPE_DIM // 2
    for h in range(MLA_HEADS):
        r0 = h * HEAD_SLAB
        for i in ids:
            x1 = qt[i][r0 + MLA_NOPE_DIM:r0 + MLA_NOPE_DIM + half]
            x2 = qt[i][r0 + MLA_NOPE_DIM + half:r0 + MLA_QK_DIM]
            slab = jnp.concatenate([qt[i][r0:r0 + MLA_NOPE_DIM], x1 * c16[i] - x2 * s16[i],
                                    x2 * c16[i] + x1 * s16[i], qt[i][r0 + MLA_QK_DIM:r0 + HEAD_SLAB]], axis=0)
            qt_ref[0, 0, r0:r0 + HEAD_SLAB, halves[i]] = (slab * q_scale).astype(BF16)

    ckv = [_dot(x, win_ref[:, IN_COLS_CQ_END + LANES:]) for x in xn]
    ckvn = [_rms(c, kvn_ref[...]).astype(BF16) for c in ckv]
    kr = [mla_rope(_dot(xn[i], win_ref[:, IN_COLS_CQ_END:IN_COLS_CQ_END + LANES]), i) for i in ids]
    kn = [_dot(c, wuk_ref[...]) for c in ckvn]
    for h in range(MLA_HEADS):
        sl = slice(h * HEAD_SLAB, (h + 1) * HEAD_SLAB)
        for i in ids:
            k_ref[0, halves[i], sl] = (kn[i][:, sl] + kr[i]).astype(BF16)
    for i in ids:
        vt_ref[0, 0, :, halves[i]] = _dot_nt(wvt_ref[...], ckvn[i]).astype(BF16)


def _in_proj(x, pre_mix_norm, w_in, q_norm, wuq_p, kv_norm, wuk_p, wvt, positions, ret_gn_w, cast_weights):
    B, S, D = x.shape
    tm = TM_PROJ
    nt = S // tm
    assert (B * nt) % CAST_SLABS == 0
    per = B * nt // CAST_SLABS
    for w in cast_weights:
        assert w.shape[0] % (CAST_SLABS * 16) == 0, "a row slab must be whole bf16 sublane tiles"
    slab = lambda w: pl.BlockSpec((w.shape[0] // CAST_SLABS, w.shape[1]), lambda b, i: ((b * nt + i) // per, 0))
    rows = np.arange(LANES)[:, None]
    place = (rows - (LANES - MLA_ROPE_DIM) + MLA_NOPE_DIM == np.arange(LANES)[None, :]) & (rows >= LANES - MLA_ROPE_DIM)
    place = jnp.asarray(place, BF16)
    assert tm == TK, "attention reads one transposed query block per key-block-sized piece"
    half_r = RET_HEAD_DIM // 2
    half_m = MLA_ROPE_DIM // 2
    f32 = np.float32
    inv_r = f32(1.0) / (f32(ROPE_BASE) ** (np.arange(half_r, dtype=f32) / f32(half_r)))
    inv_m = f32(1.0) / (f32(ROPE_BASE) ** (np.arange(half_m, dtype=f32) / f32(half_m)))
    inv_r = jnp.asarray(np.ascontiguousarray(np.broadcast_to(inv_r[:, None], (half_r, tm))))
    inv_m = jnp.asarray(np.ascontiguousarray(np.broadcast_to(inv_m[:, None], (half_m, tm))))
    q_scale = (1.0 / math.sqrt(MLA_QK_DIM)) * math.log2(math.e)
    C = RET_CHUNK
    H = RET_HEADS
    log_g = np.log(f32(1.0) - f32(2.0) ** (f32(-5.0) - np.arange(H, dtype=f32))).astype(f32)
    j = np.arange(C, dtype=f32)
    diff = j[:, None] - j[None, :]
    dmask = np.where(diff[None] >= 0, np.exp(np.maximum(diff, f32(0.0))[None] * log_g[:, None, None]), f32(0.0))
    zeta = np.exp((f32(C - 1) - j)[None, :] * log_g[:, None])
    xi = np.exp((j + f32(1.0))[None, :] * log_g[:, None])
    dmask = jnp.asarray(dmask.astype(f32))
    zeta_b = jnp.asarray(np.ascontiguousarray(np.broadcast_to(zeta.astype(f32)[:, :, None], (H, C, RET_HEAD_DIM))))
    xi_b = jnp.asarray(np.ascontiguousarray(np.broadcast_to(xi.astype(f32)[:, :, None], (H, C, RET_HEAD_DIM))))
    g_chunk = tuple(float((1.0 - 2.0 ** (-5.0 - h)) ** C) for h in range(H))
    const = lambda shape: pl.BlockSpec(shape, lambda b, i: (0,) * len(shape))
    tile = lambda w: pl.BlockSpec((1, tm, w), lambda b, i: (b, i, 0))
    bf = lambda w: jax.ShapeDtypeStruct((B, S, w), BF16)
    res = pl.pallas_call(
        partial(_in_proj_body, q_scale=q_scale, g_chunk=g_chunk, n_cast=len(cast_weights)),
        grid=(B, S // tm),
        in_specs=[tile(D), const((1, D)),
                  pl.BlockSpec((IN_COLS_ALL, D), lambda b, i: (0, 0), pipeline_mode=pl.Buffered(1)),
                  const((LANES, LANES)),
                  const((1, MLA_Q_LORA)), const((MLA_HEADS * HEAD_SLAB, MLA_Q_LORA)),
                  const((1, MLA_KV_LORA)), const((MLA_KV_LORA, MLA_HEADS * HEAD_SLAB)),
                  const((MLA_WIDTH, MLA_KV_LORA)),
                  pl.BlockSpec((1, 1, tm), lambda b, i: (b, 0, i)), const((half_r, tm)), const((half_m, tm)),
                  const((H, C, C)), const((H, C, RET_HEAD_DIM)), const((H, C, RET_HEAD_DIM)),
                  const((1, RET_WIDTH))] + [slab(w) for w in cast_weights],
        out_specs=[tile(RET_WIDTH),
                   pl.BlockSpec((1, 1, MLA_HEADS * HEAD_SLAB, tm), lambda b, i: (b, i, 0, 0)),
                   tile(MLA_HEADS * HEAD_SLAB),
                   pl.BlockSpec((1, tm // TK, MLA_WIDTH, TK), lambda b, i: (b, i, 0, 0))]
        + [slab(w) for w in cast_weights],
        out_shape=[bf(RET_WIDTH),
                   jax.ShapeDtypeStruct((B, S // tm, MLA_HEADS * HEAD_SLAB, tm), BF16),
                   bf(MLA_HEADS * HEAD_SLAB),
                   jax.ShapeDtypeStruct((B, S // TK, MLA_WIDTH, TK), BF16)]
        + [jax.ShapeDtypeStruct(w.shape, BF16) for w in cast_weights],
        scratch_shapes=[pltpu.VMEM((H, RET_HEAD_DIM, RET_HEAD_DIM), F32),
                        pltpu.VMEM((D, IN_COLS_MAIN + LANES), BF16)],
        compiler_params=pltpu.CompilerParams(dimension_semantics=("arbitrary", "arbitrary"),
                                             vmem_limit_bytes=VMEM_LIMIT),
        name="in_proj",
    )(x, pre_mix_norm, w_in, place, q_norm, wuq_p, kv_norm, wuk_p, wvt,
      positions.reshape(B, 1, S), inv_r, inv_m,
      dmask, zeta_b, xi_b, ret_gn_w, *cast_weights)
    return res[:4], res[4:]


HEADS_PER_STEP = 2
PIECES = TQ // TK


def _attn_body(qt_ref, k_ref, vt_ref, o_ref, *scratch):
    n = 2 * HEADS_PER_STEP
    s_ref = {(i // HEADS_PER_STEP, i % HEADS_PER_STEP): scratch[i] for i in range(n)}
    mb_ref = {(i // HEADS_PER_STEP, i % HEADS_PER_STEP): scratch[n + i] for i in range(n)}
    nq = qt_ref.shape[1] // PIECES
    ones = jnp.ones((ONES_ROWS, TK), BF16)
    chains = [(hh, c) for c in range(PIECES) for hh in range(HEADS_PER_STEP)]
    row = lax.broadcasted_iota(jnp.int32, (TK, TK), 0)
    col = lax.broadcasted_iota(jnp.int32, (TK, TK), 1)
    tri = row <= col

    def score_piece(qstart, j, slot, hh, c):
        kstart = pl.multiple_of(j * TK, TK)
        cols = slice(hh * HEAD_SLAB, (hh + 1) * HEAD_SLAB)
        lanes = slice(c * TK, (c + 1) * TK)
        st = _dot(k_ref[0, pl.ds(kstart, TK), cols], qt_ref[0, qstart + c, cols, :])
        s_ref[slot, hh][:, lanes] = st
        mb_ref[slot, hh][:, lanes] = jnp.max(st, axis=0, keepdims=True)

    def acc_piece(j, slot, hh, c, state, masked):
        m, acc = state
        lanes = slice(c * TK, (c + 1) * TK)
        st = s_ref[slot, hh][:, lanes]
        if masked:
            st = jnp.where(tri, st, NEG_BIG)
            blk_max = jnp.max(st, axis=0, keepdims=True)
        else:
            blk_max = mb_ref[slot, hh][:, lanes]
        m_new = jnp.maximum(m, blk_max)
        p = jnp.exp2(st - m_new).astype(BF16)
        vte = jnp.concatenate([vt_ref[0, j, hh * MLA_V_DIM:(hh + 1) * MLA_V_DIM, :], ones], axis=0)
        return m_new, acc * jnp.exp2(m - m_new) + _dot(vte, p)

    def stage(carry, j, slot, first, diag, nxt_qstart, nxt_j, nxt_first):
        carry = dict(carry)
        for hh, c in chains:
            if c >= nxt_first:
                score_piece(nxt_qstart, nxt_j, 1 - slot, hh, c)
            if c >= first:
                carry[hh, c] = acc_piece(j, slot, hh, c, carry[hh, c], diag and c == first)
        return carry

    def tile(qi, _):
        qstart = qi * PIECES

        def pair(t, flat):
            carry = dict(zip(chains, flat))
            carry = stage(carry, 2 * t, 0, 0, False, qstart, 2 * t + 1, 0)
            carry = stage(carry, 2 * t + 1, 1, 0, False, qstart, 2 * t + 2, 0)
            return tuple(carry[ch] for ch in chains)

        init = tuple((jnp.full((1, TK), NEG_BIG, F32), jnp.zeros((MLA_V_DIM + ONES_ROWS, TK), F32))
                     for _ in chains)
        d0 = PIECES * qi
        carry = dict(zip(chains, lax.fori_loop(0, d0 // 2, pair, init)))
        for d in range(PIECES):
            if d + 1 < PIECES:
                carry = stage(carry, d0 + d, d % 2, d, True, qstart, d0 + d + 1, d + 1)
            else:
                carry = stage(carry, d0 + d, d % 2, d, True, jnp.minimum(qi + 1, nq - 1) * PIECES, 0, 0)
        for hh, c in chains:
            acc = carry[hh, c][1]
            o_ref[0, qi, hh * MLA_V_DIM:(hh + 1) * MLA_V_DIM, c * TK:(c + 1) * TK] = (
                acc[:MLA_V_DIM] / acc[MLA_V_DIM:MLA_V_DIM + 1]).astype(BF16)
        return 0

    for hh, c in chains:
        score_piece(0, 0, 0, hh, c)
    lax.fori_loop(0, nq, tile, 0)


def _mla_attention(qt, k, vt):
    B, S, _ = k.shape
    nkb = S // TK
    nq = S // TQ
    hp = HEADS_PER_STEP
    return pl.pallas_call(
        _attn_body,
        grid=(B, MLA_HEADS // hp),
        in_specs=[pl.BlockSpec((1, nkb, hp * HEAD_SLAB, TK), lambda b, g: (b, 0, g, 0)),
                  pl.BlockSpec((1, S, hp * HEAD_SLAB), lambda b, g: (b, 0, g)),
                  pl.BlockSpec((1, nkb, hp * MLA_V_DIM, TK), lambda b, g: (b, 0, g, 0))],
        out_specs=pl.BlockSpec((1, nq, hp * MLA_V_DIM, TQ), lambda b, g: (b, 0, g, 0)),
        out_shape=jax.ShapeDtypeStruct((B, nq, MLA_WIDTH, TQ), BF16),
        scratch_shapes=[pltpu.VMEM((TK, TQ), F32)] * (2 * hp) + [pltpu.VMEM((1, TQ), F32)] * (2 * hp),
        compiler_params=pltpu.CompilerParams(dimension_semantics=("arbitrary", "arbitrary"),
                                             vmem_limit_bytes=VMEM_LIMIT),
        name="mla_attn",
    )(qt, k, vt)


def _post_body(ret_ref, mla_ref, x_ref, p_ref, wo_ref, pmn_ref, pfn_ref, wg_ref, wu_ref, wd_ref,
               pofn_ref, wpp_ref, plen_ref, wpg_ref, bpg_ref, o_ref, act_ref):
    tm = x_ref.shape[1]
    halves = [slice(0, tm // 2), slice(tm // 2, tm)]
    mix = [_dot(ret_ref[0, r], wo_ref[0:RET_WIDTH, :]) + _dot_tn(mla_ref[0, 0][:, r], wo_ref[RET_WIDTH:, :])
           for r in halves]
    h1 = [x_ref[0, r] + _rms(m, pmn_ref[...]) for r, m in zip(halves, mix)]
    hn = [_rms(h, pfn_ref[...]).astype(BF16) for h in h1]
    for c in range(D_FF // FF_CHUNK):
        cols = slice(c * FF_CHUNK, (c + 1) * FF_CHUNK)
        for r, hnr in zip(halves, hn):
            g = _dot(hnr, wg_ref[:, cols])
            u = _dot(hnr, wu_ref[:, cols])
            act_ref[r, cols] = (g * jax.nn.sigmoid(g) * u).astype(BF16)
    ff = [_dot(act_ref[r, :], wd_ref[...]) for r in halves]
    h2 = [h + _rms(f, pofn_ref[...]) for h, f in zip(h1, ff)]
    e = [_rms(_dot(p_ref[0, r].astype(BF16), wpp_ref[...]), plen_ref[...]) for r in halves]
    gate = [jax.nn.sigmoid(_dot(h.astype(BF16), wpg_ref[...]) + bpg_ref[...]) for h in h2]
    for r, h, ee, gg in zip(halves, h2, e, gate):
        o_ref[0, r] = h + ee * gg


def _post(ret_out, mla_out, x, p, w_o, post_mix_norm, pre_ffn_norm, w_gate, w_up, w_down,
          post_ffn_norm, w_ple_proj, ple_norm, w_ple_gate, b_ple_gate):
    B, S, D = x.shape
    tm = TM_POST
    const = lambda shape: pl.BlockSpec(shape, lambda b, i: (0,) * len(shape),
                                       pipeline_mode=pl.Buffered(1))
    tile = lambda w: pl.BlockSpec((1, tm, w), lambda b, i: (b, i, 0))
    return pl.pallas_call(
        _post_body,
        grid=(B, S // tm),
        in_specs=[tile(RET_WIDTH),
                  pl.BlockSpec((1, 1, MLA_WIDTH, tm), lambda b, i: (b, i // (TQ // tm), 0, i % (TQ // tm))),
                  tile(D), tile(PLE_DIM),
                  const((D, D)), const((1, D)), const((1, D)),
                  const((D, D_FF)), const((D, D_FF)), const((D_FF, D)), const((1, D)),
                  const((PLE_DIM, D)), const((1, D)), const((D, D)), const((1, D))],
        out_specs=tile(D),
        out_shape=jax.ShapeDtypeStruct((B, S, D), F32),
        scratch_shapes=[pltpu.VMEM((tm, D_FF), BF16)],
        compiler_params=pltpu.CompilerParams(dimension_semantics=("arbitrary", "arbitrary"),
                                             vmem_limit_bytes=VMEM_LIMIT),
        name="post",
    )(ret_out, mla_out, x, p, w_o, post_mix_norm, pre_ffn_norm, w_gate, w_up, w_down,
      post_ffn_norm, w_ple_proj, ple_norm, w_ple_gate, b_ple_gate)


def _pad_heads(w, per_head, keep):
    K = w.shape[0]
    w = w.reshape(K, MLA_HEADS, per_head)[:, :, :keep]
    w = jnp.pad(w, ((0, 0), (0, 0), (0, HEAD_SLAB - keep)))
    return w.reshape(K, MLA_HEADS * HEAD_SLAB).astype(BF16)


def kernel(x, p, positions, pre_mix_norm, w_in, ret_gn_w, mla_q_norm, w_uq, mla_kv_norm, w_ukv, w_o,
           post_mix_norm, pre_ffn_norm, w_gate, w_up, w_down, post_ffn_norm, w_ple_proj, ple_norm,
           w_ple_gate, b_ple_gate):
    depth = pre_mix_norm.shape[0]
    h = x
    for i in range(depth):
        big = (w_o[i], w_gate[i], w_up[i], w_down[i], w_ple_proj[i], w_ple_gate[i])
        wuq_p = _pad_heads(w_uq[i], MLA_QK_DIM, MLA_QK_DIM).T
        wuk_p = _pad_heads(w_ukv[i], MLA_NOPE_DIM + MLA_V_DIM, MLA_NOPE_DIM)
        wv = w_ukv[i].reshape(MLA_KV_LORA, MLA_HEADS, MLA_NOPE_DIM + MLA_V_DIM)[:, :, MLA_NOPE_DIM:]
        wvt = wv.reshape(MLA_KV_LORA, MLA_WIDTH).T.astype(BF16)
        (ret_out, q, k, vt), big = _in_proj(h, pre_mix_norm[i][None], w_in[i].T, mla_q_norm[i][None], wuq_p,
                                            mla_kv_norm[i][None], wuk_p, wvt, positions, ret_gn_w[i][None], big)
        wo_b, wg_b, wu_b, wd_b, wpp_b, wpg_b = big
        mla_out = _mla_attention(q, k, vt)
        h = _post(ret_out, mla_out, h, p[i], wo_b, post_mix_norm[i][None], pre_ffn_norm[i][None],
                  wg_b, wu_b, wd_b, post_ffn_norm[i][None], wpp_b, ple_norm[i][None], wpg_b,
                  b_ple_gate[i][None])
    return h
```

```python
import math
from functools import partial

import numpy as np
import jax
import jax.numpy as jnp
from jax import lax
from jax.experimental import pallas as pl
from jax.experimental.pallas import tpu as pltpu

D_MODEL = 1024
PLE_DIM = 256
RET_HEADS = 4
RET_HEAD_DIM = 128
RET_WIDTH = RET_HEADS * RET_HEAD_DIM
MLA_HEADS = 8
MLA_NOPE_DIM = 64
MLA_ROPE_DIM = 32
MLA_QK_DIM = MLA_NOPE_DIM + MLA_ROPE_DIM
MLA_V_DIM = 64
MLA_WIDTH = MLA_HEADS * MLA_V_DIM
MLA_Q_LORA = 384
MLA_KV_LORA = 256
D_FF = 2816
ROPE_BASE = 10000.0
EPS = 1e-6
NEG_BIG = -1e30

LANES = 128
HEAD_SLAB = 128
IN_COLS_CQ_END = 4 * RET_WIDTH + MLA_Q_LORA
IN_COLS_MAIN = IN_COLS_CQ_END + MLA_KV_LORA
IN_COLS_ALL = IN_COLS_MAIN + MLA_ROPE_DIM
VMEM_LIMIT = 58 * 1024 * 1024

TM_PROJ = 512
RET_CHUNK = 256
TQ = 2048
TK = 512
ONES_ROWS = 16
TM_POST = 1024
FF_CHUNK = 256

BF16 = jnp.bfloat16
F32 = jnp.float32


def _dot(a, b):
    return jnp.dot(a, b, preferred_element_type=F32)


def _dot_nt(a, b):
    return lax.dot_general(a, b, (((1,), (1,)), ((), ())), preferred_element_type=F32)


def _dot_tn(a, b):
    return lax.dot_general(a, b, (((0,), (0,)), ((), ())), preferred_element_type=F32)


def _rms(v, w):
    return v * lax.rsqrt(jnp.mean(v * v, axis=-1, keepdims=True) + EPS) * w


N_PROJ_INPUTS = 16
CAST_SLABS = 16


def _in_proj_body(*refs, q_scale, g_chunk, n_cast):
    (x_ref, g_ref, win32_ref, place_ref, qn_ref, wuq_ref, kvn_ref, wuk_ref, wvt_ref,
     pos_ref, invr_ref, invm_ref, dmask_ref, zeta_ref, xi_ref, gnw_ref) = refs[:N_PROJ_INPUTS]
    w32_refs = refs[N_PROJ_INPUTS:N_PROJ_INPUTS + n_cast]
    ret_ref, qt_ref, k_ref, vt_ref = refs[N_PROJ_INPUTS + n_cast:N_PROJ_INPUTS + n_cast + 4]
    w16_refs = refs[N_PROJ_INPUTS + n_cast + 4:N_PROJ_INPUTS + 2 * n_cast + 4]
    state_ref, win_ref = refs[N_PROJ_INPUTS + 2 * n_cast + 4:]

    for w32_ref, w16_ref in zip(w32_refs, w16_refs):
        w16_ref[...] = w32_ref[...].astype(BF16)

    @pl.when((pl.program_id(0) == 0) & (pl.program_id(1) == 0))
    def _():
        a, b = IN_COLS_CQ_END, IN_COLS_MAIN
        for c0 in range(0, b, LANES):
            dst = c0 if c0 < a else c0 + LANES
            win_ref[:, dst:dst + LANES] = win32_ref[c0:c0 + LANES, :].T.astype(BF16)
        tail = win32_ref[IN_COLS_ALL - LANES:IN_COLS_ALL, :].T.astype(BF16)
        win_ref[:, a:a + LANES] = _dot(tail, place_ref[...]).astype(BF16)

    @pl.when(pl.program_id(1) == 0)
    def _():
        state_ref[...] = jnp.zeros_like(state_ref)

    C = RET_CHUNK
    halves = [slice(i * C, (i + 1) * C) for i in range(x_ref.shape[1] // C)]
    ids = range(len(halves))
    xn = [_rms(x_ref[0, r], g_ref[...]).astype(BF16) for r in halves]
    cr, sr, cm, sm, c16, s16 = [], [], [], [], [], []
    pad = HEAD_SLAB - MLA_QK_DIM
    for r in halves:
        pos = pos_ref[0, :, r].astype(F32)
        ang_r = invr_ref[:, r] * pos
        c, s = jnp.cos(ang_r), jnp.sin(ang_r)
        ang_m = invm_ref[:, r] * pos
        cc, ss = jnp.cos(ang_m), jnp.sin(ang_m)
        cr.append(jnp.concatenate([c, c], axis=0).T)
        sr.append(jnp.concatenate([-s, s], axis=0).T)
        cm.append(jnp.concatenate([jnp.ones((MLA_NOPE_DIM, C), F32), cc, cc, jnp.zeros((pad, C), F32)], axis=0).T)
        sm.append(jnp.concatenate([jnp.zeros((MLA_NOPE_DIM, C), F32), -ss, ss, jnp.zeros((pad, C), F32)], axis=0).T)
        c16.append(cc)
        s16.append(ss)

    def ret_rope(v, i):
        return v * cr[i] + pltpu.roll(v, RET_HEAD_DIM // 2, 1) * sr[i]

    pq = [_dot(x, win_ref[:, 0:RET_WIDTH]) for x in xn]
    pk = [_dot(x, win_ref[:, RET_WIDTH:2 * RET_WIDTH]) for x in xn]
    pv = [_dot(x, win_ref[:, 2 * RET_WIDTH:3 * RET_WIDTH]).astype(BF16) for x in xn]
    pg = [_dot(x, win_ref[:, 3 * RET_WIDTH:4 * RET_WIDTH]) for x in xn]

    head_cols = [slice(h * RET_HEAD_DIM, (h + 1) * RET_HEAD_DIM) for h in range(RET_HEADS)]
    rq = [[ret_rope(pq[i][:, cols], i).astype(BF16) for cols in head_cols] for i in ids]
    rk = [[(ret_rope(pk[i][:, cols], i) * (RET_HEAD_DIM ** -0.5)).astype(BF16) for cols in head_cols] for i in ids]
    for i in ids:
        for h, cols in enumerate(head_cols):
            q = rq[i][h]
            k = rk[i][h]
            v = pv[i][:, cols]
            scores = _dot_nt(q, k) * dmask_ref[h]
            inner = _dot(scores.astype(BF16), v)
            r_prev = state_ref[h]
            cross = _dot(q, r_prev.astype(BF16)) * xi_ref[h]
            kz = (k.astype(F32) * zeta_ref[h]).astype(BF16)
            state_ref[h] = g_chunk[h] * r_prev + _dot_tn(kz, v)
            y = inner + cross
            mu = jnp.mean(y, axis=-1, keepdims=True)
            yc = y - mu
            var = jnp.mean(yc * yc, axis=-1, keepdims=True)
            yn = yc * lax.rsqrt(var + EPS) * gnw_ref[:, cols]
            gate = pg[i][:, cols]
            ret_ref[0, halves[i], cols] = (gate * jax.nn.sigmoid(gate) * yn).astype(BF16)

    lane = lax.broadcasted_iota(jnp.int32, (C, LANES), 1)
    upper = lane >= (MLA_NOPE_DIM + MLA_ROPE_DIM // 2)

    def mla_rope(v, i):
        swapped = jnp.where(upper, pltpu.roll(v, MLA_ROPE_DIM // 2, 1),
                            pltpu.roll(v, LANES - MLA_ROPE_DIM // 2, 1))
        return v * cm[i] + swapped * sm[i]

    o = 4 * RET_WIDTH
    cq = [_dot(x, win_ref[:, o:IN_COLS_CQ_END]) for x in xn]
    cqn = [_rms(c, qn_ref[...]).astype(BF16) for c in cq]
    qt = [_dot_nt(wuq_ref[...], c) for c in cqn]
    half = MLA_ROPE_DIM // 2
    for h in range(MLA_HEADS):
        r0 = h * HEAD_SLAB
        for i in ids:
            x1 = qt[i][r0 + MLA_NOPE_DIM:r0 + MLA_NOPE_DIM + half]
            x2 = qt[i][r0 + MLA_NOPE_DIM + half:r0 + MLA_QK_DIM]
            slab = jnp.concatenate([qt[i][r0:r0 + MLA_NOPE_DIM], x1 * c16[i] - x2 * s16[i],
                                    x2 * c16[i] + x1 * s16[i], qt[i][r0 + MLA_QK_DIM:r0 + HEAD_SLAB]], axis=0)
            qt_ref[0, 0, r0:r0 + HEAD_SLAB, halves[i]] = (slab * q_scale).astype(BF16)

    ckv = [_dot(x, win_ref[:, IN_COLS_CQ_END + LANES:]) for x in xn]
    ckvn = [_rms(c, kvn_ref[...]).astype(BF16) for c in ckv]
    kr = [mla_rope(_dot(xn[i], win_ref[:, IN_COLS_CQ_END:IN_COLS_CQ_END + LANES]), i) for i in ids]
    kn = [_dot(c, wuk_ref[...]) for c in ckvn]
    for h in range(MLA_HEADS):
        sl = slice(h * HEAD_SLAB, (h + 1) * HEAD_SLAB)
        for i in ids:
            k_ref[0, halves[i], sl] = (kn[i][:, sl] + kr[i]).astype(BF16)
    for i in ids:
        vt_ref[0, 0, :, halves[i]] = _dot_nt(wvt_ref[...], ckvn[i]).astype(BF16)


def _in_proj(x, pre_mix_norm, w_in, q_norm, wuq_p, kv_norm, wuk_p, wvt, positions, ret_gn_w, cast_weights):
    B, S, D = x.shape
    tm = TM_PROJ
    nt = S // tm
    assert (B * nt) % CAST_SLABS == 0
    per = B * nt // CAST_SLABS
    for w in cast_weights:
        assert w.shape[0] % (CAST_SLABS * 16) == 0, "a row slab must be whole bf16 sublane tiles"
    slab = lambda w: pl.BlockSpec((w.shape[0] // CAST_SLABS, w.shape[1]), lambda b, i: ((b * nt + i) // per, 0))
    rows = np.arange(LANES)[:, None]
    place = (rows - (LANES - MLA_ROPE_DIM) + MLA_NOPE_DIM == np.arange(LANES)[None, :]) & (rows >= LANES - MLA_ROPE_DIM)
    place = jnp.asarray(place, BF16)
    assert tm == TK, "attention reads one transposed query block per key-block-sized piece"
    half_r = RET_HEAD_DIM // 2
    half_m = MLA_ROPE_DIM // 2
    f32 = np.float32
    inv_r = f32(1.0) / (f32(ROPE_BASE) ** (np.arange(half_r, dtype=f32) / f32(half_r)))
    inv_m = f32(1.0) / (f32(ROPE_BASE) ** (np.arange(half_m, dtype=f32) / f32(half_m)))
    inv_r = jnp.asarray(np.ascontiguousarray(np.broadcast_to(inv_r[:, None], (half_r, tm))))
    inv_m = jnp.asarray(np.ascontiguousarray(np.broadcast_to(inv_m[:, None], (half_m, tm))))
    q_scale = (1.0 / math.sqrt(MLA_QK_DIM)) * math.log2(math.e)
    C = RET_CHUNK
    H = RET_HEADS
    log_g = np.log(f32(1.0) - f32(2.0) ** (f32(-5.0) - np.arange(H, dtype=f32))).astype(f32)
    j = np.arange(C, dtype=f32)
    diff = j[:, None] - j[None, :]
    dmask = np.where(diff[None] >= 0, np.exp(np.maximum(diff, f32(0.0))[None] * log_g[:, None, None]), f32(0.0))
    zeta = np.exp((f32(C - 1) - j)[None, :] * log_g[:, None])
    xi = np.exp((j + f32(1.0))[None, :] * log_g[:, None])
    dmask = jnp.asarray(dmask.astype(f32))
    zeta_b = jnp.asarray(np.ascontiguousarray(np.broadcast_to(zeta.astype(f32)[:, :, None], (H, C, RET_HEAD_DIM))))
    xi_b = jnp.asarray(np.ascontiguousarray(np.broadcast_to(xi.astype(f32)[:, :, None], (H, C, RET_HEAD_DIM))))
    g_chunk = tuple(float((1.0 - 2.0 ** (-5.0 - h)) ** C) for h in range(H))
    const = lambda shape: pl.BlockSpec(shape, lambda b, i: (0,) * len(shape))
    tile = lambda w: pl.BlockSpec((1, tm, w), lambda b, i: (b, i, 0))
    bf = lambda w: jax.ShapeDtypeStruct((B, S, w), BF16)
    res = pl.pallas_call(
        partial(_in_proj_body, q_scale=q_scale, g_chunk=g_chunk, n_cast=len(cast_weights)),
        grid=(B, S // tm),
        in_specs=[tile(D), const((1, D)),
                  pl.BlockSpec((IN_COLS_ALL, D), lambda b, i: (0, 0), pipeline_mode=pl.Buffered(1)),
                  const((LANES, LANES)),
                  const((1, MLA_Q_LORA)), const((MLA_HEADS * HEAD_SLAB, MLA_Q_LORA)),
                  const((1, MLA_KV_LORA)), const((MLA_KV_LORA, MLA_HEADS * HEAD_SLAB)),
                  const((MLA_WIDTH, MLA_KV_LORA)),
                  pl.BlockSpec((1, 1, tm), lambda b, i: (b, 0, i)), const((half_r, tm)), const((half_m, tm)),
                  const((H, C, C)), const((H, C, RET_HEAD_DIM)), const((H, C, RET_HEAD_DIM)),
                  const((1, RET_WIDTH))] + [slab(w) for w in cast_weights],
        out_specs=[tile(RET_WIDTH),
                   pl.BlockSpec((1, 1, MLA_HEADS * HEAD_SLAB, tm), lambda b, i: (b, i, 0, 0)),
                   tile(MLA_HEADS * HEAD_SLAB),
                   pl.BlockSpec((1, tm // TK, MLA_WIDTH, TK), lambda b, i: (b, i, 0, 0))]
        + [slab(w) for w in cast_weights],
        out_shape=[bf(RET_WIDTH),
                   jax.ShapeDtypeStruct((B, S // tm, MLA_HEADS * HEAD_SLAB, tm), BF16),
                   bf(MLA_HEADS * HEAD_SLAB),
                   jax.ShapeDtypeStruct((B, S // TK, MLA_WIDTH, TK), BF16)]
        + [jax.ShapeDtypeStruct(w.shape, BF16) for w in cast_weights],
        scratch_shapes=[pltpu.VMEM((H, RET_HEAD_DIM, RET_HEAD_DIM), F32),
                        pltpu.VMEM((D, IN_COLS_MAIN + LANES), BF16)],
        compiler_params=pltpu.CompilerParams(dimension_semantics=("arbitrary", "arbitrary"),
                                             vmem_limit_bytes=VMEM_LIMIT),
        name="in_proj",
    )(x, pre_mix_norm, w_in, place, q_norm, wuq_p, kv_norm, wuk_p, wvt,
      positions.reshape(B, 1, S), inv_r, inv_m,
      dmask, zeta_b, xi_b, ret_gn_w, *cast_weights)
    return res[:4], res[4:]


HEADS_PER_STEP = 2
PIECES = TQ // TK


def _attn_body(qt_ref, k_ref, vt_ref, o_ref, *scratch):
    n = 2 * HEADS_PER_STEP
    s_ref = {(i // HEADS_PER_STEP, i % HEADS_PER_STEP): scratch[i] for i in range(n)}
    mb_ref = {(i // HEADS_PER_STEP, i % HEADS_PER_STEP): scratch[n + i] for i in range(n)}
    nq = qt_ref.shape[1] // PIECES
    ones = jnp.ones((ONES_ROWS, TK), BF16)
    chains = [(hh, c) for c in range(PIECES) for hh in range(HEADS_PER_STEP)]
    row = lax.broadcasted_iota(jnp.int32, (TK, TK), 0)
    col = lax.broadcasted_iota(jnp.int32, (TK, TK), 1)
    tri = row <= col

    def score_piece(qstart, j, slot, hh, c):
        kstart = pl.multiple_of(j * TK, TK)
        cols = slice(hh * HEAD_SLAB, (hh + 1) * HEAD_SLAB)
        lanes = slice(c * TK, (c + 1) * TK)
        st = _dot(k_ref[0, pl.ds(kstart, TK), cols], qt_ref[0, qstart + c, cols, :])
        s_ref[slot, hh][:, lanes] = st
        mb_ref[slot, hh][:, lanes] = jnp.max(st, axis=0, keepdims=True)

    def acc_piece(j, slot, hh, c, state, masked):
        m, acc = state
        lanes = slice(c * TK, (c + 1) * TK)
        st = s_ref[slot, hh][:, lanes]
        if masked:
            st = jnp.where(tri, st, NEG_BIG)
            blk_max = jnp.max(st, axis=0, keepdims=True)
        else:
            blk_max = mb_ref[slot, hh][:, lanes]
        m_new = jnp.maximum(m, blk_max)
        p = jnp.exp2(st - m_new).astype(BF16)
        vte = jnp.concatenate([vt_ref[0, j, hh * MLA_V_DIM:(hh + 1) * MLA_V_DIM, :], ones], axis=0)
        return m_new, acc * jnp.exp2(m - m_new) + _dot(vte, p)

    def stage(carry, j, slot, first, diag, nxt_qstart, nxt_j, nxt_first):
        carry = dict(carry)
        for hh, c in chains:
            if c >= nxt_first:
                score_piece(nxt_qstart, nxt_j, 1 - slot, hh, c)
            if c >= first:
                carry[hh, c] = acc_piece(j, slot, hh, c, carry[hh, c], diag and c == first)
        return carry

    def tile(qi, _):
        qstart = qi * PIECES

        def pair(t, flat):
            carry = dict(zip(chains, flat))
            carry = stage(carry, 2 * t, 0, 0, False, qstart, 2 * t + 1, 0)
            carry = stage(carry, 2 * t + 1, 1, 0, False, qstart, 2 * t + 2, 0)
            return tuple(carry[ch] for ch in chains)

        init = tuple((jnp.full((1, TK), NEG_BIG, F32), jnp.zeros((MLA_V_DIM + ONES_ROWS, TK), F32))
                     for _ in chains)
        d0 = PIECES * qi
        carry = dict(zip(chains, lax.fori_loop(0, d0 // 2, pair, init)))
        for d in range(PIECES):
            if d + 1 < PIECES:
                carry = stage(carry, d0 + d, d % 2, d, True, qstart, d0 + d + 1, d + 1)
            else:
                carry = stage(carry, d0 + d, d % 2, d, True, jnp.minimum(qi + 1, nq - 1) * PIECES, 0, 0)
        for hh, c in chains:
            acc = carry[hh, c][1]
            o_ref[0, qi, hh * MLA_V_DIM:(hh + 1) * MLA_V_DIM, c * TK:(c + 1) * TK] = (
                acc[:MLA_V_DIM] / acc[MLA_V_DIM:MLA_V_DIM + 1]).astype(BF16)
        return 0

    for hh, c in chains:
        score_piece(0, 0, 0, hh, c)
    lax.fori_loop(0, nq, tile, 0)


def _mla_attention(qt, k, vt):
    B, S, _ = k.shape
    nkb = S // TK
    nq = S // TQ
    hp = HEADS_PER_STEP
    return pl.pallas_call(
        _attn_body,
        grid=(B, MLA_HEADS // hp),
        in_specs=[pl.BlockSpec((1, nkb, hp * HEAD_SLAB, TK), lambda b, g: (b, 0, g, 0)),
                  pl.BlockSpec((1, S, hp * HEAD_SLAB), lambda b, g: (b, 0, g)),
                  pl.BlockSpec((1, nkb, hp * MLA_V_DIM, TK), lambda b, g: (b, 0, g, 0))],
        out_specs=pl.BlockSpec((1, nq, hp * MLA_V_DIM, TQ), lambda b, g: (b, 0, g, 0)),
        out_shape=jax.ShapeDtypeStruct((B, nq, MLA_WIDTH, TQ), BF16),
        scratch_shapes=[pltpu.VMEM((TK, TQ), F32)] * (2 * hp) + [pltpu.VMEM((1, TQ), F32)] * (2 * hp),
        compiler_params=pltpu.CompilerParams(dimension_semantics=("arbitrary", "arbitrary"),
                                             vmem_limit_bytes=VMEM_LIMIT),
        name="mla_attn",
    )(qt, k, vt)


def _post_body(ret_ref, mla_ref, x_ref, p_ref, wo_ref, pmn_ref, pfn_ref, wg_ref, wu_ref, wd_ref,
               pofn_ref, wpp_ref, plen_ref, wpg_ref, bpg_ref, o_ref, act_ref):
    tm = x_ref.shape[1]
    halves = [slice(i * (tm // 4), (i + 1) * (tm // 4)) for i in range(4)]
    mix = [_dot(ret_ref[0, r], wo_ref[0:RET_WIDTH, :]) + _dot_tn(mla_ref[0, 0][:, r], wo_ref[RET_WIDTH:, :])
           for r in halves]
    h1 = [x_ref[0, r] + _rms(m, pmn_ref[...]) for r, m in zip(halves, mix)]
    hn = [_rms(h, pfn_ref[...]).astype(BF16) for h in h1]
    for c in range(D_FF // FF_CHUNK):
        cols = slice(c * FF_CHUNK, (c + 1) * FF_CHUNK)
        for r, hnr in zip(halves, hn):
            g = _dot(hnr, wg_ref[:, cols])
            u = _dot(hnr, wu_ref[:, cols])
            act_ref[r, cols] = (g * jax.nn.sigmoid(g) * u).astype(BF16)
    ff = [_dot(act_ref[r, :], wd_ref[...]) for r in halves]
    h2 = [h + _rms(f, pofn_ref[...]) for h, f in zip(h1, ff)]
    e = [_rms(_dot(p_ref[0, r].astype(BF16), wpp_ref[...]), plen_ref[...]) for r in halves]
    gate = [jax.nn.sigmoid(_dot(h.astype(BF16), wpg_ref[...]) + bpg_ref[...]) for h in h2]
    for r, h, ee, gg in zip(halves, h2, e, gate):
        o_ref[0, r] = h + ee * gg


def _post(ret_out, mla_out, x, p, w_o, post_mix_norm, pre_ffn_norm, w_gate, w_up, w_down,
          post_ffn_norm, w_ple_proj, ple_norm, w_ple_gate, b_ple_gate):
    B, S, D = x.shape
    tm = TM_POST
    const = lambda shape: pl.BlockSpec(shape, lambda b, i: (0,) * len(shape),
                                       pipeline_mode=pl.Buffered(1))
    tile = lambda w: pl.BlockSpec((1, tm, w), lambda b, i: (b, i, 0))
    return pl.pallas_call(
        _post_body,
        grid=(B, S // tm),
        in_specs=[tile(RET_WIDTH),
                  pl.BlockSpec((1, 1, MLA_WIDTH, tm), lambda b, i: (b, i // (TQ // tm), 0, i % (TQ // tm))),
                  tile(D), tile(PLE_DIM),
                  const((D, D)), const((1, D)), const((1, D)),
                  const((D, D_FF)), const((D, D_FF)), const((D_FF, D)), const((1, D)),
                  const((PLE_DIM, D)), const((1, D)), const((D, D)), const((1, D))],
        out_specs=tile(D),
        out_shape=jax.ShapeDtypeStruct((B, S, D), F32),
        scratch_shapes=[pltpu.VMEM((tm, D_FF), BF16)],
        compiler_params=pltpu.CompilerParams(dimension_semantics=("arbitrary", "arbitrary"),
                                             vmem_limit_bytes=VMEM_LIMIT),
        name="post",
    )(ret_out, mla_out, x, p, w_o, post_mix_norm, pre_ffn_norm, w_gate, w_up, w_down,
      post_ffn_norm, w_ple_proj, ple_norm, w_ple_gate, b_ple_gate)


def _pad_heads(w, per_head, keep):
    K = w.shape[0]
    w = w.reshape(K, MLA_HEADS, per_head)[:, :, :keep]
    w = jnp.pad(w, ((0, 0), (0, 0), (0, HEAD_SLAB - keep)))
    return w.reshape(K, MLA_HEADS * HEAD_SLAB).astype(BF16)


def kernel(x, p, positions, pre_mix_norm, w_in, ret_gn_w, mla_q_norm, w_uq, mla_kv_norm, w_ukv, w_o,
           post_mix_norm, pre_ffn_norm, w_gate, w_up, w_down, post_ffn_norm, w_ple_proj, ple_norm,
           w_ple_gate, b_ple_gate):
    depth = pre_mix_norm.shape[0]
    h = x
    for i in range(depth):
        big = (w_o[i], w_gate[i], w_up[i], w_down[i], w_ple_proj[i], w_ple_gate[i])
        wuq_p = _pad_heads(w_uq[i], MLA_QK_DIM, MLA_QK_DIM).T
        wuk_p = _pad_heads(w_ukv[i], MLA_NOPE_DIM + MLA_V_DIM, MLA_NOPE_DIM)
        wv = w_ukv[i].reshape(MLA_KV_LORA, MLA_HEADS, MLA_NOPE_DIM + MLA_V_DIM)[:, :, MLA_NOPE_DIM:]
        wvt = wv.reshape(MLA_KV_LORA, MLA_WIDTH).T.astype(BF16)
        (ret_out, q, k, vt), big = _in_proj(h, pre_mix_norm[i][None], w_in[i].T, mla_q_norm[i][None], wuq_p,
                                            mla_kv_norm[i][None], wuk_p, wvt, positions, ret_gn_w[i][None], big)
        wo_b, wg_b, wu_b, wd_b, wpp_b, wpg_b = big
        mla_out = _mla_attention(q, k, vt)
        h = _post(ret_out, mla_out, h, p[i], wo_b, post_mix_norm[i][None], pre_ffn_norm[i][None],
                  wg_b, wu_b, wd_b, post_ffn_norm[i][None], wpp_b, ple_norm[i][None], wpg_b,
                  b_ple_gate[i][None])
    return h
```
